```python
import math
import jax, jax.numpy as jnp
from jax import lax
import numpy as np

D_MODEL = 2048
BATCH = 4
SEQ = 2048
DEPTH = 4

N_MEM = 256
N_MIXERS = 3
EPS = 1e-6
D_FF = ((8 * D_MODEL // 3 + 127) // 128) * 128

HG_HEAD = 128
HG_HEADS = D_MODEL // HG_HEAD
HG_WIDTH = HG_HEADS * HG_HEAD
HG_CHUNK = 64
F_FLOOR = 1e-12

LRU_WIDTH = D_MODEL
LRU_BLOCKS = 8
LRU_BLOCK = LRU_WIDTH // LRU_BLOCKS
CONV_W = 4
LRU_C = 8.0

ML_HEADS = 8
ML_DQK = D_MODEL // (2 * ML_HEADS)
ML_DV = D_MODEL // ML_HEADS
ML_HQK = ML_HEADS * ML_DQK
ML_HV = ML_HEADS * ML_DV
ML_IN = 2 * ML_HQK + 2 * ML_HV + 2 * ML_HEADS
ML_SPLITS = (ML_HQK, 2 * ML_HQK, 2 * ML_HQK + ML_HV, 2 * ML_HQK + 2 * ML_HV, 2 * ML_HQK + 2 * ML_HV + ML_HEADS)
ML_CHUNK = 64
GATE_CAP = 15.0
NEG_BIG = -1e30

XA_HEADS = 4
XA_HEAD = D_MODEL // XA_HEADS

N_A = (DEPTH + N_MIXERS - 1) // N_MIXERS
N_B = (DEPTH + N_MIXERS - 2) // N_MIXERS
N_C = DEPTH // N_MIXERS

kernel_name = "hybrid_hgrn2_rglru_mlstm_macaron"


def rmsnorm(x, g):
    xf = x.astype(jnp.float32)
    y = xf * lax.rsqrt(jnp.mean(xf * xf, axis=-1, keepdims=True) + EPS)
    return (y * g.astype(jnp.float32)).astype(x.dtype)


def head_rmsnorm(h, g):
    hf = h.astype(jnp.float32)
    return hf * lax.rsqrt(jnp.mean(hf * hf, axis=-1, keepdims=True) + EPS) * g.astype(jnp.float32)


def swiglu(x, w_gu, w_down):
    gate, up = jnp.split(x @ w_gu, 2, axis=-1)
    return (jax.nn.silu(gate) * up) @ w_down


def _chunk(t, n_chunks, heads, d):
    b = t.shape[0]
    return t.reshape(b, n_chunks, -1, heads, d).transpose(1, 0, 3, 2, 4)


def _chunk_gate(t, n_chunks, heads):
    b = t.shape[0]
    return t.reshape(b, n_chunks, -1, heads).transpose(1, 0, 3, 2)


def _unchunk(t):
    n, b, h, c, d = t.shape
    return t.transpose(1, 0, 3, 2, 4).reshape(b, n * c, h, d)


def hgrn2_mixer(x, w_in, g_norm, w_out, lb):
    bsz, seq, _ = x.shape
    n = seq // HG_CHUNK
    proj = (x @ w_in).astype(jnp.float32)
    q, f_pre, v, g = jnp.split(proj, 4, axis=-1)
    q = jax.nn.silu(q)
    f = lb + (1.0 - lb) * jax.nn.sigmoid(f_pre)
    log_f = jnp.log(jnp.maximum(f, F_FLOOR))
    k = (1.0 - lb) * jax.nn.sigmoid(-f_pre)
    qc, kc, vc, lfc = (_chunk(t, n, HG_HEADS, HG_HEAD) for t in (q, k, v, log_f))
    causal = jnp.tril(jnp.ones((HG_CHUNK, HG_CHUNK), dtype=bool))[:, :, None]

    def step(state, inp):
        q_c, k_c, v_c, lf_c = inp
        b = jnp.cumsum(lf_c, axis=2)
        diff = b[:, :, :, None, :] - b[:, :, None, :, :]
        decay = jnp.where(causal, jnp.exp(jnp.where(causal, diff, 0.0)), 0.0)
        scores = jnp.einsum('bhtd,bhsd,bhtsd->bhts', q_c, k_c, decay)
        o = (jnp.einsum('bhts,bhsv->bhtv', scores, v_c)
             + jnp.einsum('bhtd,bhdv->bhtv', q_c * jnp.exp(b), state))
        b_last = b[:, :, -1:, :]
        new_state = (jnp.exp(b_last[:, :, 0, :])[..., None] * state
                     + jnp.einsum('bhsd,bhsv->bhdv', k_c * jnp.exp(b_last - b), v_c))
        return new_state, o

    s0 = jnp.zeros((bsz, HG_HEADS, HG_HEAD, HG_HEAD), jnp.float32)
    _, o = lax.scan(step, s0, (qc, kc, vc, lfc))
    o = head_rmsnorm(_unchunk(o), g_norm).reshape(bsz, seq, HG_WIDTH) * jax.nn.silu(g)
    return o.astype(x.dtype) @ w_out


def _lin_combine(left, right):
    a_l, b_l = left
    a_r, b_r = right
    return a_l * a_r, a_r * b_l + b_r


def rglru_mixer(x, w_in, conv_w, conv_b, w_a, b_a, w_x, b_x, lam, w_out):
    bsz, seq, _ = x.shape
    gate_branch, u = jnp.split(x @ w_in, 2, axis=-1)
    u = lax.conv_general_dilated(u, conv_w[:, None, :], window_strides=(1,),
                                 padding=[(CONV_W - 1, 0)],
                                 dimension_numbers=('NWC', 'WIO', 'NWC'),
                                 feature_group_count=LRU_WIDTH) + conv_b
    u = u.astype(jnp.float32)
    ub = u.reshape(bsz, seq, LRU_BLOCKS, LRU_BLOCK)
    r = jax.nn.sigmoid(jnp.einsum('bsnc,ncd->bsnd', ub, w_a.astype(jnp.float32)).reshape(bsz, seq, LRU_WIDTH)
                       + b_a.astype(jnp.float32))
    i = jax.nn.sigmoid(jnp.einsum('bsnc,ncd->bsnd', ub, w_x.astype(jnp.float32)).reshape(bsz, seq, LRU_WIDTH)
                       + b_x.astype(jnp.float32))
    log_a = -LRU_C * r * jax.nn.softplus(-lam.astype(jnp.float32))
    a = jnp.exp(log_a)
    inp = jnp.sqrt(jnp.maximum(-jnp.expm1(2.0 * log_a), 0.0)) * (i * u)
    _, h = lax.associative_scan(_lin_combine, (a, inp), axis=1)
    y = jax.nn.gelu(gate_branch.astype(jnp.float32)) * h
    return y.astype(x.dtype) @ w_out


def _softcap(t):
    return GATE_CAP * jnp.tanh(t / GATE_CAP)


def mlstm_mixer(x, w_in, b_if, g_norm, w_out):
    bsz, seq, _ = x.shape
    n = seq // ML_CHUNK
    proj = (x @ w_in).astype(jnp.float32)
    q, k, v, o, ig, fg = jnp.split(proj, ML_SPLITS, axis=-1)
    b_if = b_if.astype(jnp.float32)
    ig = _softcap(ig + b_if[0])
    log_f = jax.nn.log_sigmoid(_softcap(fg + b_if[1]))
    k = k * (ML_DQK ** -0.5)
    qc = _chunk(q, n, ML_HEADS, ML_DQK)
    kc = _chunk(k, n, ML_HEADS, ML_DQK)
    vc = _chunk(v, n, ML_HEADS, ML_DV)
    igc = _chunk_gate(ig, n, ML_HEADS)
    lfc = _chunk_gate(log_f, n, ML_HEADS)
    causal = jnp.tril(jnp.ones((ML_CHUNK, ML_CHUNK), dtype=bool))

    def step(carry, inp):
        c_st, n_st, m_st = carry
        q_c, k_c, v_c, i_c, lf_c = inp
        b = jnp.cumsum(lf_c, axis=-1)
        d_mat = jnp.where(causal, b[..., :, None] - b[..., None, :] + i_c[..., None, :], NEG_BIG)
        inter = b + m_st[..., None]
        m_t = jnp.maximum(inter, jnp.max(d_mat, axis=-1))
        w_intra = jnp.where(causal, jnp.exp(jnp.minimum(d_mat - m_t[..., None], 0.0)), 0.0)
        w_inter = jnp.exp(inter - m_t)
        qk = jnp.einsum('bhtd,bhsd->bhts', q_c, k_c) * w_intra
        num = (jnp.einsum('bhts,bhsv->bhtv', qk, v_c)
               + w_inter[..., None] * jnp.einsum('bhtd,bhdv->bhtv', q_c, c_st))
        den = jnp.sum(qk, axis=-1) + w_inter * jnp.einsum('bhtd,bhd->bht', q_c, n_st)
        h = num / jnp.maximum(jnp.abs(den), jnp.exp(-m_t))[..., None]
        g = b[..., -1]
        upd = g[..., None] - b + i_c
        m_new = jnp.maximum(g + m_st, jnp.max(upd, axis=-1))
        w_upd = jnp.exp(upd - m_new[..., None])
        decay = jnp.exp(g + m_st - m_new)
        c_new = decay[..., None, None] * c_st + jnp.einsum('bhs,bhsd,bhsv->bhdv', w_upd, k_c, v_c)
        n_new = decay[..., None] * n_st + jnp.einsum('bhs,bhsd->bhd', w_upd, k_c)
        return (c_new, n_new, m_new), h

    carry0 = (jnp.zeros((bsz, ML_HEADS, ML_DQK, ML_DV), jnp.float32),
              jnp.zeros((bsz, ML_HEADS, ML_DQK), jnp.float32),
              jnp.zeros((bsz, ML_HEADS), jnp.float32))
    _, h = lax.scan(step, carry0, (qc, kc, vc, igc, lfc))
    h = head_rmsnorm(_unchunk(h), g_norm).reshape(bsz, seq, ML_HV) * jax.nn.sigmoid(o)
    return h.astype(x.dtype) @ w_out


def mem_cross_attention(x, mem_n, w_q, w_kv, w_o):
    bsz, seq, _ = x.shape
    q = (x @ w_q).reshape(bsz, seq, XA_HEADS, XA_HEAD)
    k, v = jnp.split(mem_n @ w_kv, 2, axis=-1)
    k = k.reshape(bsz, -1, XA_HEADS, XA_HEAD)
    v = v.reshape(bsz, -1, XA_HEADS, XA_HEAD)
    s = jnp.einsum('bshd,bmhd->bhsm', q, k).astype(jnp.float32) * (XA_HEAD ** -0.5)
    p = jax.nn.softmax(s, axis=-1).astype(v.dtype)
    o = jnp.einsum('bhsm,bmhd->bshd', p, v).reshape(bsz, seq, D_MODEL)
    return o @ w_o


def setup_inputs(seed: int = 0) -> dict:
    key = jax.random.key(seed)
    ks = iter(jax.random.split(key, 40))

    def nrm(shape, scale):
        return jax.random.normal(next(ks), shape, jnp.float32) * scale

    x = nrm((BATCH, SEQ, D_MODEL), 1.0)
    mem = nrm((BATCH, N_MEM, D_MODEL), 1.0)
    mem_norm_g = 1.0 + nrm((D_MODEL,), 0.02)
    norm_g = 1.0 + nrm((DEPTH, 4, D_MODEL), 0.02)
    final_norm_g = 1.0 + nrm((D_MODEL,), 0.02)
    ffn_w_gu = nrm((DEPTH, 2, D_MODEL, 2 * D_FF), D_MODEL ** -0.5)
    ffn_w_down = nrm((DEPTH, 2, D_FF, D_MODEL), D_FF ** -0.5)
    xa_w_q = nrm((DEPTH, D_MODEL, D_MODEL), D_MODEL ** -0.5)
    xa_w_kv = nrm((DEPTH, D_MODEL, 2 * D_MODEL), D_MODEL ** -0.5)
    xa_w_o = nrm((DEPTH, D_MODEL, D_MODEL), D_MODEL ** -0.5)
    hg_lb_param = nrm((DEPTH, HG_WIDTH), 0.1)
    hg_w_in = nrm((N_A, D_MODEL, 4 * HG_WIDTH), D_MODEL ** -0.5)
    hg_g_norm = 1.0 + nrm((N_A, HG_HEAD), 0.02)
    hg_w_out = nrm((N_A, HG_WIDTH, D_MODEL), HG_WIDTH ** -0.5)
    lru_w_in = nrm((N_B, D_MODEL, 2 * LRU_WIDTH), D_MODEL ** -0.5)
    lru_conv_w = nrm((N_B, CONV_W, LRU_WIDTH), CONV_W ** -0.5)
    lru_conv_b = nrm((N_B, LRU_WIDTH), 0.01)
    lru_w_a = nrm((N_B, LRU_BLOCKS, LRU_BLOCK, LRU_BLOCK), LRU_BLOCK ** -0.5)
    lru_b_a = nrm((N_B, LRU_WIDTH), 0.01)
    lru_w_x = nrm((N_B, LRU_BLOCKS, LRU_BLOCK, LRU_BLOCK), LRU_BLOCK ** -0.5)
    lru_b_x = nrm((N_B, LRU_WIDTH), 0.01)
    a0 = jax.random.uniform(next(ks), (N_B, LRU_WIDTH), jnp.float32, minval=0.9, maxval=0.999)
    lru_lambda = jnp.log(a0) - jnp.log1p(-a0)
    lru_w_out = nrm((N_B, LRU_WIDTH, D_MODEL), LRU_WIDTH ** -0.5)
    ml_w_in = nrm((N_C, D_MODEL, ML_IN), D_MODEL ** -0.5)
    ig_bias = nrm((N_C, ML_HEADS), 0.1)
    fg_bias = jnp.linspace(3.0, 6.0, ML_HEADS, dtype=jnp.float32)[None, :] + nrm((N_C, ML_HEADS), 0.1)
    ml_b_if = jnp.stack([ig_bias, fg_bias], axis=1)
    ml_g_norm = 1.0 + nrm((N_C, ML_DV), 0.02)
    ml_w_out = nrm((N_C, ML_HV, D_MODEL), ML_HV ** -0.5)
    return {
        "x": x, "mem": mem, "mem_norm_g": mem_norm_g, "norm_g": norm_g, "final_norm_g": final_norm_g,
        "ffn_w_gu": ffn_w_gu, "ffn_w_down": ffn_w_down,
        "xa_w_q": xa_w_q, "xa_w_kv": xa_w_kv, "xa_w_o": xa_w_o,
        "hg_lb_param": hg_lb_param, "hg_w_in": hg_w_in, "hg_g_norm": hg_g_norm, "hg_w_out": hg_w_out,
        "lru_w_in": lru_w_in, "lru_conv_w": lru_conv_w, "lru_conv_b": lru_conv_b,
        "lru_w_a": lru_w_a, "lru_b_a": lru_b_a, "lru_w_x": lru_w_x, "lru_b_x": lru_b_x,
        "lru_lambda": lru_lambda, "lru_w_out": lru_w_out,
        "ml_w_in": ml_w_in, "ml_b_if": ml_b_if, "ml_g_norm": ml_g_norm, "ml_w_out": ml_w_out,
    }


def reference(x, mem, mem_norm_g, norm_g, final_norm_g, ffn_w_gu, ffn_w_down,
              xa_w_q, xa_w_kv, xa_w_o,
              hg_lb_param, hg_w_in, hg_g_norm, hg_w_out,
              lru_w_in, lru_conv_w, lru_conv_b, lru_w_a, lru_b_a, lru_w_x, lru_b_x, lru_lambda, lru_w_out,
              ml_w_in, ml_b_if, ml_g_norm, ml_w_out):
    mem_n = rmsnorm(mem, mem_norm_g)
    lb_p = jax.nn.softmax(hg_lb_param.astype(jnp.float32), axis=0)
    lb_all = jnp.cumsum(lb_p, axis=0) - lb_p[0]

    for layer in range(DEPTH):
        kind = layer % N_MIXERS
        idx = layer // N_MIXERS
        x = x + 0.5 * swiglu(rmsnorm(x, norm_g[layer, 0]), ffn_w_gu[layer, 0], ffn_w_down[layer, 0])
        h = rmsnorm(x, norm_g[layer, 1])
        if kind == 0:
            h = hgrn2_mixer(h, hg_w_in[idx], hg_g_norm[idx], hg_w_out[idx], lb_all[layer])
        elif kind == 1:
            h = rglru_mixer(h, lru_w_in[idx], lru_conv_w[idx], lru_conv_b[idx], lru_w_a[idx], lru_b_a[idx],
                            lru_w_x[idx], lru_b_x[idx], lru_lambda[idx], lru_w_out[idx])
        else:
            h = mlstm_mixer(h, ml_w_in[idx], ml_b_if[idx], ml_g_norm[idx], ml_w_out[idx])
        x = x + h
        x = x + mem_cross_attention(rmsnorm(x, norm_g[layer, 2]), mem_n,
                                    xa_w_q[layer], xa_w_kv[layer], xa_w_o[layer])
        x = x + 0.5 * swiglu(rmsnorm(x, norm_g[layer, 3]), ffn_w_gu[layer, 1], ffn_w_down[layer, 1])
    return rmsnorm(x, final_norm_g)
```

```python
import functools
import math

import jax
import jax.numpy as jnp
from jax import lax
from jax.experimental import pallas as pl
from jax.experimental.pallas import tpu as pltpu

EPS = 1e-6
N_MIXERS = 3
HG_HEAD = 128
HG_CHUNK = 64
HG_SUB = 16
F_FLOOR = 1e-12
LRU_BLOCKS = 8
CONV_W = 4
LRU_C = 8.0
ML_HEADS = 8
ML_CHUNK = 256
GATE_CAP = 15.0
NEG_BIG = -1e30
XA_HEADS = 4
LANE = 128
FF_TILE = 512
VMEM_LIMIT = 56 * 1024 * 1024

_BF16 = jnp.bfloat16
_F32 = jnp.float32


def _params(*sem):
    return pltpu.CompilerParams(dimension_semantics=sem, vmem_limit_bytes=VMEM_LIMIT)


def _tile(dim, pref):
    t = min(dim, pref)
    while dim % t:
        t //= 2
    return t


def _sigmoid(x):
    return 1.0 / (1.0 + jnp.exp(-x))


def _dot(a, b):
    return jnp.dot(a, b, preferred_element_type=_F32)


def _dot_nt(a, b):
    return lax.dot_general(a, b, (((1,), (1,)), ((), ())), preferred_element_type=_F32)


def _dot_tn(a, b):
    return lax.dot_general(a, b, (((0,), (0,)), ((), ())), preferred_element_type=_F32)


def _split3(x):
    hi = x.astype(_BF16)
    r1 = x - hi.astype(_F32)
    mid = r1.astype(_BF16)
    lo = (r1 - mid.astype(_F32)).astype(_BF16)
    return hi, mid, lo


def _dot_exact_lhs(sel, x):
    hi, mid, lo = _split3(x)
    return _dot(sel, hi) + _dot(sel, mid) + _dot(sel, lo)


def _rms(x, g):
    return x * lax.rsqrt(jnp.mean(x * x, axis=-1, keepdims=True) + EPS) * g


def _rms_kernel(x_ref, g_ref, o_ref):
    o_ref[...] = _rms(x_ref[...], g_ref[...]).astype(o_ref.dtype)


def rms_cast(x, g):
    m, d = x.shape
    tm = _tile(m, 512)
    return pl.pallas_call(
        _rms_kernel,
        grid=(m // tm,),
        in_specs=[pl.BlockSpec((tm, d), lambda i: (i, 0)),
                  pl.BlockSpec((1, d), lambda i: (0, 0))],
        out_specs=pl.BlockSpec((tm, d), lambda i: (i, 0)),
        out_shape=jax.ShapeDtypeStruct((m, d), _BF16),
        compiler_params=_params("parallel"),
        name="rms_cast",
    )(x, g.reshape(1, d))


def _mm_kernel(a_ref, w_ref, o_ref):
    o_ref[...] = _dot(a_ref[...], w_ref[...]).astype(o_ref.dtype)


def mm_plain(a, w, out_dtype, tm=1024, tn=1024, name="mm_plain"):
    m, k = a.shape
    n = w.shape[1]
    tm = _tile(m, tm)
    tn = _tile(n, tn)
    return pl.pallas_call(
        _mm_kernel,
        grid=(m // tm, n // tn),
        in_specs=[pl.BlockSpec((tm, k), lambda i, j: (i, 0)),
                  pl.BlockSpec((k, tn), lambda i, j: (0, j))],
        out_specs=pl.BlockSpec((tm, tn), lambda i, j: (i, j)),
        out_shape=jax.ShapeDtypeStruct((m, n), out_dtype),
        compiler_params=_params("parallel", "parallel"),
        name=name,
    )(a, w)


def _swiglu_kernel(a_ref, w_ref, o_ref, *, tnh):
    acc = _dot(a_ref[...], w_ref[...])
    gate = acc[:, :tnh]
    up = acc[:, tnh:]
    o_ref[...] = (gate * _sigmoid(gate) * up).astype(o_ref.dtype)


def mm_swiglu(a, w_gu_tiled, tnh):
    m, k = a.shape
    n2 = w_gu_tiled.shape[1]
    nh = n2 // 2
    tm = _tile(m, 1024)
    return pl.pallas_call(
        functools.partial(_swiglu_kernel, tnh=tnh),
        grid=(m // tm, nh // tnh),
        in_specs=[pl.BlockSpec((tm, k), lambda i, j: (i, 0)),
                  pl.BlockSpec((k, 2 * tnh), lambda i, j: (0, j))],
        out_specs=pl.BlockSpec((tm, tnh), lambda i, j: (i, j)),
        out_shape=jax.ShapeDtypeStruct((m, nh), _BF16),
        compiler_params=_params("parallel", "parallel"),
        name="mm_swiglu",
    )(a, w_gu_tiled)


def _mm_resid_kernel(a_ref, w_ref, r_ref, g_ref, *rest, scale, nk, emit_x, norm_dtype):
    if emit_x:
        x_ref, n_ref = rest[0], rest[1]
        scratch = rest[2:]
    else:
        x_ref, n_ref = None, rest[0]
        scratch = rest[1:]

    def finish(acc):
        x = r_ref[...] + scale * acc
        if emit_x:
            x_ref[...] = x
        n_ref[...] = _rms(x, g_ref[...]).astype(norm_dtype)

    part = _dot(a_ref[...], w_ref[...])
    if nk == 1:
        finish(part)
        return
    acc_ref = scratch[0]
    kk = pl.program_id(1)

    @pl.when(kk == 0)
    def _():
        acc_ref[...] = part

    @pl.when(jnp.logical_and(kk > 0, kk < nk - 1))
    def _():
        acc_ref[...] += part

    @pl.when(kk == nk - 1)
    def _():
        finish(acc_ref[...] + part)


def mm_resid_norm(a, w, resid, g, scale, *, emit_x=True, norm_dtype=_BF16, tk=1024, name="mm_resid"):
    m, k = a.shape
    d = w.shape[1]
    tm = _tile(m, 512)
    tk = tk if k % tk == 0 else k
    nk = k // tk
    out_shape = []
    out_specs = []
    if emit_x:
        out_shape.append(jax.ShapeDtypeStruct((m, d), _F32))
        out_specs.append(pl.BlockSpec((tm, d), lambda i, kk: (i, 0)))
    out_shape.append(jax.ShapeDtypeStruct((m, d), norm_dtype))
    out_specs.append(pl.BlockSpec((tm, d), lambda i, kk: (i, 0)))
    res = pl.pallas_call(
        functools.partial(_mm_resid_kernel, scale=scale, nk=nk, emit_x=emit_x, norm_dtype=norm_dtype),
        grid=(m // tm, nk),
        in_specs=[pl.BlockSpec((tm, tk), lambda i, kk: (i, kk)),
                  pl.BlockSpec((tk, d), lambda i, kk: (kk, 0)),
                  pl.BlockSpec((tm, d), lambda i, kk: (i, 0)),
                  pl.BlockSpec((1, d), lambda i, kk: (0, 0))],
        out_specs=out_specs,
        out_shape=out_shape,
        scratch_shapes=[pltpu.VMEM((tm, d), _F32)] if nk > 1 else [],
        compiler_params=_params("parallel", "arbitrary"),
        name=name,
    )(a, w, resid, g.reshape(1, d))
    if emit_x:
        return res[0], res[1]
    return None, res[0]


def _lb_kernel(p_ref, o_ref):
    p = p_ref[...]
    depth = p.shape[0]
    e = jnp.exp(p - jnp.max(p, axis=0, keepdims=True))
    sm = e / jnp.sum(e, axis=0, keepdims=True)
    run = jnp.zeros_like(sm[0:1])
    for layer in range(depth):
        run = run + sm[layer:layer + 1]
        o_ref[layer:layer + 1, :] = run - sm[0:1]


def hgrn_lower_bounds(p):
    return pl.pallas_call(
        _lb_kernel,
        out_shape=jax.ShapeDtypeStruct(p.shape, _F32),
        name="hgrn_lb",
    )(p)


def _hgrn_kernel(q_ref, f_ref, v_ref, g_ref, lb_ref, gn_ref, o_ref, st_ref, *, nchunk):
    c = HG_CHUNK
    sb = HG_SUB
    nsb = c // sb

    @pl.when(pl.program_id(2) == 0)
    def _():
        st_ref[...] = jnp.zeros_like(st_ref)

    lb = lb_ref[...]
    gn = gn_ref[...]
    row = lax.broadcasted_iota(jnp.int32, (c, c), 0)
    col = lax.broadcasted_iota(jnp.int32, (c, c), 1)
    tril = (row >= col).astype(_BF16)
    t_sub = lax.broadcasted_iota(jnp.int32, (sb, HG_HEAD), 0)
    lane_sc = lax.broadcasted_iota(jnp.int32, (sb, c), 1)
    t_sc = lax.broadcasted_iota(jnp.int32, (sb, c), 0)

    def chunk(ci, carry):
        r0 = pl.multiple_of(ci * c, c)
        qp = q_ref[pl.ds(r0, c), :]
        fp = f_ref[pl.ds(r0, c), :]
        v = v_ref[pl.ds(r0, c), :]
        gp = g_ref[pl.ds(r0, c), :]
        q = qp * _sigmoid(qp)
        f = lb + (1.0 - lb) * _sigmoid(fp)
        lf = jnp.log(jnp.maximum(f, F_FLOOR))
        k = (1.0 - lb) * _sigmoid(-fp)
        b = _dot_exact_lhs(tril, lf)
        b_last = b[c - 1:c, :]
        st = st_ref[...]
        v16 = v.astype(_BF16)

        blocks = []
        for bi in range(nsb):
            lo = bi * sb
            b_i = b[lo:lo + sb, :]
            q_i = q[lo:lo + sb, :]
            k_i = k[lo:lo + sb, :]
            if bi == 0:
                sc = jnp.zeros((sb, c), _F32)
            else:
                ref = b[lo - 1:lo, :]
                qt = q_i * jnp.exp(b_i - ref)
                kt = k * jnp.exp(jnp.minimum(ref - b, 0.0))
                sc = _dot_nt(qt.astype(_BF16), kt.astype(_BF16))
                sc = jnp.where(lane_sc < lo, sc, 0.0)
            for s in range(sb):
                keep = t_sub >= s
                diff = jnp.where(keep, b_i - b_i[s:s + 1, :], 0.0)
                w = jnp.where(keep, jnp.exp(diff), 0.0)
                colv = jnp.sum(q_i * w * k_i[s:s + 1, :], axis=-1, keepdims=True)
                sc = jnp.where(lane_sc == lo + s, colv, sc)
            blocks.append(sc)
        scores = jnp.concatenate(blocks, axis=0)

        q_state = (q * jnp.exp(b)).astype(_BF16)
        o = _dot(scores.astype(_BF16), v16) + _dot_nt(q_state, st.astype(_BF16))
        k_state = (k * jnp.exp(b_last - b)).astype(_BF16)
        st_ref[...] = st * jnp.exp(b_last) + _dot_tn(v16, k_state)

        on = o * lax.rsqrt(jnp.mean(o * o, axis=-1, keepdims=True) + EPS) * gn
        o_ref[pl.ds(r0, c), :] = (on * (gp * _sigmoid(gp))).astype(o_ref.dtype)
        return carry

    lax.fori_loop(0, nchunk, chunk, 0)


def hgrn_core(proj, lb, g_norm, bsz, seq):
    m = proj.shape[0]
    width = proj.shape[1] // 4
    heads = width // HG_HEAD
    tb = _tile(seq, 512)
    nsblk = seq // tb

    def spec(off):
        return pl.BlockSpec((tb, HG_HEAD), lambda bb, h, s: (bb * nsblk + s, off * heads + h))

    return pl.pallas_call(
        functools.partial(_hgrn_kernel, nchunk=tb // HG_CHUNK),
        grid=(bsz, heads, nsblk),
        in_specs=[spec(0), spec(1), spec(2), spec(3),
                  pl.BlockSpec((1, HG_HEAD), lambda bb, h, s: (0, h)),
                  pl.BlockSpec((1, HG_HEAD), lambda bb, h, s: (0, 0))],
        out_specs=pl.BlockSpec((tb, HG_HEAD), lambda bb, h, s: (bb * nsblk + s, h)),
        out_shape=jax.ShapeDtypeStruct((m, width), _BF16),
        scratch_shapes=[pltpu.VMEM((HG_HEAD, HG_HEAD), _F32)],
        compiler_params=_params("parallel", "parallel", "arbitrary"),
        name="hgrn_core",
    )(proj, proj, proj, proj, lb.reshape(1, width), g_norm.reshape(1, HG_HEAD))


def _gelu_tanh(x):
    c0 = math.sqrt(2.0 / math.pi)
    return 0.5 * x * (1.0 + jnp.tanh(c0 * (x + 0.044715 * (x * x * x))))


def _lru_kernel(gate_ref, u_ref, cw_ref, cb_ref, wa_ref, ba_ref, wx_ref, bx_ref, lam_ref,
                o_ref, ext_ref, sa_ref, sx_ref, h_ref, *, ts, pad):
    width = u_ref.shape[1]
    blk = width // LRU_BLOCKS
    halo = 8

    @pl.when(pl.program_id(1) == 0)
    def _():
        ext_ref[0:halo, :] = jnp.zeros((halo, width), _F32)
        h_ref[...] = jnp.zeros_like(h_ref)

    @pl.when(pl.program_id(1) > 0)
    def _():
        ext_ref[0:halo, :] = ext_ref[ts:ts + halo, :]

    ext_ref[halo:halo + ts, :] = u_ref[...]
    uc = cb_ref[...] + cw_ref[0:1, :] * ext_ref[pl.ds(halo - 3, ts), :]
    for j in range(1, CONV_W):
        uc = uc + cw_ref[j:j + 1, :] * ext_ref[pl.ds(halo - 3 + j, ts), :]

    u16 = uc.astype(_BF16)
    r_parts = []
    i_parts = []
    for n in range(LRU_BLOCKS):
        ub = u16[:, n * blk:(n + 1) * blk]
        r_parts.append(_dot(ub, wa_ref[n]))
        i_parts.append(_dot(ub, wx_ref[n]))
    r = _sigmoid(jnp.concatenate(r_parts, axis=1) + ba_ref[...])
    ig = _sigmoid(jnp.concatenate(i_parts, axis=1) + bx_ref[...])

    lam = lam_ref[...]
    softplus = jnp.maximum(-lam, 0.0) + jnp.log(1.0 + jnp.exp(-jnp.abs(lam)))
    log_a = -LRU_C * r * softplus
    a = jnp.exp(log_a)
    inp = jnp.sqrt(jnp.maximum(1.0 - a * a, 0.0)) * (ig * uc)

    sa_ref[0:pad, :] = jnp.ones((pad, width), _F32)
    sx_ref[0:pad, :] = jnp.zeros((pad, width), _F32)
    shift = 1
    while shift < ts:
        sa_ref[pad:pad + ts, :] = a
        sx_ref[pad:pad + ts, :] = inp
        a_sh = sa_ref[pl.ds(pad - shift, ts), :]
        x_sh = sx_ref[pl.ds(pad - shift, ts), :]
        inp = a * x_sh + inp
        a = a * a_sh
        shift *= 2

    h = inp + a * h_ref[0:1, :]
    h_ref[0:1, :] = h[ts - 1:ts, :]
    o_ref[...] = (_gelu_tanh(gate_ref[...]) * h).astype(o_ref.dtype)


def lru_core(proj, conv_w, conv_b, w_a, b_a, w_x, b_x, lam, bsz, seq):
    m = proj.shape[0]
    width = proj.shape[1] // 2
    blk = width // LRU_BLOCKS
    ts = _tile(seq, 256)
    pad = ts // 2 if ts >= 16 else 8
    pad = max(pad, 8)
    nsblk = seq // ts
    vec = lambda t: t.reshape(1, width)
    row_spec = pl.BlockSpec((1, width), lambda bb, s: (0, 0))
    w_spec = pl.BlockSpec((LRU_BLOCKS, blk, blk), lambda bb, s: (0, 0, 0))
    return pl.pallas_call(
        functools.partial(_lru_kernel, ts=ts, pad=pad),
        grid=(bsz, nsblk),
        in_specs=[pl.BlockSpec((ts, width), lambda bb, s: (bb * nsblk + s, 0)),
                  pl.BlockSpec((ts, width), lambda bb, s: (bb * nsblk + s, 1)),
                  pl.BlockSpec((CONV_W, width), lambda bb, s: (0, 0)),
                  row_spec, w_spec, row_spec, w_spec, row_spec, row_spec],
        out_specs=pl.BlockSpec((ts, width), lambda bb, s: (bb * nsblk + s, 0)),
        out_shape=jax.ShapeDtypeStruct((m, width), _BF16),
        scratch_shapes=[pltpu.VMEM((ts + 8, width), _F32),
                        pltpu.VMEM((ts + pad, width), _F32),
                        pltpu.VMEM((ts + pad, width), _F32),
                        pltpu.VMEM((8, width), _F32)],
        compiler_params=_params("parallel", "arbitrary"),
        name="lru_core",
    )(proj, proj, conv_w, vec(conv_b), w_a, vec(b_a), w_x, vec(b_x), vec(lam))


def _softcap(t):
    return GATE_CAP * jnp.tanh(t / GATE_CAP)


def _log_sigmoid(x):
    return jnp.minimum(x, 0.0) - jnp.log(1.0 + jnp.exp(-jnp.abs(x)))


def _mlstm_kernel(q_ref, k_ref, v_ref, og_ref, gt_ref, bif_ref, gn_ref, o_ref,
                  c_ref, n_ref, m_ref, *, nchunk, c, dqk):
    h_id = pl.program_id(1)

    @pl.when(pl.program_id(2) == 0)
    def _():
        c_ref[...] = jnp.zeros_like(c_ref)
        n_ref[...] = jnp.zeros_like(n_ref)
        m_ref[...] = jnp.zeros_like(m_ref)

    row = lax.broadcasted_iota(jnp.int32, (c, c), 0)
    col = lax.broadcasted_iota(jnp.int32, (c, c), 1)
    causal = row >= col
    tril = causal.astype(_BF16)
    lane = lax.broadcasted_iota(jnp.int32, (8, LANE), 1)
    sel_i = (lane == h_id).astype(_BF16)
    sel_f = (lane == h_id + ML_HEADS).astype(_BF16)
    lane1 = lax.broadcasted_iota(jnp.int32, (1, LANE), 1)
    sel_i_row = (lane1 == h_id).astype(_F32)
    sel_f_row = (lane1 == h_id + ML_HEADS).astype(_F32)
    gn = gn_ref[...]
    bif = bif_ref[...]
    kscale = dqk ** -0.5

    def chunk(ci, carry):
        r0 = pl.multiple_of(ci * c, c)
        q = q_ref[pl.ds(r0, c), :]
        k = k_ref[pl.ds(r0, c), :] * kscale
        v = v_ref[pl.ds(r0, c), :]
        og = og_ref[pl.ds(r0, c), :]
        pre = _softcap(gt_ref[pl.ds(r0, c), :] + bif)
        is_f = jnp.logical_and(lane1 >= ML_HEADS, lane1 < 2 * ML_HEADS)
        gl = jnp.where(is_f, _log_sigmoid(pre), pre)
        cum = _dot_exact_lhs(tril, gl)
        b_col = jnp.sum(cum * sel_f_row, axis=-1, keepdims=True)
        i_col = jnp.sum(gl * sel_i_row, axis=-1, keepdims=True)
        ch, cm, cl = _split3(cum)
        b_row = (_dot_nt(sel_f, ch) + _dot_nt(sel_f, cm) + _dot_nt(sel_f, cl))[0:1, :]
        gh, gm, gl3 = _split3(gl)
        i_row = (_dot_nt(sel_i, gh) + _dot_nt(sel_i, gm) + _dot_nt(sel_i, gl3))[0:1, :]

        m_st = m_ref[...]
        c_st = c_ref[...]
        n_st = n_ref[...]

        d_mat = jnp.where(causal, b_col - b_row + i_row, NEG_BIG)
        inter = b_col + m_st
        m_t = jnp.maximum(inter, jnp.max(d_mat, axis=-1, keepdims=True))
        w_intra = jnp.where(causal, jnp.exp(jnp.minimum(d_mat - m_t, 0.0)), 0.0)
        w_inter = jnp.exp(inter - m_t)
        q16 = q.astype(_BF16)
        k16 = k.astype(_BF16)
        v16 = v.astype(_BF16)
        qk = _dot_nt(q16, k16) * w_intra
        num = _dot(qk.astype(_BF16), v16) + w_inter * _dot(q16, c_st.astype(_BF16))
        den = jnp.sum(qk, axis=-1, keepdims=True) + w_inter * jnp.sum(q * n_st, axis=-1, keepdims=True)
        hh = num / jnp.maximum(jnp.abs(den), jnp.exp(-m_t))

        g_tot = b_col[c - 1:c, :]
        upd_col = g_tot - b_col + i_col
        m_new = jnp.maximum(g_tot + m_st, jnp.max(upd_col, axis=0, keepdims=True))
        w_upd = jnp.exp(upd_col - m_new)
        decay = jnp.exp(g_tot + m_st - m_new)
        kw = k * w_upd
        c_ref[...] = decay * c_st + _dot_tn(kw.astype(_BF16), v16)
        n_ref[...] = decay * n_st + jnp.sum(kw, axis=0, keepdims=True)
        m_ref[...] = m_new

        hn = hh * lax.rsqrt(jnp.mean(hh * hh, axis=-1, keepdims=True) + EPS) * gn
        o_ref[pl.ds(r0, c), :] = (hn * _sigmoid(og)).astype(o_ref.dtype)
        return carry

    lax.fori_loop(0, nchunk, chunk, 0)


def mlstm_core(proj, b_if, g_norm, bsz, seq, dqk, dv):
    m = proj.shape[0]
    heads = ML_HEADS
    hv = heads * dv
    c = _tile(seq, ML_CHUNK)
    tb = _tile(seq, 512)
    nsblk = seq // tb
    gate_blk = (2 * heads * dqk + 2 * hv) // LANE
    bif = jnp.zeros((1, LANE), _F32).at[0, :2 * heads].set(b_if.reshape(-1))
    r = dv // dqk
    return pl.pallas_call(
        functools.partial(_mlstm_kernel, nchunk=tb // c, c=c, dqk=dqk),
        grid=(bsz, heads, nsblk),
        in_specs=[pl.BlockSpec((tb, dqk), lambda bb, h, s: (bb * nsblk + s, h)),
                  pl.BlockSpec((tb, dqk), lambda bb, h, s: (bb * nsblk + s, heads + h)),
                  pl.BlockSpec((tb, dv), lambda bb, h, s: (bb * nsblk + s, (2 * heads) // r + h)),
                  pl.BlockSpec((tb, dv), lambda bb, h, s: (bb * nsblk + s, (2 * heads) // r + heads + h)),
                  pl.BlockSpec((tb, LANE), lambda bb, h, s: (bb * nsblk + s, gate_blk)),
                  pl.BlockSpec((1, LANE), lambda bb, h, s: (0, 0)),
                  pl.BlockSpec((1, dv), lambda bb, h, s: (0, 0))],
        out_specs=pl.BlockSpec((tb, dv), lambda bb, h, s: (bb * nsblk + s, h)),
        out_shape=jax.ShapeDtypeStruct((m, hv), _BF16),
        scratch_shapes=[pltpu.VMEM((dqk, dv), _F32),
                        pltpu.VMEM((1, dqk), _F32),
                        pltpu.VMEM((1, 1), _F32)],
        compiler_params=_params("parallel", "parallel", "arbitrary"),
        name="mlstm_core",
    )(proj, proj, proj, proj, proj, bif, g_norm.reshape(1, dv))


def _xattn_kernel(q_ref, k_ref, v_ref, o_ref, *, heads):
    d = q_ref.shape[1]
    hd = d // heads
    scale = hd ** -0.5
    for h in range(heads):
        qh = q_ref[:, h * hd:(h + 1) * hd]
        kh = k_ref[:, h * hd:(h + 1) * hd]
        vh = v_ref[:, h * hd:(h + 1) * hd]
        s = _dot_nt(qh, kh) * scale
        s = s - jnp.max(s, axis=-1, keepdims=True)
        e = jnp.exp(s)
        p = e / jnp.sum(e, axis=-1, keepdims=True)
        o_ref[:, h * hd:(h + 1) * hd] = _dot(p.astype(_BF16), vh).astype(o_ref.dtype)


def xattn_core(q, kv, bsz, seq, n_mem):
    m, d = q.shape
    tq = _tile(seq, 512)
    nq = seq // tq
    return pl.pallas_call(
        functools.partial(_xattn_kernel, heads=XA_HEADS),
        grid=(bsz, nq),
        in_specs=[pl.BlockSpec((tq, d), lambda bb, s: (bb * nq + s, 0)),
                  pl.BlockSpec((n_mem, d), lambda bb, s: (bb, 0)),
                  pl.BlockSpec((n_mem, d), lambda bb, s: (bb, 1))],
        out_specs=pl.BlockSpec((tq, d), lambda bb, s: (bb * nq + s, 0)),
        out_shape=jax.ShapeDtypeStruct((m, d), _BF16),
        compiler_params=_params("parallel", "parallel"),
        name="xattn_core",
    )(q, kv, kv)


def _tile_gu(w_gu, d_ff, d_ff_pad, tnh):
    d = w_gu.shape[0]
    gate = jnp.pad(w_gu[:, :d_ff], ((0, 0), (0, d_ff_pad - d_ff)))
    up = jnp.pad(w_gu[:, d_ff:], ((0, 0), (0, d_ff_pad - d_ff)))
    nt = d_ff_pad // tnh
    both = jnp.stack([gate.reshape(d, nt, tnh), up.reshape(d, nt, tnh)], axis=2)
    return both.reshape(d, 2 * d_ff_pad).astype(_BF16)


def _pad_down(w_down, d_ff_pad):
    return jnp.pad(w_down, ((0, d_ff_pad - w_down.shape[0]), (0, 0))).astype(_BF16)


def kernel(x, mem, mem_norm_g, norm_g, final_norm_g, ffn_w_gu, ffn_w_down, xa_w_q, xa_w_kv, xa_w_o,
           hg_lb_param, hg_w_in, hg_g_norm, hg_w_out,
           lru_w_in, lru_conv_w, lru_conv_b, lru_w_a, lru_b_a, lru_w_x, lru_b_x, lru_lambda, lru_w_out,
           ml_w_in, ml_b_if, ml_g_norm, ml_w_out):
    bsz, seq, d = x.shape
    n_mem = mem.shape[1]
    depth = norm_g.shape[0]
    d_ff = ffn_w_down.shape[2]
    tnh = FF_TILE
    d_ff_pad = -(-d_ff // tnh) * tnh
    m = bsz * seq
    ml_dqk = d // (2 * ML_HEADS)
    ml_dv = d // ML_HEADS
    ml_main = 2 * ML_HEADS * ml_dqk + 2 * ML_HEADS * ml_dv

    xf = x.reshape(m, d)
    mem_n = rms_cast(mem.reshape(bsz * n_mem, d), mem_norm_g)
    lb_all = hgrn_lower_bounds(hg_lb_param)

    xn = rms_cast(xf, norm_g[0, 0])
    for layer in range(depth):
        kind = layer % N_MIXERS
        idx = layer // N_MIXERS
        last = layer == depth - 1

        h = mm_swiglu(xn, _tile_gu(ffn_w_gu[layer, 0], d_ff, d_ff_pad, tnh), tnh)
        xf, xn = mm_resid_norm(h, _pad_down(ffn_w_down[layer, 0], d_ff_pad), xf, norm_g[layer, 1], 0.5,
                               tk=d_ff_pad // 4, name="ffn_down")

        if kind == 0:
            proj = mm_plain(xn, hg_w_in[idx].astype(_BF16), _F32, name="hg_in")
            y = hgrn_core(proj, lb_all[layer], hg_g_norm[idx], bsz, seq)
            w_out = hg_w_out[idx]
        elif kind == 1:
            proj = mm_plain(xn, lru_w_in[idx].astype(_BF16), _F32, name="lru_in")
            y = lru_core(proj, lru_conv_w[idx], lru_conv_b[idx], lru_w_a[idx].astype(_BF16), lru_b_a[idx],
                         lru_w_x[idx].astype(_BF16), lru_b_x[idx], lru_lambda[idx], bsz, seq)
            w_out = lru_w_out[idx]
        else:
            w_in = ml_w_in[idx]
            w_in = jnp.pad(w_in, ((0, 0), (0, ml_main + LANE - w_in.shape[1]))).astype(_BF16)
            proj = mm_plain(xn, w_in, _F32, tn=896, name="ml_in")
            y = mlstm_core(proj, ml_b_if[idx], ml_g_norm[idx], bsz, seq, ml_dqk, ml_dv)
            w_out = ml_w_out[idx]
        xf, xn = mm_resid_norm(y, w_out.astype(_BF16), xf, norm_g[layer, 2], 1.0, name="mixer_out")

        q = mm_plain(xn, xa_w_q[layer].astype(_BF16), _BF16, name="xa_q")
        kv = mm_plain(mem_n, xa_w_kv[layer].astype(_BF16), _BF16, name="xa_kv")
        o = xattn_core(q, kv, bsz, seq, n_mem)
        xf, xn = mm_resid_norm(o, xa_w_o[layer].astype(_BF16), xf, norm_g[layer, 3], 1.0, name="xa_out")

        h = mm_swiglu(xn, _tile_gu(ffn_w_gu[layer, 1], d_ff, d_ff_pad, tnh), tnh)
        w_down = _pad_down(ffn_w_down[layer, 1], d_ff_pad)
        if last:
            _, out = mm_resid_norm(h, w_down, xf, final_norm_g, 0.5, emit_x=False, norm_dtype=_F32,
                                   tk=d_ff_pad // 4, name="ffn_down_final")
        else:
            xf, xn = mm_resid_norm(h, w_down, xf, norm_g[layer + 1, 0], 0.5,
                                   tk=d_ff_pad // 4, name="ffn_down")
    return out.reshape(bsz, seq, d)
```

```python
import functools
import math

import jax
import jax.numpy as jnp
from jax import lax
from jax.experimental import pallas as pl
from jax.experimental.pallas import tpu as pltpu

EPS = 1e-6
N_MIXERS = 3
HG_HEAD = 128
HG_CHUNK = 64
HG_SUB = 16
F_FLOOR = 1e-12
LRU_BLOCKS = 8
CONV_W = 4
LRU_C = 8.0
ML_HEADS = 8
ML_CHUNK = 256
GATE_CAP = 15.0
NEG_BIG = -1e30
XA_HEADS = 4
LANE = 128
FF_TILE = 512
VMEM_LIMIT = 56 * 1024 * 1024

_BF16 = jnp.bfloat16
_F32 = jnp.float32


def _params(*sem):
    return pltpu.CompilerParams(dimension_semantics=sem, vmem_limit_bytes=VMEM_LIMIT)


def _tile(dim, pref):
    t = min(dim, pref)
    while dim % t:
        t //= 2
    return t


def _sigmoid(x):
    return 1.0 / (1.0 + jnp.exp(-x))


def _dot(a, b):
    return jnp.dot(a, b, preferred_element_type=_F32)


def _dot_nt(a, b):
    return lax.dot_general(a, b, (((1,), (1,)), ((), ())), preferred_element_type=_F32)


def _dot_tn(a, b):
    return lax.dot_general(a, b, (((0,), (0,)), ((), ())), preferred_element_type=_F32)


def _split3(x):
    hi = x.astype(_BF16)
    r1 = x - hi.astype(_F32)
    mid = r1.astype(_BF16)
    lo = (r1 - mid.astype(_F32)).astype(_BF16)
    return hi, mid, lo


def _dot_exact_lhs(sel, x):
    hi, mid, lo = _split3(x)
    return _dot(sel, hi) + _dot(sel, mid) + _dot(sel, lo)


def _rms(x, g):
    return x * lax.rsqrt(jnp.mean(x * x, axis=-1, keepdims=True) + EPS) * g


def _rms_kernel(x_ref, g_ref, o_ref):
    o_ref[...] = _rms(x_ref[...], g_ref[...]).astype(o_ref.dtype)


def rms_cast(x, g, out_dtype=_BF16):
    m, d = x.shape
    tm = _tile(m, 512)
    return pl.pallas_call(
        _rms_kernel,
        grid=(m // tm,),
        in_specs=[pl.BlockSpec((tm, d), lambda i: (i, 0)),
                  pl.BlockSpec((1, d), lambda i: (0, 0))],
        out_specs=pl.BlockSpec((tm, d), lambda i: (i, 0)),
        out_shape=jax.ShapeDtypeStruct((m, d), out_dtype),
        compiler_params=_params("parallel"),
        name="rms_cast",
    )(x, g.reshape(1, d))


def _wspec(widx, block, index_map, **kw):
    lead = tuple(widx)
    return pl.BlockSpec((None,) * len(lead) + tuple(block), lambda *g: lead + tuple(index_map(*g)), **kw)


def _mm_ws_kernel(a_ref, w_ref, o_ref, wbf_ref):
    @pl.when(pl.program_id(1) == 0)
    def _():
        wbf_ref[...] = w_ref[...].astype(_BF16)

    o_ref[...] = _dot(a_ref[...], wbf_ref[...]).astype(o_ref.dtype)


def mm_ws(a, w, widx, out_dtype, tm=1024, tn=1024, name="mm_ws"):
    m, k = a.shape
    n = w.shape[-1]
    tm = _tile(m, tm)
    nj = -(-n // tn)
    return pl.pallas_call(
        _mm_ws_kernel,
        grid=(nj, m // tm),
        in_specs=[pl.BlockSpec((tm, k), lambda j, i: (i, 0)),
                  _wspec(widx, (k, tn), lambda j, i: (0, j))],
        out_specs=pl.BlockSpec((tm, tn), lambda j, i: (i, j)),
        out_shape=jax.ShapeDtypeStruct((m, nj * tn), out_dtype),
        scratch_shapes=[pltpu.VMEM((k, tn), _BF16)],
        compiler_params=_params("arbitrary", "arbitrary"),
        name=name,
    )(a, w)


def _swiglu_kernel(a_ref, wg_ref, wu_ref, o_ref, wbf_ref, *, tnh):
    @pl.when(pl.program_id(1) == 0)
    def _():
        lead = (0,) * (len(wg_ref.shape) - 2)
        wbf_ref[:, :tnh] = wg_ref[lead].astype(_BF16)
        wbf_ref[:, tnh:] = wu_ref[lead].astype(_BF16)

    acc = _dot(a_ref[...], wbf_ref[...])
    gate = acc[:, :tnh]
    up = acc[:, tnh:]
    o_ref[...] = (gate * _sigmoid(gate) * up).astype(o_ref.dtype)


def mm_swiglu(a, w_gu, widx, tnh=FF_TILE):
    m, k = a.shape
    d_ff = w_gu.shape[-1] // 2
    tm = _tile(m, 1024)
    nj = -(-d_ff // tnh)

    lead = tuple(widx)
    lead_blk = (1,) * len(lead)

    assert d_ff % LANE == 0 and tnh % LANE == 0

    def start(j, base=0):
        return LANE * (base // LANE + jnp.minimum(j * (tnh // LANE), (d_ff - tnh) // LANE))

    def elem_spec(block, index_map):
        return pl.BlockSpec(tuple(pl.Element(b) for b in block), index_map)

    return pl.pallas_call(
        functools.partial(_swiglu_kernel, tnh=tnh),
        grid=(nj, m // tm),
        in_specs=[pl.BlockSpec((tm, k), lambda j, i: (i, 0)),
                  elem_spec(lead_blk + (k, tnh), lambda j, i: lead + (0, start(j))),
                  elem_spec(lead_blk + (k, tnh), lambda j, i: lead + (0, start(j, d_ff)))],
        out_specs=elem_spec((tm, tnh), lambda j, i: (i * tm, start(j))),
        out_shape=jax.ShapeDtypeStruct((m, d_ff), _BF16),
        scratch_shapes=[pltpu.VMEM((k, 2 * tnh), _BF16)],
        compiler_params=_params("arbitrary", "arbitrary"),
        name="mm_swiglu",
    )(a, w_gu, w_gu)


def _down_kernel(a_ref, w_ref, r_ref, o_ref, wbf_ref, *, scale):
    @pl.when(pl.program_id(1) == 0)
    def _():
        wbf_ref[...] = w_ref[...].astype(_BF16)

    o_ref[...] = r_ref[...] + scale * _dot(a_ref[...], wbf_ref[...])


def mm_down(a, w, widx, resid, scale, tm=512, tn=512, name="ffn_down"):
    m, k = a.shape
    d = w.shape[-1]
    tm = _tile(m, tm)
    tn = _tile(d, tn)
    return pl.pallas_call(
        functools.partial(_down_kernel, scale=scale),
        grid=(d // tn, m // tm),
        in_specs=[pl.BlockSpec((tm, k), lambda j, i: (i, 0)),
                  _wspec(widx, (k, tn), lambda j, i: (0, j)),
                  pl.BlockSpec((tm, tn), lambda j, i: (i, j))],
        out_specs=pl.BlockSpec((tm, tn), lambda j, i: (i, j)),
        out_shape=jax.ShapeDtypeStruct((m, d), _F32),
        scratch_shapes=[pltpu.VMEM((k, tn), _BF16)],
        compiler_params=_params("arbitrary", "arbitrary"),
        name=name,
    )(a, w, resid)


def _resid_norm_kernel(a_ref, w_ref, r_ref, g_ref, x_ref, n_ref, wbf_ref, *, scale):
    @pl.when(pl.program_id(0) == 0)
    def _():
        wbf_ref[...] = w_ref[...].astype(_BF16)

    x = r_ref[...] + scale * _dot(a_ref[...], wbf_ref[...])
    x_ref[...] = x
    n_ref[...] = _rms(x, g_ref[...]).astype(n_ref.dtype)


def mm_resid_norm(a, w, widx, resid, g, scale, name="mm_resid"):
    m, k = a.shape
    d = w.shape[-1]
    tm = _tile(m, 512)
    return pl.pallas_call(
        functools.partial(_resid_norm_kernel, scale=scale),
        grid=(m // tm,),
        in_specs=[pl.BlockSpec((tm, k), lambda i: (i, 0)),
                  _wspec(widx, (k, d), lambda i: (0, 0), pipeline_mode=pl.Buffered(1)),
                  pl.BlockSpec((tm, d), lambda i: (i, 0)),
                  pl.BlockSpec((1, d), lambda i: (0, 0))],
        out_specs=[pl.BlockSpec((tm, d), lambda i: (i, 0)),
                   pl.BlockSpec((tm, d), lambda i: (i, 0))],
        out_shape=[jax.ShapeDtypeStruct((m, d), _F32),
                   jax.ShapeDtypeStruct((m, d), _BF16)],
        scratch_shapes=[pltpu.VMEM((k, d), _BF16)],
        compiler_params=_params("arbitrary"),
        name=name,
    )(a, w, resid, g.reshape(1, d))


def _lb_kernel(p_ref, o_ref):
    p = p_ref[...]
    depth = p.shape[0]
    e = jnp.exp(p - jnp.max(p, axis=0, keepdims=True))
    sm = e / jnp.sum(e, axis=0, keepdims=True)
    run = jnp.zeros_like(sm[0:1])
    for layer in range(depth):
        run = run + sm[layer:layer + 1]
        o_ref[layer:layer + 1, :] = run - sm[0:1]


def hgrn_lower_bounds(p):
    return pl.pallas_call(
        _lb_kernel,
        out_shape=jax.ShapeDtypeStruct(p.shape, _F32),
        name="hgrn_lb",
    )(p)


def _hgrn_kernel(q_ref, f_ref, v_ref, g_ref, lb_ref, gn_ref, o_ref, st_ref, *, nchunk):
    c = HG_CHUNK
    sb = HG_SUB
    nsb = c // sb

    @pl.when(pl.program_id(2) == 0)
    def _():
        st_ref[...] = jnp.zeros_like(st_ref)

    lb = lb_ref[...]
    gn = gn_ref[...]
    row = lax.broadcasted_iota(jnp.int32, (c, c), 0)
    col = lax.broadcasted_iota(jnp.int32, (c, c), 1)
    tril = (row >= col).astype(_BF16)
    t_sub = lax.broadcasted_iota(jnp.int32, (sb, HG_HEAD), 0)
    lane_sc = lax.broadcasted_iota(jnp.int32, (sb, c), 1)
    t_sc = lax.broadcasted_iota(jnp.int32, (sb, c), 0)

    def chunk(ci, carry):
        r0 = pl.multiple_of(ci * c, c)
        qp = q_ref[pl.ds(r0, c), :]
        fp = f_ref[pl.ds(r0, c), :]
        v = v_ref[pl.ds(r0, c), :]
        gp = g_ref[pl.ds(r0, c), :]
        q = qp * _sigmoid(qp)
        f = lb + (1.0 - lb) * _sigmoid(fp)
        lf = jnp.log(jnp.maximum(f, F_FLOOR))
        k = (1.0 - lb) * _sigmoid(-fp)
        b = _dot_exact_lhs(tril, lf)
        b_last = b[c - 1:c, :]
        st = st_ref[...]
        v16 = v.astype(_BF16)

        blocks = []
        for bi in range(nsb):
            lo = bi * sb
            b_i = b[lo:lo + sb, :]
            q_i = q[lo:lo + sb, :]
            k_i = k[lo:lo + sb, :]
            if bi == 0:
                sc = jnp.zeros((sb, c), _F32)
            else:
                ref = b[lo - 1:lo, :]
                qt = q_i * jnp.exp(b_i - ref)
                kt = k * jnp.exp(jnp.minimum(ref - b, 0.0))
                sc = _dot_nt(qt.astype(_BF16), kt.astype(_BF16))
                sc = jnp.where(lane_sc < lo, sc, 0.0)
            for s in range(sb):
                keep = t_sub >= s
                diff = jnp.where(keep, b_i - b_i[s:s + 1, :], 0.0)
                w = jnp.where(keep, jnp.exp(diff), 0.0)
                colv = jnp.sum(q_i * w * k_i[s:s + 1, :], axis=-1, keepdims=True)
                sc = jnp.where(lane_sc == lo + s, colv, sc)
            blocks.append(sc)
        scores = jnp.concatenate(blocks, axis=0)

        q_state = (q * jnp.exp(b)).astype(_BF16)
        o = _dot(scores.astype(_BF16), v16) + _dot_nt(q_state, st.astype(_BF16))
        k_state = (k * jnp.exp(b_last - b)).astype(_BF16)
        st_ref[...] = st * jnp.exp(b_last) + _dot_tn(v16, k_state)

        on = o * lax.rsqrt(jnp.mean(o * o, axis=-1, keepdims=True) + EPS) * gn
        o_ref[pl.ds(r0, c), :] = (on * (gp * _sigmoid(gp))).astype(o_ref.dtype)
        return carry

    lax.fori_loop(0, nchunk, chunk, 0)


def hgrn_core(proj, lb, g_norm, bsz, seq):
    m = proj.shape[0]
    width = proj.shape[1] // 4
    heads = width // HG_HEAD
    tb = _tile(seq, 512)
    nsblk = seq // tb

    def spec(off):
        return pl.BlockSpec((tb, HG_HEAD), lambda bb, h, s: (bb * nsblk + s, off * heads + h))

    return pl.pallas_call(
        functools.partial(_hgrn_kernel, nchunk=tb // HG_CHUNK),
        grid=(bsz, heads, nsblk),
        in_specs=[spec(0), spec(1), spec(2), spec(3),
                  pl.BlockSpec((1, HG_HEAD), lambda bb, h, s: (0, h)),
                  pl.BlockSpec((1, HG_HEAD), lambda bb, h, s: (0, 0))],
        out_specs=pl.BlockSpec((tb, HG_HEAD), lambda bb, h, s: (bb * nsblk + s, h)),
        out_shape=jax.ShapeDtypeStruct((m, width), _BF16),
        scratch_shapes=[pltpu.VMEM((HG_HEAD, HG_HEAD), _F32)],
        compiler_params=_params("parallel", "parallel", "arbitrary"),
        name="hgrn_core",
    )(proj, proj, proj, proj, lb.reshape(1, width), g_norm.reshape(1, HG_HEAD))


def _gelu_tanh(x):
    c0 = math.sqrt(2.0 / math.pi)
    return 0.5 * x * (1.0 + jnp.tanh(c0 * (x + 0.044715 * (x * x * x))))


def _lru_kernel(gate_ref, u_ref, cw_ref, cb_ref, wa_ref, ba_ref, wx_ref, bx_ref, lam_ref,
                o_ref, ext_ref, sa_ref, sx_ref, h_ref, *, ts, pad):
    width = u_ref.shape[1]
    blk = width // LRU_BLOCKS
    halo = 8

    @pl.when(pl.program_id(1) == 0)
    def _():
        ext_ref[0:halo, :] = jnp.zeros((halo, width), _F32)
        h_ref[...] = jnp.zeros_like(h_ref)

    @pl.when(pl.program_id(1) > 0)
    def _():
        ext_ref[0:halo, :] = ext_ref[ts:ts + halo, :]

    ext_ref[halo:halo + ts, :] = u_ref[...]
    uc = cb_ref[...] + cw_ref[0:1, :] * ext_ref[pl.ds(halo - 3, ts), :]
    for j in range(1, CONV_W):
        uc = uc + cw_ref[j:j + 1, :] * ext_ref[pl.ds(halo - 3 + j, ts), :]

    u16 = uc.astype(_BF16)
    r_parts = []
    i_parts = []
    for n in range(LRU_BLOCKS):
        ub = u16[:, n * blk:(n + 1) * blk]
        r_parts.append(_dot(ub, wa_ref[n].astype(_BF16)))
        i_parts.append(_dot(ub, wx_ref[n].astype(_BF16)))
    r = _sigmoid(jnp.concatenate(r_parts, axis=1) + ba_ref[...])
    ig = _sigmoid(jnp.concatenate(i_parts, axis=1) + bx_ref[...])

    lam = lam_ref[...]
    softplus = jnp.maximum(-lam, 0.0) + jnp.log(1.0 + jnp.exp(-jnp.abs(lam)))
    log_a = -LRU_C * r * softplus
    a = jnp.exp(log_a)
    inp = jnp.sqrt(jnp.maximum(1.0 - a * a, 0.0)) * (ig * uc)

    sa_ref[0:pad, :] = jnp.ones((pad, width), _F32)
    sx_ref[0:pad, :] = jnp.zeros((pad, width), _F32)
    shift = 1
    while shift < ts:
        sa_ref[pad:pad + ts, :] = a
        sx_ref[pad:pad + ts, :] = inp
        a_sh = sa_ref[pl.ds(pad - shift, ts), :]
        x_sh = sx_ref[pl.ds(pad - shift, ts), :]
        inp = a * x_sh + inp
        a = a * a_sh
        shift *= 2

    h = inp + a * h_ref[0:1, :]
    h_ref[0:1, :] = h[ts - 1:ts, :]
    o_ref[...] = (_gelu_tanh(gate_ref[...]) * h).astype(o_ref.dtype)


def lru_core(proj, conv_w, conv_b, w_a, b_a, w_x, b_x, lam, bsz, seq):
    m = proj.shape[0]
    width = proj.shape[1] // 2
    blk = width // LRU_BLOCKS
    ts = _tile(seq, 256)
    pad = ts // 2 if ts >= 16 else 8
    pad = max(pad, 8)
    nsblk = seq // ts
    vec = lambda t: t.reshape(1, width)
    row_spec = pl.BlockSpec((1, width), lambda bb, s: (0, 0))
    w_spec = pl.BlockSpec((LRU_BLOCKS, blk, blk), lambda bb, s: (0, 0, 0))
    return pl.pallas_call(
        functools.partial(_lru_kernel, ts=ts, pad=pad),
        grid=(bsz, nsblk),
        in_specs=[pl.BlockSpec((ts, width), lambda bb, s: (bb * nsblk + s, 0)),
                  pl.BlockSpec((ts, width), lambda bb, s: (bb * nsblk + s, 1)),
                  pl.BlockSpec((CONV_W, width), lambda bb, s: (0, 0)),
                  row_spec, w_spec, row_spec, w_spec, row_spec, row_spec],
        out_specs=pl.BlockSpec((ts, width), lambda bb, s: (bb * nsblk + s, 0)),
        out_shape=jax.ShapeDtypeStruct((m, width), _BF16),
        scratch_shapes=[pltpu.VMEM((ts + 8, width), _F32),
                        pltpu.VMEM((ts + pad, width), _F32),
                        pltpu.VMEM((ts + pad, width), _F32),
                        pltpu.VMEM((8, width), _F32)],
        compiler_params=_params("parallel", "arbitrary"),
        name="lru_core",
    )(proj, proj, conv_w, vec(conv_b), w_a, vec(b_a), w_x, vec(b_x), vec(lam))


def _softcap(t):
    return GATE_CAP * jnp.tanh(t / GATE_CAP)


def _log_sigmoid(x):
    return jnp.minimum(x, 0.0) - jnp.log(1.0 + jnp.exp(-jnp.abs(x)))


def _mlstm_kernel(q_ref, k_ref, v_ref, og_ref, gt_ref, bif_ref, gn_ref, o_ref,
                  c_ref, n_ref, m_ref, *, nchunk, c, dqk):
    h_id = pl.program_id(1)

    @pl.when(pl.program_id(2) == 0)
    def _():
        c_ref[...] = jnp.zeros_like(c_ref)
        n_ref[...] = jnp.zeros_like(n_ref)
        m_ref[...] = jnp.zeros_like(m_ref)

    row = lax.broadcasted_iota(jnp.int32, (c, c), 0)
    col = lax.broadcasted_iota(jnp.int32, (c, c), 1)
    causal = row >= col
    tril = causal.astype(_BF16)
    lane = lax.broadcasted_iota(jnp.int32, (8, LANE), 1)
    sel_i = (lane == h_id).astype(_BF16)
    sel_f = (lane == h_id + ML_HEADS).astype(_BF16)
    lane1 = lax.broadcasted_iota(jnp.int32, (1, LANE), 1)
    sel_i_row = (lane1 == h_id).astype(_F32)
    sel_f_row = (lane1 == h_id + ML_HEADS).astype(_F32)
    gn = gn_ref[...]
    bif = bif_ref[...]
    kscale = dqk ** -0.5

    def chunk(ci, carry):
        r0 = pl.multiple_of(ci * c, c)
        q = q_ref[pl.ds(r0, c), :]
        k = k_ref[pl.ds(r0, c), :] * kscale
        v = v_ref[pl.ds(r0, c), :]
        og = og_ref[pl.ds(r0, c), :]
        gt = jnp.where(lane1 < 2 * ML_HEADS, gt_ref[pl.ds(r0, c), :], 0.0)
        pre = _softcap(gt + bif)
        is_f = jnp.logical_and(lane1 >= ML_HEADS, lane1 < 2 * ML_HEADS)
        gl = jnp.where(is_f, _log_sigmoid(pre), pre)
        cum = _dot_exact_lhs(tril, gl)
        b_col = jnp.sum(cum * sel_f_row, axis=-1, keepdims=True)
        i_col = jnp.sum(gl * sel_i_row, axis=-1, keepdims=True)
        ch, cm, cl = _split3(cum)
        b_row = (_dot_nt(sel_f, ch) + _dot_nt(sel_f, cm) + _dot_nt(sel_f, cl))[0:1, :]
        gh, gm, gl3 = _split3(gl)
        i_row = (_dot_nt(sel_i, gh) + _dot_nt(sel_i, gm) + _dot_nt(sel_i, gl3))[0:1, :]

        m_st = m_ref[...]
        c_st = c_ref[...]
        n_st = n_ref[...]

        d_mat = jnp.where(causal, b_col - b_row + i_row, NEG_BIG)
        inter = b_col + m_st
        m_t = jnp.maximum(inter, jnp.max(d_mat, axis=-1, keepdims=True))
        w_intra = jnp.where(causal, jnp.exp(jnp.minimum(d_mat - m_t, 0.0)), 0.0)
        w_inter = jnp.exp(inter - m_t)
        q16 = q.astype(_BF16)
        k16 = k.astype(_BF16)
        v16 = v.astype(_BF16)
        qk = _dot_nt(q16, k16) * w_intra
        num = _dot(qk.astype(_BF16), v16) + w_inter * _dot(q16, c_st.astype(_BF16))
        den = jnp.sum(qk, axis=-1, keepdims=True) + w_inter * jnp.sum(q * n_st, axis=-1, keepdims=True)
        hh = num / jnp.maximum(jnp.abs(den), jnp.exp(-m_t))

        g_tot = b_col[c - 1:c, :]
        upd_col = g_tot - b_col + i_col
        m_new = jnp.maximum(g_tot + m_st, jnp.max(upd_col, axis=0, keepdims=True))
        w_upd = jnp.exp(upd_col - m_new)
        decay = jnp.exp(g_tot + m_st - m_new)
        kw = k * w_upd
        c_ref[...] = decay * c_st + _dot_tn(kw.astype(_BF16), v16)
        n_ref[...] = decay * n_st + jnp.sum(kw, axis=0, keepdims=True)
        m_ref[...] = m_new

        hn = hh * lax.rsqrt(jnp.mean(hh * hh, axis=-1, keepdims=True) + EPS) * gn
        o_ref[pl.ds(r0, c), :] = (hn * _sigmoid(og)).astype(o_ref.dtype)
        return carry

    lax.fori_loop(0, nchunk, chunk, 0)


def mlstm_core(proj, b_if, g_norm, bsz, seq, dqk, dv):
    m = proj.shape[0]
    heads = ML_HEADS
    hv = heads * dv
    c = _tile(seq, ML_CHUNK)
    tb = _tile(seq, 512)
    nsblk = seq // tb
    gate_blk = (2 * heads * dqk + 2 * hv) // LANE
    bif = jnp.zeros((1, LANE), _F32).at[0, :2 * heads].set(b_if.reshape(-1))
    r = dv // dqk
    return pl.pallas_call(
        functools.partial(_mlstm_kernel, nchunk=tb // c, c=c, dqk=dqk),
        grid=(bsz, heads, nsblk),
        in_specs=[pl.BlockSpec((tb, dqk), lambda bb, h, s: (bb * nsblk + s, h)),
                  pl.BlockSpec((tb, dqk), lambda bb, h, s: (bb * nsblk + s, heads + h)),
                  pl.BlockSpec((tb, dv), lambda bb, h, s: (bb * nsblk + s, (2 * heads) // r + h)),
                  pl.BlockSpec((tb, dv), lambda bb, h, s: (bb * nsblk + s, (2 * heads) // r + heads + h)),
                  pl.BlockSpec((tb, LANE), lambda bb, h, s: (bb * nsblk + s, gate_blk)),
                  pl.BlockSpec((1, LANE), lambda bb, h, s: (0, 0)),
                  pl.BlockSpec((1, dv), lambda bb, h, s: (0, 0))],
        out_specs=pl.BlockSpec((tb, dv), lambda bb, h, s: (bb * nsblk + s, h)),
        out_shape=jax.ShapeDtypeStruct((m, hv), _BF16),
        scratch_shapes=[pltpu.VMEM((dqk, dv), _F32),
                        pltpu.VMEM((1, dqk), _F32),
                        pltpu.VMEM((1, 1), _F32)],
        compiler_params=_params("parallel", "parallel", "arbitrary"),
        name="mlstm_core",
    )(proj, proj, proj, proj, proj, bif, g_norm.reshape(1, dv))


def _xattn_kernel(q_ref, k_ref, v_ref, o_ref, *, heads):
    d = q_ref.shape[1]
    hd = d // heads
    scale = hd ** -0.5
    for h in range(heads):
        qh = q_ref[:, h * hd:(h + 1) * hd]
        kh = k_ref[:, h * hd:(h + 1) * hd]
        vh = v_ref[:, h * hd:(h + 1) * hd]
        s = _dot_nt(qh, kh) * scale
        s = s - jnp.max(s, axis=-1, keepdims=True)
        e = jnp.exp(s)
        p = e / jnp.sum(e, axis=-1, keepdims=True)
        o_ref[:, h * hd:(h + 1) * hd] = _dot(p.astype(_BF16), vh).astype(o_ref.dtype)


def xattn_core(q, kv, bsz, seq, n_mem):
    m, d = q.shape
    tq = _tile(seq, 512)
    nq = seq // tq
    return pl.pallas_call(
        functools.partial(_xattn_kernel, heads=XA_HEADS),
        grid=(bsz, nq),
        in_specs=[pl.BlockSpec((tq, d), lambda bb, s: (bb * nq + s, 0)),
                  pl.BlockSpec((n_mem, d), lambda bb, s: (bb, 0)),
                  pl.BlockSpec((n_mem, d), lambda bb, s: (bb, 1))],
        out_specs=pl.BlockSpec((tq, d), lambda bb, s: (bb * nq + s, 0)),
        out_shape=jax.ShapeDtypeStruct((m, d), _BF16),
        compiler_params=_params("parallel", "parallel"),
        name="xattn_core",
    )(q, kv, kv)


def kernel(x, mem, mem_norm_g, norm_g, final_norm_g, ffn_w_gu, ffn_w_down, xa_w_q, xa_w_kv, xa_w_o,
           hg_lb_param, hg_w_in, hg_g_norm, hg_w_out,
           lru_w_in, lru_conv_w, lru_conv_b, lru_w_a, lru_b_a, lru_w_x, lru_b_x, lru_lambda, lru_w_out,
           ml_w_in, ml_b_if, ml_g_norm, ml_w_out):
    bsz, seq, d = x.shape
    n_mem = mem.shape[1]
    depth = norm_g.shape[0]
    m = bsz * seq
    ml_dqk = d // (2 * ML_HEADS)
    ml_dv = d // ML_HEADS

    xf = x.reshape(m, d)
    mem_n = rms_cast(mem.reshape(bsz * n_mem, d), mem_norm_g)
    lb_all = hgrn_lower_bounds(hg_lb_param)

    xn = rms_cast(xf, norm_g[0, 0])
    for layer in range(depth):
        kind = layer % N_MIXERS
        idx = layer // N_MIXERS

        h = mm_swiglu(xn, ffn_w_gu, (layer, 0))
        xf = mm_down(h, ffn_w_down, (layer, 0), xf, 0.5)
        xn = rms_cast(xf, norm_g[layer, 1])

        if kind == 0:
            proj = mm_ws(xn, hg_w_in, (idx,), _F32, name="hg_in")
            y = hgrn_core(proj, lb_all[layer], hg_g_norm[idx], bsz, seq)
            w_out = hg_w_out
        elif kind == 1:
            proj = mm_ws(xn, lru_w_in, (idx,), _F32, name="lru_in")
            y = lru_core(proj, lru_conv_w[idx], lru_conv_b[idx], lru_w_a[idx], lru_b_a[idx],
                         lru_w_x[idx], lru_b_x[idx], lru_lambda[idx], bsz, seq)
            w_out = lru_w_out
        else:
            proj = mm_ws(xn, ml_w_in, (idx,), _F32, tn=896, name="ml_in")
            y = mlstm_core(proj, ml_b_if[idx], ml_g_norm[idx], bsz, seq, ml_dqk, ml_dv)
            w_out = ml_w_out
        xf, xn = mm_resid_norm(y, w_out, (idx,), xf, norm_g[layer, 2], 1.0, name="mixer_out")

        q = mm_ws(xn, xa_w_q, (layer,), _BF16, name="xa_q")
        kv = mm_ws(mem_n, xa_w_kv, (layer,), _BF16, name="xa_kv")
        o = xattn_core(q, kv, bsz, seq, n_mem)
        xf, xn = mm_resid_norm(o, xa_w_o, (layer,), xf, norm_g[layer, 3], 1.0, name="xa_out")

        h = mm_swiglu(xn, ffn_w_gu, (layer, 1))
        xf = mm_down(h, ffn_w_down, (layer, 1), xf, 0.5)
        if layer + 1 < depth:
            xn = rms_cast(xf, norm_g[layer + 1, 0])
    out = rms_cast(xf, final_norm_g, _F32)
    return out.reshape(bsz, seq, d)
```

```python
import functools
import math

import jax
import jax.numpy as jnp
from jax import lax
from jax.experimental import pallas as pl
from jax.experimental.pallas import tpu as pltpu

EPS = 1e-6
N_MIXERS = 3
HG_HEAD = 128
HG_CHUNK = 64
HG_SUB = 16
HG_GROUP = 4
F_FLOOR = 1e-12
LRU_BLOCKS = 8
CONV_W = 4
LRU_C = 8.0
ML_HEADS = 8
ML_CHUNK = 256
GATE_CAP = 15.0
NEG_BIG = -1e30
XA_HEADS = 4
LANE = 128
FF_TILE = 512
VMEM_LIMIT = 56 * 1024 * 1024

_BF16 = jnp.bfloat16
_F32 = jnp.float32


def _params(*sem):
    return pltpu.CompilerParams(dimension_semantics=sem, vmem_limit_bytes=VMEM_LIMIT)


def _tile(dim, pref):
    t = min(dim, pref)
    while dim % t:
        t //= 2
    return t


def _sigmoid(x):
    return 1.0 / (1.0 + jnp.exp(-x))


def _dot(a, b):
    return jnp.dot(a, b, preferred_element_type=_F32)


def _dot_nt(a, b):
    return lax.dot_general(a, b, (((1,), (1,)), ((), ())), preferred_element_type=_F32)


def _dot_tn(a, b):
    return lax.dot_general(a, b, (((0,), (0,)), ((), ())), preferred_element_type=_F32)


def _split3(x):
    hi = x.astype(_BF16)
    r1 = x - hi.astype(_F32)
    mid = r1.astype(_BF16)
    lo = (r1 - mid.astype(_F32)).astype(_BF16)
    return hi, mid, lo


def _dot_exact_lhs(sel, x):
    hi, mid, lo = _split3(x)
    return _dot(sel, hi) + _dot(sel, mid) + _dot(sel, lo)


def _rms(x, g):
    return x * lax.rsqrt(jnp.mean(x * x, axis=-1, keepdims=True) + EPS) * g


def _rms_kernel(x_ref, g_ref, o_ref):
    o_ref[...] = _rms(x_ref[...], g_ref[...]).astype(o_ref.dtype)


def rms_cast(x, g, out_dtype=_BF16):
    m, d = x.shape
    tm = _tile(m, 512)
    return pl.pallas_call(
        _rms_kernel,
        grid=(m // tm,),
        in_specs=[pl.BlockSpec((tm, d), lambda i: (i, 0)),
                  pl.BlockSpec((1, d), lambda i: (0, 0))],
        out_specs=pl.BlockSpec((tm, d), lambda i: (i, 0)),
        out_shape=jax.ShapeDtypeStruct((m, d), out_dtype),
        compiler_params=_params("parallel"),
        name="rms_cast",
    )(x, g.reshape(1, d))


def _wspec(widx, block, index_map, **kw):
    lead = tuple(widx)
    return pl.BlockSpec((None,) * len(lead) + tuple(block), lambda *g: lead + tuple(index_map(*g)), **kw)


def _mm_ws_kernel(a_ref, w_ref, o_ref, wbf_ref):
    @pl.when(pl.program_id(1) == 0)
    def _():
        wbf_ref[...] = w_ref[...].astype(_BF16)

    o_ref[...] = _dot(a_ref[...], wbf_ref[...]).astype(o_ref.dtype)


def mm_ws(a, w, widx, out_dtype, tm=1024, tn=1024, name="mm_ws"):
    m, k = a.shape
    n = w.shape[-1]
    tm = _tile(m, tm)
    nj = -(-n // tn)
    return pl.pallas_call(
        _mm_ws_kernel,
        grid=(nj, m // tm),
        in_specs=[pl.BlockSpec((tm, k), lambda j, i: (i, 0)),
                  _wspec(widx, (k, tn), lambda j, i: (0, j))],
        out_specs=pl.BlockSpec((tm, tn), lambda j, i: (i, j)),
        out_shape=jax.ShapeDtypeStruct((m, nj * tn), out_dtype),
        scratch_shapes=[pltpu.VMEM((k, tn), _BF16)],
        compiler_params=_params("arbitrary", "arbitrary"),
        name=name,
    )(a, w)


def _swiglu_kernel(a_ref, wg_ref, wu_ref, o_ref, wbf_ref, *, tnh):
    @pl.when(pl.program_id(1) == 0)
    def _():
        lead = (0,) * (len(wg_ref.shape) - 2)
        wbf_ref[:, :tnh] = wg_ref[lead].astype(_BF16)
        wbf_ref[:, tnh:] = wu_ref[lead].astype(_BF16)

    acc = _dot(a_ref[...], wbf_ref[...])
    gate = acc[:, :tnh]
    up = acc[:, tnh:]
    o_ref[...] = (gate * _sigmoid(gate) * up).astype(o_ref.dtype)


def mm_swiglu(a, w_gu, widx, tnh=FF_TILE):
    m, k = a.shape
    d_ff = w_gu.shape[-1] // 2
    tm = _tile(m, 1024)
    nj = -(-d_ff // tnh)

    lead = tuple(widx)
    lead_blk = (1,) * len(lead)

    assert d_ff % LANE == 0 and tnh % LANE == 0

    def start(j, base=0):
        return LANE * (base // LANE + jnp.minimum(j * (tnh // LANE), (d_ff - tnh) // LANE))

    def elem_spec(block, index_map):
        return pl.BlockSpec(tuple(pl.Element(b) for b in block), index_map)

    return pl.pallas_call(
        functools.partial(_swiglu_kernel, tnh=tnh),
        grid=(nj, m // tm),
        in_specs=[pl.BlockSpec((tm, k), lambda j, i: (i, 0)),
                  elem_spec(lead_blk + (k, tnh), lambda j, i: lead + (0, start(j))),
                  elem_spec(lead_blk + (k, tnh), lambda j, i: lead + (0, start(j, d_ff)))],
        out_specs=elem_spec((tm, tnh), lambda j, i: (i * tm, start(j))),
        out_shape=jax.ShapeDtypeStruct((m, d_ff), _BF16),
        scratch_shapes=[pltpu.VMEM((k, 2 * tnh), _BF16)],
        compiler_params=_params("arbitrary", "arbitrary"),
        name="mm_swiglu",
    )(a, w_gu, w_gu)


def _down_kernel(a_ref, w_ref, r_ref, o_ref, wbf_ref, *, scale):
    @pl.when(pl.program_id(1) == 0)
    def _():
        wbf_ref[...] = w_ref[...].astype(_BF16)

    o_ref[...] = r_ref[...] + scale * _dot(a_ref[...], wbf_ref[...])


def mm_down(a, w, widx, resid, scale, tm=512, tn=512, name="ffn_down"):
    m, k = a.shape
    d = w.shape[-1]
    tm = _tile(m, tm)
    tn = _tile(d, tn)
    return pl.pallas_call(
        functools.partial(_down_kernel, scale=scale),
        grid=(d // tn, m // tm),
        in_specs=[pl.BlockSpec((tm, k), lambda j, i: (i, 0)),
                  _wspec(widx, (k, tn), lambda j, i: (0, j)),
                  pl.BlockSpec((tm, tn), lambda j, i: (i, j))],
        out_specs=pl.BlockSpec((tm, tn), lambda j, i: (i, j)),
        out_shape=jax.ShapeDtypeStruct((m, d), _F32),
        scratch_shapes=[pltpu.VMEM((k, tn), _BF16)],
        compiler_params=_params("arbitrary", "arbitrary"),
        name=name,
    )(a, w, resid)


def _resid_norm_kernel(a_ref, w_ref, r_ref, g_ref, x_ref, n_ref, wbf_ref, *, scale):
    @pl.when(pl.program_id(0) == 0)
    def _():
        wbf_ref[...] = w_ref[...].astype(_BF16)

    x = r_ref[...] + scale * _dot(a_ref[...], wbf_ref[...])
    x_ref[...] = x
    n_ref[...] = _rms(x, g_ref[...]).astype(n_ref.dtype)


def mm_resid_norm(a, w, widx, resid, g, scale, name="mm_resid"):
    m, k = a.shape
    d = w.shape[-1]
    tm = _tile(m, 512)
    return pl.pallas_call(
        functools.partial(_resid_norm_kernel, scale=scale),
        grid=(m // tm,),
        in_specs=[pl.BlockSpec((tm, k), lambda i: (i, 0)),
                  _wspec(widx, (k, d), lambda i: (0, 0), pipeline_mode=pl.Buffered(1)),
                  pl.BlockSpec((tm, d), lambda i: (i, 0)),
                  pl.BlockSpec((1, d), lambda i: (0, 0))],
        out_specs=[pl.BlockSpec((tm, d), lambda i: (i, 0)),
                   pl.BlockSpec((tm, d), lambda i: (i, 0))],
        out_shape=[jax.ShapeDtypeStruct((m, d), _F32),
                   jax.ShapeDtypeStruct((m, d), _BF16)],
        scratch_shapes=[pltpu.VMEM((k, d), _BF16)],
        compiler_params=_params("arbitrary"),
        name=name,
    )(a, w, resid, g.reshape(1, d))


def _lb_kernel(p_ref, o_ref):
    p = p_ref[...]
    depth = p.shape[0]
    e = jnp.exp(p - jnp.max(p, axis=0, keepdims=True))
    sm = e / jnp.sum(e, axis=0, keepdims=True)
    run = jnp.zeros_like(sm[0:1])
    for layer in range(depth):
        run = run + sm[layer:layer + 1]
        o_ref[layer:layer + 1, :] = run - sm[0:1]


def hgrn_lower_bounds(p):
    return pl.pallas_call(
        _lb_kernel,
        out_shape=jax.ShapeDtypeStruct(p.shape, _F32),
        name="hgrn_lb",
    )(p)


def _hgrn_kernel(q_ref, f_ref, v_ref, g_ref, lb_ref, gn_ref, o_ref, st_ref, *, nchunk, nh):
    c = HG_CHUNK
    sb = HG_SUB
    d = HG_HEAD
    ru = 8

    @pl.when(pl.program_id(2) == 0)
    def _():
        st_ref[...] = jnp.zeros_like(st_ref)

    gn = gn_ref[...]
    row = lax.broadcasted_iota(jnp.int32, (c, c), 0)
    col = lax.broadcasted_iota(jnp.int32, (c, c), 1)
    tril = (row >= col).astype(_BF16)
    future = col > row
    lane_u = lax.broadcasted_iota(jnp.int32, (ru, c), 1)

    def head_chunk(r0, hh):
        cs = slice(hh * d, (hh + 1) * d)
        lb = lb_ref[:, cs]
        qp = q_ref[pl.ds(r0, c), cs]
        fp = f_ref[pl.ds(r0, c), cs]
        v = v_ref[pl.ds(r0, c), cs]
        gp = g_ref[pl.ds(r0, c), cs]
        q = qp * _sigmoid(qp)
        e_abs = jnp.exp(-jnp.abs(fp))
        r_abs = 1.0 / (1.0 + e_abs)
        er_abs = e_abs * r_abs
        pos = fp >= 0.0
        f = lb + (1.0 - lb) * jnp.where(pos, r_abs, er_abs)
        k = (1.0 - lb) * jnp.where(pos, er_abs, r_abs)
        lf = jnp.log2(jnp.maximum(f, F_FLOOR))
        b = _dot_exact_lhs(tril, lf)
        b_last = b[c - 1:c, :]
        bk = b - jnp.log2(k)
        st = st_ref[hh]
        v16 = v.astype(_BF16)

        units = []
        for bi in range(c // sb):
            lo = bi * sb
            if bi > 0:
                ref = b[lo - 1:lo, :]
                qt = q[lo:lo + sb, :] * jnp.exp2(b[lo:lo + sb, :] - ref)
                kt = k[:lo, :] * jnp.exp2(ref - b[:lo, :])
                kt = jnp.concatenate([kt, jnp.zeros((c - lo, d), _F32)], axis=0)
                off = _dot_nt(qt.astype(_BF16), kt.astype(_BF16))
            for uu in range(sb // ru):
                t0 = lo + uu * ru
                b_u = b[t0:t0 + ru, :]
                q_u = q[t0:t0 + ru, :]
                sc = off[uu * ru:(uu + 1) * ru, :] if bi > 0 else jnp.zeros((ru, c), _F32)
                for s in range(lo, t0 + ru):
                    w = q_u * jnp.exp2(b_u - bk[s:s + 1, :])
                    sc = jnp.where(lane_u == s, jnp.sum(w, axis=-1, keepdims=True), sc)
                units.append(sc)
        scores = jnp.where(future, 0.0, jnp.concatenate(units, axis=0))

        q_state = (q * jnp.exp2(b)).astype(_BF16)
        o = _dot(scores.astype(_BF16), v16) + _dot_nt(q_state, st.astype(_BF16))
        k_state = (k * jnp.exp2(b_last - b)).astype(_BF16)
        st_ref[hh] = st * jnp.exp2(b_last) + _dot_tn(v16, k_state)

        on = o * lax.rsqrt(jnp.mean(o * o, axis=-1, keepdims=True) + EPS) * gn
        o_ref[pl.ds(r0, c), cs] = (on * (gp * _sigmoid(gp))).astype(o_ref.dtype)

    def chunk(ci, carry):
        r0 = pl.multiple_of(ci * c, c)
        for hh in range(nh):
            head_chunk(r0, hh)
        return carry

    lax.fori_loop(0, nchunk, chunk, 0)


def hgrn_core(proj, lb, g_norm, bsz, seq):
    m = proj.shape[0]
    width = proj.shape[1] // 4
    heads = width // HG_HEAD
    nh = HG_GROUP
    gw = nh * HG_HEAD
    ngrp = heads // nh
    tb = _tile(seq, 512)
    nsblk = seq // tb

    def spec(off):
        return pl.BlockSpec((tb, gw), lambda bb, h, s: (bb * nsblk + s, off * ngrp + h))

    return pl.pallas_call(
        functools.partial(_hgrn_kernel, nchunk=tb // HG_CHUNK, nh=nh),
        grid=(bsz, ngrp, nsblk),
        in_specs=[spec(0), spec(1), spec(2), spec(3),
                  pl.BlockSpec((1, gw), lambda bb, h, s: (0, h)),
                  pl.BlockSpec((1, HG_HEAD), lambda bb, h, s: (0, 0))],
        out_specs=pl.BlockSpec((tb, gw), lambda bb, h, s: (bb * nsblk + s, h)),
        out_shape=jax.ShapeDtypeStruct((m, width), _BF16),
        scratch_shapes=[pltpu.VMEM((nh, HG_HEAD, HG_HEAD), _F32)],
        compiler_params=_params("parallel", "parallel", "arbitrary"),
        name="hgrn_core",
    )(proj, proj, proj, proj, lb.reshape(1, width), g_norm.reshape(1, HG_HEAD))


def _gelu_tanh(x):
    c0 = math.sqrt(2.0 / math.pi)
    return 0.5 * x * (1.0 + jnp.tanh(c0 * (x + 0.044715 * (x * x * x))))


def _lru_kernel(gate_ref, u_ref, cw_ref, cb_ref, wa_ref, ba_ref, wx_ref, bx_ref, lam_ref,
                o_ref, ext_ref, sa_ref, sx_ref, h_ref, *, ts, pad):
    width = u_ref.shape[1]
    blk = width // LRU_BLOCKS
    halo = 8

    @pl.when(pl.program_id(1) == 0)
    def _():
        ext_ref[0:halo, :] = jnp.zeros((halo, width), _F32)
        h_ref[...] = jnp.zeros_like(h_ref)

    @pl.when(pl.program_id(1) > 0)
    def _():
        ext_ref[0:halo, :] = ext_ref[ts:ts + halo, :]

    ext_ref[halo:halo + ts, :] = u_ref[...]
    uc = cb_ref[...] + cw_ref[0:1, :] * ext_ref[pl.ds(halo - 3, ts), :]
    for j in range(1, CONV_W):
        uc = uc + cw_ref[j:j + 1, :] * ext_ref[pl.ds(halo - 3 + j, ts), :]

    u16 = uc.astype(_BF16)
    r_parts = []
    i_parts = []
    for n in range(LRU_BLOCKS):
        ub = u16[:, n * blk:(n + 1) * blk]
        r_parts.append(_dot(ub, wa_ref[n].astype(_BF16)))
        i_parts.append(_dot(ub, wx_ref[n].astype(_BF16)))
    r = _sigmoid(jnp.concatenate(r_parts, axis=1) + ba_ref[...])
    ig = _sigmoid(jnp.concatenate(i_parts, axis=1) + bx_ref[...])

    lam = lam_ref[...]
    softplus = jnp.maximum(-lam, 0.0) + jnp.log(1.0 + jnp.exp(-jnp.abs(lam)))
    log_a = -LRU_C * r * softplus
    a = jnp.exp(log_a)
    inp = jnp.sqrt(jnp.maximum(1.0 - a * a, 0.0)) * (ig * uc)

    sa_ref[0:pad, :] = jnp.ones((pad, width), _F32)
    sx_ref[0:pad, :] = jnp.zeros((pad, width), _F32)
    shift = 1
    while shift < ts:
        sa_ref[pad:pad + ts, :] = a
        sx_ref[pad:pad + ts, :] = inp
        a_sh = sa_ref[pl.ds(pad - shift, ts), :]
        x_sh = sx_ref[pl.ds(pad - shift, ts), :]
        inp = a * x_sh + inp
        a = a * a_sh
        shift *= 2

    h = inp + a * h_ref[0:1, :]
    h_ref[0:1, :] = h[ts - 1:ts, :]
    o_ref[...] = (_gelu_tanh(gate_ref[...]) * h).astype(o_ref.dtype)


def lru_core(proj, conv_w, conv_b, w_a, b_a, w_x, b_x, lam, bsz, seq):
    m = proj.shape[0]
    width = proj.shape[1] // 2
    blk = width // LRU_BLOCKS
    ts = _tile(seq, 256)
    pad = ts // 2 if ts >= 16 else 8
    pad = max(pad, 8)
    nsblk = seq // ts
    vec = lambda t: t.reshape(1, width)
    row_spec = pl.BlockSpec((1, width), lambda bb, s: (0, 0))
    w_spec = pl.BlockSpec((LRU_BLOCKS, blk, blk), lambda bb, s: (0, 0, 0))
    return pl.pallas_call(
        functools.partial(_lru_kernel, ts=ts, pad=pad),
        grid=(bsz, nsblk),
        in_specs=[pl.BlockSpec((ts, width), lambda bb, s: (bb * nsblk + s, 0)),
                  pl.BlockSpec((ts, width), lambda bb, s: (bb * nsblk + s, 1)),
                  pl.BlockSpec((CONV_W, width), lambda bb, s: (0, 0)),
                  row_spec, w_spec, row_spec, w_spec, row_spec, row_spec],
        out_specs=pl.BlockSpec((ts, width), lambda bb, s: (bb * nsblk + s, 0)),
        out_shape=jax.ShapeDtypeStruct((m, width), _BF16),
        scratch_shapes=[pltpu.VMEM((ts + 8, width), _F32),
                        pltpu.VMEM((ts + pad, width), _F32),
                        pltpu.VMEM((ts + pad, width), _F32),
                        pltpu.VMEM((8, width), _F32)],
        compiler_params=_params("parallel", "arbitrary"),
        name="lru_core",
    )(proj, proj, conv_w, vec(conv_b), w_a, vec(b_a), w_x, vec(b_x), vec(lam))


def _softcap(t):
    return GATE_CAP * jnp.tanh(t / GATE_CAP)


def _log_sigmoid(x):
    return jnp.minimum(x, 0.0) - jnp.log(1.0 + jnp.exp(-jnp.abs(x)))


def _mlstm_kernel(q_ref, k_ref, v_ref, og_ref, gt_ref, bif_ref, gn_ref, o_ref,
                  c_ref, n_ref, m_ref, *, nchunk, c, dqk):
    h_id = pl.program_id(1)

    @pl.when(pl.program_id(2) == 0)
    def _():
        c_ref[...] = jnp.zeros_like(c_ref)
        n_ref[...] = jnp.zeros_like(n_ref)
        m_ref[...] = jnp.zeros_like(m_ref)

    row = lax.broadcasted_iota(jnp.int32, (c, c), 0)
    col = lax.broadcasted_iota(jnp.int32, (c, c), 1)
    causal = row >= col
    tril = causal.astype(_BF16)
    lane = lax.broadcasted_iota(jnp.int32, (8, LANE), 1)
    sel_i = (lane == h_id).astype(_BF16)
    sel_f = (lane == h_id + ML_HEADS).astype(_BF16)
    lane1 = lax.broadcasted_iota(jnp.int32, (1, LANE), 1)
    sel_i_row = (lane1 == h_id).astype(_F32)
    sel_f_row = (lane1 == h_id + ML_HEADS).astype(_F32)
    gn = gn_ref[...]
    bif = bif_ref[...]
    kscale = dqk ** -0.5

    def chunk(ci, carry):
        r0 = pl.multiple_of(ci * c, c)
        q = q_ref[pl.ds(r0, c), :]
        k = k_ref[pl.ds(r0, c), :] * kscale
        v = v_ref[pl.ds(r0, c), :]
        og = og_ref[pl.ds(r0, c), :]
        gt = jnp.where(lane1 < 2 * ML_HEADS, gt_ref[pl.ds(r0, c), :], 0.0)
        pre = _softcap(gt + bif)
        is_f = jnp.logical_and(lane1 >= ML_HEADS, lane1 < 2 * ML_HEADS)
        gl = jnp.where(is_f, _log_sigmoid(pre), pre)
        cum = _dot_exact_lhs(tril, gl)
        b_col = jnp.sum(cum * sel_f_row, axis=-1, keepdims=True)
        i_col = jnp.sum(gl * sel_i_row, axis=-1, keepdims=True)
        ch, cm, cl = _split3(cum)
        b_row = (_dot_nt(sel_f, ch) + _dot_nt(sel_f, cm) + _dot_nt(sel_f, cl))[0:1, :]
        gh, gm, gl3 = _split3(gl)
        i_row = (_dot_nt(sel_i, gh) + _dot_nt(sel_i, gm) + _dot_nt(sel_i, gl3))[0:1, :]

        m_st = m_ref[...]
        c_st = c_ref[...]
        n_st = n_ref[...]

        d_mat = jnp.where(causal, b_col - b_row + i_row, NEG_BIG)
        inter = b_col + m_st
        m_t = jnp.maximum(inter, jnp.max(d_mat, axis=-1, keepdims=True))
        w_intra = jnp.where(causal, jnp.exp(jnp.minimum(d_mat - m_t, 0.0)), 0.0)
        w_inter = jnp.exp(inter - m_t)
        q16 = q.astype(_BF16)
        k16 = k.astype(_BF16)
        v16 = v.astype(_BF16)
        qk = _dot_nt(q16, k16) * w_intra
        num = _dot(qk.astype(_BF16), v16) + w_inter * _dot(q16, c_st.astype(_BF16))
        den = jnp.sum(qk, axis=-1, keepdims=True) + w_inter * jnp.sum(q * n_st, axis=-1, keepdims=True)
        hh = num / jnp.maximum(jnp.abs(den), jnp.exp(-m_t))

        g_tot = b_col[c - 1:c, :]
        upd_col = g_tot - b_col + i_col
        m_new = jnp.maximum(g_tot + m_st, jnp.max(upd_col, axis=0, keepdims=True))
        w_upd = jnp.exp(upd_col - m_new)
        decay = jnp.exp(g_tot + m_st - m_new)
        kw = k * w_upd
        c_ref[...] = decay * c_st + _dot_tn(kw.astype(_BF16), v16)
        n_ref[...] = decay * n_st + jnp.sum(kw, axis=0, keepdims=True)
        m_ref[...] = m_new

        hn = hh * lax.rsqrt(jnp.mean(hh * hh, axis=-1, keepdims=True) + EPS) * gn
        o_ref[pl.ds(r0, c), :] = (hn * _sigmoid(og)).astype(o_ref.dtype)
        return carry

    lax.fori_loop(0, nchunk, chunk, 0)


def mlstm_core(proj, b_if, g_norm, bsz, seq, dqk, dv):
    m = proj.shape[0]
    heads = ML_HEADS
    hv = heads * dv
    c = _tile(seq, ML_CHUNK)
    tb = _tile(seq, 512)
    nsblk = seq // tb
    gate_blk = (2 * heads * dqk + 2 * hv) // LANE
    bif = jnp.zeros((1, LANE), _F32).at[0, :2 * heads].set(b_if.reshape(-1))
    r = dv // dqk
    return pl.pallas_call(
        functools.partial(_mlstm_kernel, nchunk=tb // c, c=c, dqk=dqk),
        grid=(bsz, heads, nsblk),
        in_specs=[pl.BlockSpec((tb, dqk), lambda bb, h, s: (bb * nsblk + s, h)),
                  pl.BlockSpec((tb, dqk), lambda bb, h, s: (bb * nsblk + s, heads + h)),
                  pl.BlockSpec((tb, dv), lambda bb, h, s: (bb * nsblk + s, (2 * heads) // r + h)),
                  pl.BlockSpec((tb, dv), lambda bb, h, s: (bb * nsblk + s, (2 * heads) // r + heads + h)),
                  pl.BlockSpec((tb, LANE), lambda bb, h, s: (bb * nsblk + s, gate_blk)),
                  pl.BlockSpec((1, LANE), lambda bb, h, s: (0, 0)),
                  pl.BlockSpec((1, dv), lambda bb, h, s: (0, 0))],
        out_specs=pl.BlockSpec((tb, dv), lambda bb, h, s: (bb * nsblk + s, h)),
        out_shape=jax.ShapeDtypeStruct((m, hv), _BF16),
        scratch_shapes=[pltpu.VMEM((dqk, dv), _F32),
                        pltpu.VMEM((1, dqk), _F32),
                        pltpu.VMEM((1, 1), _F32)],
        compiler_params=_params("parallel", "parallel", "arbitrary"),
        name="mlstm_core",
    )(proj, proj, proj, proj, proj, bif, g_norm.reshape(1, dv))


def _xattn_kernel(q_ref, k_ref, v_ref, o_ref, *, heads):
    d = q_ref.shape[1]
    hd = d // heads
    scale = hd ** -0.5
    for h in range(heads):
        qh = q_ref[:, h * hd:(h + 1) * hd]
        kh = k_ref[:, h * hd:(h + 1) * hd]
        vh = v_ref[:, h * hd:(h + 1) * hd]
        s = _dot_nt(qh, kh) * scale
        s = s - jnp.max(s, axis=-1, keepdims=True)
        e = jnp.exp(s)
        p = e / jnp.sum(e, axis=-1, keepdims=True)
        o_ref[:, h * hd:(h + 1) * hd] = _dot(p.astype(_BF16), vh).astype(o_ref.dtype)


def xattn_core(q, kv, bsz, seq, n_mem):
    m, d = q.shape
    tq = _tile(seq, 512)
    nq = seq // tq
    return pl.pallas_call(
        functools.partial(_xattn_kernel, heads=XA_HEADS),
        grid=(bsz, nq),
        in_specs=[pl.BlockSpec((tq, d), lambda bb, s: (bb * nq + s, 0)),
                  pl.BlockSpec((n_mem, d), lambda bb, s: (bb, 0)),
                  pl.BlockSpec((n_mem, d), lambda bb, s: (bb, 1))],
        out_specs=pl.BlockSpec((tq, d), lambda bb, s: (bb * nq + s, 0)),
        out_shape=jax.ShapeDtypeStruct((m, d), _BF16),
        compiler_params=_params("parallel", "parallel"),
        name="xattn_core",
    )(q, kv, kv)


def kernel(x, mem, mem_norm_g, norm_g, final_norm_g, ffn_w_gu, ffn_w_down, xa_w_q, xa_w_kv, xa_w_o,
           hg_lb_param, hg_w_in, hg_g_norm, hg_w_out,
           lru_w_in, lru_conv_w, lru_conv_b, lru_w_a, lru_b_a, lru_w_x, lru_b_x, lru_lambda, lru_w_out,
           ml_w_in, ml_b_if, ml_g_norm, ml_w_out):
    bsz, seq, d = x.shape
    n_mem = mem.shape[1]
    depth = norm_g.shape[0]
    m = bsz * seq
    ml_dqk = d // (2 * ML_HEADS)
    ml_dv = d // ML_HEADS

    xf = x.reshape(m, d)
    mem_n = rms_cast(mem.reshape(bsz * n_mem, d), mem_norm_g)
    lb_all = hgrn_lower_bounds(hg_lb_param)

    xn = rms_cast(xf, norm_g[0, 0])
    for layer in range(depth):
        kind = layer % N_MIXERS
        idx = layer // N_MIXERS

        h = mm_swiglu(xn, ffn_w_gu, (layer, 0))
        xf = mm_down(h, ffn_w_down, (layer, 0), xf, 0.5)
        xn = rms_cast(xf, norm_g[layer, 1])

        if kind == 0:
            proj = mm_ws(xn, hg_w_in, (idx,), _F32, name="hg_in")
            y = hgrn_core(proj, lb_all[layer], hg_g_norm[idx], bsz, seq)
            w_out = hg_w_out
        elif kind == 1:
            proj = mm_ws(xn, lru_w_in, (idx,), _F32, name="lru_in")
            y = lru_core(proj, lru_conv_w[idx], lru_conv_b[idx], lru_w_a[idx], lru_b_a[idx],
                         lru_w_x[idx], lru_b_x[idx], lru_lambda[idx], bsz, seq)
            w_out = lru_w_out
        else:
            proj = mm_ws(xn, ml_w_in, (idx,), _F32, tn=896, name="ml_in")
            y = mlstm_core(proj, ml_b_if[idx], ml_g_norm[idx], bsz, seq, ml_dqk, ml_dv)
            w_out = ml_w_out
        xf, xn = mm_resid_norm(y, w_out, (idx,), xf, norm_g[layer, 2], 1.0, name="mixer_out")

        q = mm_ws(xn, xa_w_q, (layer,), _BF16, name="xa_q")
        kv = mm_ws(mem_n, xa_w_kv, (layer,), _BF16, name="xa_kv")
        o = xattn_core(q, kv, bsz, seq, n_mem)
        xf, xn = mm_resid_norm(o, xa_w_o, (layer,), xf, norm_g[layer, 3], 1.0, name="xa_out")

        h = mm_swiglu(xn, ffn_w_gu, (layer, 1))
        xf = mm_down(h, ffn_w_down, (layer, 1), xf, 0.5)
        if layer + 1 < depth:
            xn = rms_cast(xf, norm_g[layer + 1, 0])
    out = rms_cast(xf, final_norm_g, _F32)
    return out.reshape(bsz, seq, d)
```

```python
import functools
import math

import jax
import jax.numpy as jnp
from jax import lax
from jax.experimental import pallas as pl
from jax.experimental.pallas import tpu as pltpu

EPS = 1e-6
N_MIXERS = 3
HG_HEAD = 128
HG_CHUNK = 64
HG_SUB = 16
HG_GROUP = 4
F_FLOOR = 1e-12
LRU_BLOCKS = 8
CONV_W = 4
LRU_C = 8.0
ML_HEADS = 8
ML_CHUNK = 256
GATE_CAP = 15.0
NEG_BIG = -1e30
XA_HEADS = 4
LANE = 128
FF_TILE = 512
SWIGLU_SPLIT = 4
EPILOGUE_SPLIT = 2
VMEM_LIMIT = 56 * 1024 * 1024

_BF16 = jnp.bfloat16
_F32 = jnp.float32


def _params(*sem):
    return pltpu.CompilerParams(dimension_semantics=sem, vmem_limit_bytes=VMEM_LIMIT)


def _tile(dim, pref):
    t = min(dim, pref)
    while dim % t:
        t //= 2
    return t


def _sigmoid(x):
    return 1.0 / (1.0 + jnp.exp(-x))


def _dot(a, b):
    return jnp.dot(a, b, preferred_element_type=_F32)


def _dot_nt(a, b):
    return lax.dot_general(a, b, (((1,), (1,)), ((), ())), preferred_element_type=_F32)


def _dot_tn(a, b):
    return lax.dot_general(a, b, (((0,), (0,)), ((), ())), preferred_element_type=_F32)


def _split3(x):
    hi = x.astype(_BF16)
    r1 = x - hi.astype(_F32)
    mid = r1.astype(_BF16)
    lo = (r1 - mid.astype(_F32)).astype(_BF16)
    return hi, mid, lo


def _dot_exact_lhs(sel, x):
    hi, mid, lo = _split3(x)
    return _dot(sel, hi) + _dot(sel, mid) + _dot(sel, lo)


def _rms(x, g):
    return x * lax.rsqrt(jnp.mean(x * x, axis=-1, keepdims=True) + EPS) * g


def _rms_kernel(x_ref, g_ref, o_ref):
    o_ref[...] = _rms(x_ref[...], g_ref[...]).astype(o_ref.dtype)


def rms_cast(x, g, out_dtype=_BF16):
    m, d = x.shape
    tm = _tile(m, 512)
    return pl.pallas_call(
        _rms_kernel,
        grid=(m // tm,),
        in_specs=[pl.BlockSpec((tm, d), lambda i: (i, 0)),
                  pl.BlockSpec((1, d), lambda i: (0, 0))],
        out_specs=pl.BlockSpec((tm, d), lambda i: (i, 0)),
        out_shape=jax.ShapeDtypeStruct((m, d), out_dtype),
        compiler_params=_params("parallel"),
        name="rms_cast",
    )(x, g.reshape(1, d))


def _wspec(widx, block, index_map, **kw):
    lead = tuple(widx)
    return pl.BlockSpec((None,) * len(lead) + tuple(block), lambda *g: lead + tuple(index_map(*g)), **kw)


def _mm_ws_kernel(a_ref, w_ref, o_ref, wbf_ref):
    @pl.when(pl.program_id(1) == 0)
    def _():
        wbf_ref[...] = w_ref[...].astype(_BF16)

    o_ref[...] = _dot(a_ref[...], wbf_ref[...]).astype(o_ref.dtype)


def mm_ws(a, w, widx, out_dtype, tm=1024, tn=1024, name="mm_ws"):
    m, k = a.shape
    n = w.shape[-1]
    tm = _tile(m, tm)
    nj = -(-n // tn)
    return pl.pallas_call(
        _mm_ws_kernel,
        grid=(nj, m // tm),
        in_specs=[pl.BlockSpec((tm, k), lambda j, i: (i, 0)),
                  _wspec(widx, (k, tn), lambda j, i: (0, j))],
        out_specs=pl.BlockSpec((tm, tn), lambda j, i: (i, j)),
        out_shape=jax.ShapeDtypeStruct((m, nj * tn), out_dtype),
        scratch_shapes=[pltpu.VMEM((k, tn), _BF16)],
        compiler_params=_params("arbitrary", "arbitrary"),
        name=name,
    )(a, w)


def _swiglu_kernel(a_ref, wg_ref, wu_ref, o_ref, wbf_ref, *, tnh, rem):
    @pl.when(pl.program_id(1) == 0)
    def _():
        lead = (0,) * (len(wg_ref.shape) - 2)
        wbf_ref[:, :tnh] = wg_ref[lead].astype(_BF16)
        wbf_ref[:, tnh:] = wu_ref[lead].astype(_BF16)

    tm = a_ref.shape[0]
    sub = tm // SWIGLU_SPLIT
    last = pl.num_programs(0) - 1
    for r in range(SWIGLU_SPLIT):
        rows = slice(r * sub, (r + 1) * sub)
        acc = _dot(a_ref[rows, :], wbf_ref[...])
        gate = acc[:, :tnh]
        up = acc[:, tnh:]
        res = (gate * _sigmoid(gate) * up).astype(o_ref.dtype)
        if rem == tnh:
            o_ref[rows, :] = res
        else:
            o_ref[rows, :rem] = jnp.where(pl.program_id(0) == last, res[:, tnh - rem:], res[:, :rem])
            o_ref[rows, rem:] = res[:, rem:]


def mm_swiglu(a, w_gu, widx, tnh=FF_TILE):
    m, k = a.shape
    d_ff = w_gu.shape[-1] // 2
    tm = _tile(m, 1024)
    nj = -(-d_ff // tnh)

    lead = tuple(widx)
    lead_blk = (1,) * len(lead)

    assert d_ff % LANE == 0 and tnh % LANE == 0

    def start(j, base=0):
        return LANE * (base // LANE + jnp.minimum(j * (tnh // LANE), (d_ff - tnh) // LANE))

    def elem_spec(block, index_map):
        return pl.BlockSpec(tuple(pl.Element(b) for b in block), index_map)

    return pl.pallas_call(
        functools.partial(_swiglu_kernel, tnh=tnh, rem=d_ff - (nj - 1) * tnh),
        grid=(nj, m // tm),
        in_specs=[pl.BlockSpec((tm, k), lambda j, i: (i, 0)),
                  elem_spec(lead_blk + (k, tnh), lambda j, i: lead + (0, start(j))),
                  elem_spec(lead_blk + (k, tnh), lambda j, i: lead + (0, start(j, d_ff)))],
        out_specs=pl.BlockSpec((tm, tnh), lambda j, i: (i, j)),
        out_shape=jax.ShapeDtypeStruct((m, d_ff), _BF16),
        scratch_shapes=[pltpu.VMEM((k, 2 * tnh), _BF16)],
        compiler_params=_params("arbitrary", "arbitrary"),
        name="mm_swiglu",
    )(a, w_gu, w_gu)


def _down_kernel(a_ref, w_ref, r_ref, o_ref, wbf_ref, *, scale):
    @pl.when(pl.program_id(1) == 0)
    def _():
        wbf_ref[...] = w_ref[...].astype(_BF16)

    sub = a_ref.shape[0] // EPILOGUE_SPLIT
    for r in range(EPILOGUE_SPLIT):
        rows = slice(r * sub, (r + 1) * sub)
        o_ref[rows, :] = r_ref[rows, :] + scale * _dot(a_ref[rows, :], wbf_ref[...])


def mm_down(a, w, widx, resid, scale, tm=512, tn=512, name="ffn_down"):
    m, k = a.shape
    d = w.shape[-1]
    tm = _tile(m, tm)
    tn = _tile(d, tn)
    return pl.pallas_call(
        functools.partial(_down_kernel, scale=scale),
        grid=(d // tn, m // tm),
        in_specs=[pl.BlockSpec((tm, k), lambda j, i: (i, 0)),
                  _wspec(widx, (k, tn), lambda j, i: (0, j)),
                  pl.BlockSpec((tm, tn), lambda j, i: (i, j))],
        out_specs=pl.BlockSpec((tm, tn), lambda j, i: (i, j)),
        out_shape=jax.ShapeDtypeStruct((m, d), _F32),
        scratch_shapes=[pltpu.VMEM((k, tn), _BF16)],
        compiler_params=_params("arbitrary", "arbitrary"),
        name=name,
    )(a, w, resid)


def _resid_norm_kernel(a_ref, w_ref, r_ref, g_ref, x_ref, n_ref, wbf_ref, *, scale):
    @pl.when(pl.program_id(0) == 0)
    def _():
        wbf_ref[...] = w_ref[...].astype(_BF16)

    sub = a_ref.shape[0] // EPILOGUE_SPLIT
    for r in range(EPILOGUE_SPLIT):
        rows = slice(r * sub, (r + 1) * sub)
        x = r_ref[rows, :] + scale * _dot(a_ref[rows, :], wbf_ref[...])
        x_ref[rows, :] = x
        n_ref[rows, :] = _rms(x, g_ref[...]).astype(n_ref.dtype)


def mm_resid_norm(a, w, widx, resid, g, scale, name="mm_resid"):
    m, k = a.shape
    d = w.shape[-1]
    tm = _tile(m, 512)
    return pl.pallas_call(
        functools.partial(_resid_norm_kernel, scale=scale),
        grid=(m // tm,),
        in_specs=[pl.BlockSpec((tm, k), lambda i: (i, 0)),
                  _wspec(widx, (k, d), lambda i: (0, 0), pipeline_mode=pl.Buffered(1)),
                  pl.BlockSpec((tm, d), lambda i: (i, 0)),
                  pl.BlockSpec((1, d), lambda i: (0, 0))],
        out_specs=[pl.BlockSpec((tm, d), lambda i: (i, 0)),
                   pl.BlockSpec((tm, d), lambda i: (i, 0))],
        out_shape=[jax.ShapeDtypeStruct((m, d), _F32),
                   jax.ShapeDtypeStruct((m, d), _BF16)],
        scratch_shapes=[pltpu.VMEM((k, d), _BF16)],
        compiler_params=_params("arbitrary"),
        name=name,
    )(a, w, resid, g.reshape(1, d))


def _lb_kernel(p_ref, o_ref):
    p = p_ref[...]
    depth = p.shape[0]
    e = jnp.exp(p - jnp.max(p, axis=0, keepdims=True))
    sm = e / jnp.sum(e, axis=0, keepdims=True)
    run = jnp.zeros_like(sm[0:1])
    for layer in range(depth):
        run = run + sm[layer:layer + 1]
        o_ref[layer:layer + 1, :] = run - sm[0:1]


def hgrn_lower_bounds(p):
    return pl.pallas_call(
        _lb_kernel,
        out_shape=jax.ShapeDtypeStruct(p.shape, _F32),
        name="hgrn_lb",
    )(p)


def _hgrn_kernel(q_ref, f_ref, v_ref, g_ref, lb_ref, gn_ref, o_ref, st_ref, *, nchunk, nh):
    c = HG_CHUNK
    sb = HG_SUB
    d = HG_HEAD
    ru = 8

    @pl.when(pl.program_id(2) == 0)
    def _():
        st_ref[...] = jnp.zeros_like(st_ref)

    gn = gn_ref[...]
    row = lax.broadcasted_iota(jnp.int32, (c, c), 0)
    col = lax.broadcasted_iota(jnp.int32, (c, c), 1)
    tril = (row >= col).astype(_BF16)
    future = col > row
    lane_u = lax.broadcasted_iota(jnp.int32, (ru, c), 1)

    def head_chunk(r0, hh):
        cs = slice(hh * d, (hh + 1) * d)
        lb = lb_ref[:, cs]
        qp = q_ref[pl.ds(r0, c), cs]
        fp = f_ref[pl.ds(r0, c), cs]
        v = v_ref[pl.ds(r0, c), cs]
        gp = g_ref[pl.ds(r0, c), cs]
        q = qp * _sigmoid(qp)
        e_abs = jnp.exp(-jnp.abs(fp))
        r_abs = 1.0 / (1.0 + e_abs)
        er_abs = e_abs * r_abs
        pos = fp >= 0.0
        f = lb + (1.0 - lb) * jnp.where(pos, r_abs, er_abs)
        k = (1.0 - lb) * jnp.where(pos, er_abs, r_abs)
        lf = jnp.log2(jnp.maximum(f, F_FLOOR))
        b = _dot_exact_lhs(tril, lf)
        b_last = b[c - 1:c, :]
        bk = b - jnp.log2(k)
        st = st_ref[hh]
        v16 = v.astype(_BF16)

        units = []
        for bi in range(c // sb):
            lo = bi * sb
            if bi > 0:
                ref = b[lo - 1:lo, :]
                qt = q[lo:lo + sb, :] * jnp.exp2(b[lo:lo + sb, :] - ref)
                kt = k[:lo, :] * jnp.exp2(ref - b[:lo, :])
                kt = jnp.concatenate([kt, jnp.zeros((c - lo, d), _F32)], axis=0)
                off = _dot_nt(qt.astype(_BF16), kt.astype(_BF16))
            for uu in range(sb // ru):
                t0 = lo + uu * ru
                b_u = b[t0:t0 + ru, :]
                q_u = q[t0:t0 + ru, :]
                sc = off[uu * ru:(uu + 1) * ru, :] if bi > 0 else jnp.zeros((ru, c), _F32)
                for s in range(lo, t0 + ru):
                    w = q_u * jnp.exp2(b_u - bk[s:s + 1, :])
                    sc = jnp.where(lane_u == s, jnp.sum(w, axis=-1, keepdims=True), sc)
                units.append(sc)
        scores = jnp.where(future, 0.0, jnp.concatenate(units, axis=0))

        q_state = (q * jnp.exp2(b)).astype(_BF16)
        o = _dot(scores.astype(_BF16), v16) + _dot_nt(q_state, st.astype(_BF16))
        k_state = (k * jnp.exp2(b_last - b)).astype(_BF16)
        st_ref[hh] = st * jnp.exp2(b_last) + _dot_tn(v16, k_state)

        on = o * lax.rsqrt(jnp.mean(o * o, axis=-1, keepdims=True) + EPS) * gn
        o_ref[pl.ds(r0, c), cs] = (on * (gp * _sigmoid(gp))).astype(o_ref.dtype)

    def chunk_pair(ci, carry):
        for half in range(2):
            r0 = pl.multiple_of((2 * ci + half) * c, c)
            for hh in range(nh):
                head_chunk(r0, hh)
        return carry

    assert nchunk % 2 == 0
    lax.fori_loop(0, nchunk // 2, chunk_pair, 0)


def hgrn_core(proj, lb, g_norm, bsz, seq):
    m = proj.shape[0]
    width = proj.shape[1] // 4
    heads = width // HG_HEAD
    nh = HG_GROUP
    gw = nh * HG_HEAD
    ngrp = heads // nh
    tb = _tile(seq, 512)
    nsblk = seq // tb

    def spec(off):
        return pl.BlockSpec((tb, gw), lambda bb, h, s: (bb * nsblk + s, off * ngrp + h))

    return pl.pallas_call(
        functools.partial(_hgrn_kernel, nchunk=tb // HG_CHUNK, nh=nh),
        grid=(bsz, ngrp, nsblk),
        in_specs=[spec(0), spec(1), spec(2), spec(3),
                  pl.BlockSpec((1, gw), lambda bb, h, s: (0, h)),
                  pl.BlockSpec((1, HG_HEAD), lambda bb, h, s: (0, 0))],
        out_specs=pl.BlockSpec((tb, gw), lambda bb, h, s: (bb * nsblk + s, h)),
        out_shape=jax.ShapeDtypeStruct((m, width), _BF16),
        scratch_shapes=[pltpu.VMEM((nh, HG_HEAD, HG_HEAD), _F32)],
        compiler_params=_params("parallel", "parallel", "arbitrary"),
        name="hgrn_core",
    )(proj, proj, proj, proj, lb.reshape(1, width), g_norm.reshape(1, HG_HEAD))


def _gelu_tanh(x):
    c0 = math.sqrt(2.0 / math.pi)
    return 0.5 * x * (1.0 + jnp.tanh(c0 * (x + 0.044715 * (x * x * x))))


def _lru_kernel(gate_ref, u_ref, cw_ref, cb_ref, wa_ref, ba_ref, wx_ref, bx_ref, lam_ref,
                o_ref, ext_ref, sa_ref, sx_ref, h_ref, *, ts, pad):
    width = u_ref.shape[1]
    blk = width // LRU_BLOCKS
    halo = 8

    @pl.when(pl.program_id(1) == 0)
    def _():
        ext_ref[0:halo, :] = jnp.zeros((halo, width), _F32)
        h_ref[...] = jnp.zeros_like(h_ref)

    @pl.when(pl.program_id(1) > 0)
    def _():
        ext_ref[0:halo, :] = ext_ref[ts:ts + halo, :]

    ext_ref[halo:halo + ts, :] = u_ref[...]
    uc = cb_ref[...] + cw_ref[0:1, :] * ext_ref[pl.ds(halo - 3, ts), :]
    for j in range(1, CONV_W):
        uc = uc + cw_ref[j:j + 1, :] * ext_ref[pl.ds(halo - 3 + j, ts), :]

    u16 = uc.astype(_BF16)
    r_parts = []
    i_parts = []
    for n in range(LRU_BLOCKS):
        ub = u16[:, n * blk:(n + 1) * blk]
        r_parts.append(_dot(ub, wa_ref[n].astype(_BF16)))
        i_parts.append(_dot(ub, wx_ref[n].astype(_BF16)))
    r = _sigmoid(jnp.concatenate(r_parts, axis=1) + ba_ref[...])
    ig = _sigmoid(jnp.concatenate(i_parts, axis=1) + bx_ref[...])

    lam = lam_ref[...]
    softplus = jnp.maximum(-lam, 0.0) + jnp.log(1.0 + jnp.exp(-jnp.abs(lam)))
    log_a = -LRU_C * r * softplus
    a = jnp.exp(log_a)
    inp = jnp.sqrt(jnp.maximum(1.0 - a * a, 0.0)) * (ig * uc)

    sa_ref[0:pad, :] = jnp.ones((pad, width), _F32)
    sx_ref[0:pad, :] = jnp.zeros((pad, width), _F32)
    shift = 1
    while shift < ts:
        sa_ref[pad:pad + ts, :] = a
        sx_ref[pad:pad + ts, :] = inp
        a_sh = sa_ref[pl.ds(pad - shift, ts), :]
        x_sh = sx_ref[pl.ds(pad - shift, ts), :]
        inp = a * x_sh + inp
        a = a * a_sh
        shift *= 2

    h = inp + a * h_ref[0:1, :]
    h_ref[0:1, :] = h[ts - 1:ts, :]
    o_ref[...] = (_gelu_tanh(gate_ref[...]) * h).astype(o_ref.dtype)


def lru_core(proj, conv_w, conv_b, w_a, b_a, w_x, b_x, lam, bsz, seq):
    m = proj.shape[0]
    width = proj.shape[1] // 2
    blk = width // LRU_BLOCKS
    ts = _tile(seq, 256)
    pad = ts // 2 if ts >= 16 else 8
    pad = max(pad, 8)
    nsblk = seq // ts
    vec = lambda t: t.reshape(1, width)
    row_spec = pl.BlockSpec((1, width), lambda bb, s: (0, 0))
    w_spec = pl.BlockSpec((LRU_BLOCKS, blk, blk), lambda bb, s: (0, 0, 0))
    return pl.pallas_call(
        functools.partial(_lru_kernel, ts=ts, pad=pad),
        grid=(bsz, nsblk),
        in_specs=[pl.BlockSpec((ts, width), lambda bb, s: (bb * nsblk + s, 0)),
                  pl.BlockSpec((ts, width), lambda bb, s: (bb * nsblk + s, 1)),
                  pl.BlockSpec((CONV_W, width), lambda bb, s: (0, 0)),
                  row_spec, w_spec, row_spec, w_spec, row_spec, row_spec],
        out_specs=pl.BlockSpec((ts, width), lambda bb, s: (bb * nsblk + s, 0)),
        out_shape=jax.ShapeDtypeStruct((m, width), _BF16),
        scratch_shapes=[pltpu.VMEM((ts + 8, width), _F32),
                        pltpu.VMEM((ts + pad, width), _F32),
                        pltpu.VMEM((ts + pad, width), _F32),
                        pltpu.VMEM((8, width), _F32)],
        compiler_params=_params("parallel", "arbitrary"),
        name="lru_core",
    )(proj, proj, conv_w, vec(conv_b), w_a, vec(b_a), w_x, vec(b_x), vec(lam))


def _softcap(t):
    return GATE_CAP * jnp.tanh(t / GATE_CAP)


def _log_sigmoid(x):
    return jnp.minimum(x, 0.0) - jnp.log(1.0 + jnp.exp(-jnp.abs(x)))


def _mlstm_kernel(q_ref, k_ref, v_ref, og_ref, gt_ref, bif_ref, gn_ref, o_ref,
                  c_ref, n_ref, m_ref, *, nchunk, c, dqk):
    h_id = pl.program_id(1)

    @pl.when(pl.program_id(2) == 0)
    def _():
        c_ref[...] = jnp.zeros_like(c_ref)
        n_ref[...] = jnp.zeros_like(n_ref)
        m_ref[...] = jnp.zeros_like(m_ref)

    row = lax.broadcasted_iota(jnp.int32, (c, c), 0)
    col = lax.broadcasted_iota(jnp.int32, (c, c), 1)
    causal = row >= col
    tril = causal.astype(_BF16)
    lane = lax.broadcasted_iota(jnp.int32, (8, LANE), 1)
    sel_i = (lane == h_id).astype(_BF16)
    sel_f = (lane == h_id + ML_HEADS).astype(_BF16)
    lane1 = lax.broadcasted_iota(jnp.int32, (1, LANE), 1)
    sel_i_row = (lane1 == h_id).astype(_F32)
    sel_f_row = (lane1 == h_id + ML_HEADS).astype(_F32)
    gn = gn_ref[...]
    bif = bif_ref[...]
    kscale = dqk ** -0.5

    def chunk(ci, carry):
        r0 = pl.multiple_of(ci * c, c)
        q = q_ref[pl.ds(r0, c), :]
        k = k_ref[pl.ds(r0, c), :] * kscale
        v = v_ref[pl.ds(r0, c), :]
        og = og_ref[pl.ds(r0, c), :]
        gt = jnp.where(lane1 < 2 * ML_HEADS, gt_ref[pl.ds(r0, c), :], 0.0)
        pre = _softcap(gt + bif)
        is_f = jnp.logical_and(lane1 >= ML_HEADS, lane1 < 2 * ML_HEADS)
        gl = jnp.where(is_f, _log_sigmoid(pre), pre)
        cum = _dot_exact_lhs(tril, gl)
        b_col = jnp.sum(cum * sel_f_row, axis=-1, keepdims=True)
        i_col = jnp.sum(gl * sel_i_row, axis=-1, keepdims=True)
        ch, cm, cl = _split3(cum)
        b_row = (_dot_nt(sel_f, ch) + _dot_nt(sel_f, cm) + _dot_nt(sel_f, cl))[0:1, :]
        gh, gm, gl3 = _split3(gl)
        i_row = (_dot_nt(sel_i, gh) + _dot_nt(sel_i, gm) + _dot_nt(sel_i, gl3))[0:1, :]

        m_st = m_ref[...]
        c_st = c_ref[...]
        n_st = n_ref[...]

        d_mat = jnp.where(causal, b_col - b_row + i_row, NEG_BIG)
        inter = b_col + m_st
        m_t = jnp.maximum(inter, jnp.max(d_mat, axis=-1, keepdims=True))
        w_intra = jnp.where(causal, jnp.exp(jnp.minimum(d_mat - m_t, 0.0)), 0.0)
        w_inter = jnp.exp(inter - m_t)
        q16 = q.astype(_BF16)
        k16 = k.astype(_BF16)
        v16 = v.astype(_BF16)
        qk = _dot_nt(q16, k16) * w_intra
        num = _dot(qk.astype(_BF16), v16) + w_inter * _dot(q16, c_st.astype(_BF16))
        den = jnp.sum(qk, axis=-1, keepdims=True) + w_inter * jnp.sum(q * n_st, axis=-1, keepdims=True)
        hh = num / jnp.maximum(jnp.abs(den), jnp.exp(-m_t))

        g_tot = b_col[c - 1:c, :]
        upd_col = g_tot - b_col + i_col
        m_new = jnp.maximum(g_tot + m_st, jnp.max(upd_col, axis=0, keepdims=True))
        w_upd = jnp.exp(upd_col - m_new)
        decay = jnp.exp(g_tot + m_st - m_new)
        kw = k * w_upd
        c_ref[...] = decay * c_st + _dot_tn(kw.astype(_BF16), v16)
        n_ref[...] = decay * n_st + jnp.sum(kw, axis=0, keepdims=True)
        m_ref[...] = m_new

        hn = hh * lax.rsqrt(jnp.mean(hh * hh, axis=-1, keepdims=True) + EPS) * gn
        o_ref[pl.ds(r0, c), :] = (hn * _sigmoid(og)).astype(o_ref.dtype)
        return carry

    lax.fori_loop(0, nchunk, chunk, 0)


def mlstm_core(proj, b_if, g_norm, bsz, seq, dqk, dv):
    m = proj.shape[0]
    heads = ML_HEADS
    hv = heads * dv
    c = _tile(seq, ML_CHUNK)
    tb = _tile(seq, 512)
    nsblk = seq // tb
    gate_blk = (2 * heads * dqk + 2 * hv) // LANE
    bif = jnp.zeros((1, LANE), _F32).at[0, :2 * heads].set(b_if.reshape(-1))
    r = dv // dqk
    return pl.pallas_call(
        functools.partial(_mlstm_kernel, nchunk=tb // c, c=c, dqk=dqk),
        grid=(bsz, heads, nsblk),
        in_specs=[pl.BlockSpec((tb, dqk), lambda bb, h, s: (bb * nsblk + s, h)),
                  pl.BlockSpec((tb, dqk), lambda bb, h, s: (bb * nsblk + s, heads + h)),
                  pl.BlockSpec((tb, dv), lambda bb, h, s: (bb * nsblk + s, (2 * heads) // r + h)),
                  pl.BlockSpec((tb, dv), lambda bb, h, s: (bb * nsblk + s, (2 * heads) // r + heads + h)),
                  pl.BlockSpec((tb, LANE), lambda bb, h, s: (bb * nsblk + s, gate_blk)),
                  pl.BlockSpec((1, LANE), lambda bb, h, s: (0, 0)),
                  pl.BlockSpec((1, dv), lambda bb, h, s: (0, 0))],
        out_specs=pl.BlockSpec((tb, dv), lambda bb, h, s: (bb * nsblk + s, h)),
        out_shape=jax.ShapeDtypeStruct((m, hv), _BF16),
        scratch_shapes=[pltpu.VMEM((dqk, dv), _F32),
                        pltpu.VMEM((1, dqk), _F32),
                        pltpu.VMEM((1, 1), _F32)],
        compiler_params=_params("parallel", "parallel", "arbitrary"),
        name="mlstm_core",
    )(proj, proj, proj, proj, proj, bif, g_norm.reshape(1, dv))


def _xattn_kernel(q_ref, k_ref, v_ref, o_ref, *, heads):
    d = q_ref.shape[1]
    hd = d // heads
    scale = hd ** -0.5
    for h in range(heads):
        qh = q_ref[:, h * hd:(h + 1) * hd]
        kh = k_ref[:, h * hd:(h + 1) * hd]
        vh = v_ref[:, h * hd:(h + 1) * hd]
        s = _dot_nt(qh, kh) * scale
        s = s - jnp.max(s, axis=-1, keepdims=True)
        e = jnp.exp(s)
        p = e / jnp.sum(e, axis=-1, keepdims=True)
        o_ref[:, h * hd:(h + 1) * hd] = _dot(p.astype(_BF16), vh).astype(o_ref.dtype)


def xattn_core(q, kv, bsz, seq, n_mem):
    m, d = q.shape
    tq = _tile(seq, 512)
    nq = seq // tq
    return pl.pallas_call(
        functools.partial(_xattn_kernel, heads=XA_HEADS),
        grid=(bsz, nq),
        in_specs=[pl.BlockSpec((tq, d), lambda bb, s: (bb * nq + s, 0)),
                  pl.BlockSpec((n_mem, d), lambda bb, s: (bb, 0)),
                  pl.BlockSpec((n_mem, d), lambda bb, s: (bb, 1))],
        out_specs=pl.BlockSpec((tq, d), lambda bb, s: (bb * nq + s, 0)),
        out_shape=jax.ShapeDtypeStruct((m, d), _BF16),
        compiler_params=_params("parallel", "parallel"),
        name="xattn_core",
    )(q, kv, kv)


def kernel(x, mem, mem_norm_g, norm_g, final_norm_g, ffn_w_gu, ffn_w_down, xa_w_q, xa_w_kv, xa_w_o,
           hg_lb_param, hg_w_in, hg_g_norm, hg_w_out,
           lru_w_in, lru_conv_w, lru_conv_b, lru_w_a, lru_b_a, lru_w_x, lru_b_x, lru_lambda, lru_w_out,
           ml_w_in, ml_b_if, ml_g_norm, ml_w_out):
    bsz, seq, d = x.shape
    n_mem = mem.shape[1]
    depth = norm_g.shape[0]
    m = bsz * seq
    ml_dqk = d // (2 * ML_HEADS)
    ml_dv = d // ML_HEADS

    xf = x.reshape(m, d)
    mem_n = rms_cast(mem.reshape(bsz * n_mem, d), mem_norm_g)
    lb_all = hgrn_lower_bounds(hg_lb_param)

    xn = rms_cast(xf, norm_g[0, 0])
    for layer in range(depth):
        kind = layer % N_MIXERS
        idx = layer // N_MIXERS

        h = mm_swiglu(xn, ffn_w_gu, (layer, 0))
        xf = mm_down(h, ffn_w_down, (layer, 0), xf, 0.5)
        xn = rms_cast(xf, norm_g[layer, 1])

        if kind == 0:
            proj = mm_ws(xn, hg_w_in, (idx,), _F32, name="hg_in")
            y = hgrn_core(proj, lb_all[layer], hg_g_norm[idx], bsz, seq)
            w_out = hg_w_out
        elif kind == 1:
            proj = mm_ws(xn, lru_w_in, (idx,), _F32, name="lru_in")
            y = lru_core(proj, lru_conv_w[idx], lru_conv_b[idx], lru_w_a[idx], lru_b_a[idx],
                         lru_w_x[idx], lru_b_x[idx], lru_lambda[idx], bsz, seq)
            w_out = lru_w_out
        else:
            proj = mm_ws(xn, ml_w_in, (idx,), _F32, tn=896, name="ml_in")
            y = mlstm_core(proj, ml_b_if[idx], ml_g_norm[idx], bsz, seq, ml_dqk, ml_dv)
            w_out = ml_w_out
        xf, xn = mm_resid_norm(y, w_out, (idx,), xf, norm_g[layer, 2], 1.0, name="mixer_out")

        q = mm_ws(xn, xa_w_q, (layer,), _BF16, name="xa_q")
        kv = mm_ws(mem_n, xa_w_kv, (layer,), _BF16, name="xa_kv")
        o = xattn_core(q, kv, bsz, seq, n_mem)
        xf, xn = mm_resid_norm(o, xa_w_o, (layer,), xf, norm_g[layer, 3], 1.0, name="xa_out")

        h = mm_swiglu(xn, ffn_w_gu, (layer, 1))
        xf = mm_down(h, ffn_w_down, (layer, 1), xf, 0.5)
        if layer + 1 < depth:
            xn = rms_cast(xf, norm_g[layer + 1, 0])
    out = rms_cast(xf, final_norm_g, _F32)
    return out.reshape(bsz, seq, d)
```

```python
import functools
import math

import jax
import jax.numpy as jnp
from jax import lax
from jax.experimental import pallas as pl
from jax.experimental.pallas import tpu as pltpu

EPS = 1e-6
N_MIXERS = 3
HG_HEAD = 128
HG_CHUNK = 64
HG_SUB = 16
HG_GROUP = 4
F_FLOOR = 1e-12
LRU_BLOCKS = 8
CONV_W = 4
LRU_C = 8.0
ML_HEADS = 8
ML_CHUNK = 256
ML_GROUP = 8
GATE_CAP = 15.0
NEG_BIG = -1e30
XA_HEADS = 4
LANE = 128
FF_TILE = 512
SWIGLU_SPLIT = 4
EPILOGUE_SPLIT = 2
VMEM_LIMIT = 56 * 1024 * 1024

_BF16 = jnp.bfloat16
_F32 = jnp.float32


def _params(*sem):
    return pltpu.CompilerParams(dimension_semantics=sem, vmem_limit_bytes=VMEM_LIMIT)


def _tile(dim, pref):
    t = min(dim, pref)
    while dim % t:
        t //= 2
    return t


def _sigmoid(x):
    return 1.0 / (1.0 + jnp.exp(-x))


def _dot(a, b):
    return jnp.dot(a, b, preferred_element_type=_F32)


def _dot_nt(a, b):
    return lax.dot_general(a, b, (((1,), (1,)), ((), ())), preferred_element_type=_F32)


def _dot_tn(a, b):
    return lax.dot_general(a, b, (((0,), (0,)), ((), ())), preferred_element_type=_F32)


def _split3(x):
    hi = x.astype(_BF16)
    r1 = x - hi.astype(_F32)
    mid = r1.astype(_BF16)
    lo = (r1 - mid.astype(_F32)).astype(_BF16)
    return hi, mid, lo


def _dot_exact_lhs(sel, x):
    hi, mid, lo = _split3(x)
    return _dot(sel, hi) + _dot(sel, mid) + _dot(sel, lo)


def _rms(x, g):
    return x * lax.rsqrt(jnp.mean(x * x, axis=-1, keepdims=True) + EPS) * g


def _rms_kernel(x_ref, g_ref, o_ref):
    o_ref[...] = _rms(x_ref[...], g_ref[...]).astype(o_ref.dtype)


def rms_cast(x, g, out_dtype=_BF16):
    m, d = x.shape
    tm = _tile(m, 512)
    return pl.pallas_call(
        _rms_kernel,
        grid=(m // tm,),
        in_specs=[pl.BlockSpec((tm, d), lambda i: (i, 0)),
                  pl.BlockSpec((1, d), lambda i: (0, 0))],
        out_specs=pl.BlockSpec((tm, d), lambda i: (i, 0)),
        out_shape=jax.ShapeDtypeStruct((m, d), out_dtype),
        compiler_params=_params("parallel"),
        name="rms_cast",
    )(x, g.reshape(1, d))


def _wspec(widx, block, index_map, **kw):
    lead = tuple(widx)
    return pl.BlockSpec((None,) * len(lead) + tuple(block), lambda *g: lead + tuple(index_map(*g)), **kw)


def _mm_ws_kernel(a_ref, w_ref, o_ref, wbf_ref):
    @pl.when(pl.program_id(1) == 0)
    def _():
        wbf_ref[...] = w_ref[...].astype(_BF16)

    o_ref[...] = _dot(a_ref[...], wbf_ref[...]).astype(o_ref.dtype)


def mm_ws(a, w, widx, out_dtype, tm=1024, tn=1024, name="mm_ws"):
    m, k = a.shape
    n = w.shape[-1]
    tm = _tile(m, tm)
    nj = -(-n // tn)
    return pl.pallas_call(
        _mm_ws_kernel,
        grid=(nj, m // tm),
        in_specs=[pl.BlockSpec((tm, k), lambda j, i: (i, 0)),
                  _wspec(widx, (k, tn), lambda j, i: (0, j))],
        out_specs=pl.BlockSpec((tm, tn), lambda j, i: (i, j)),
        out_shape=jax.ShapeDtypeStruct((m, nj * tn), out_dtype),
        scratch_shapes=[pltpu.VMEM((k, tn), _BF16)],
        compiler_params=_params("arbitrary", "arbitrary"),
        name=name,
    )(a, w)


def _swiglu_kernel(a_ref, wg_ref, wu_ref, o_ref, wbf_ref, *, tnh, rem):
    @pl.when(pl.program_id(1) == 0)
    def _():
        lead = (0,) * (len(wg_ref.shape) - 2)
        wbf_ref[:, :tnh] = wg_ref[lead].astype(_BF16)
        wbf_ref[:, tnh:] = wu_ref[lead].astype(_BF16)

    tm = a_ref.shape[0]
    sub = tm // SWIGLU_SPLIT
    last = pl.num_programs(0) - 1
    for r in range(SWIGLU_SPLIT):
        rows = slice(r * sub, (r + 1) * sub)
        acc = _dot(a_ref[rows, :], wbf_ref[...])
        gate = acc[:, :tnh]
        up = acc[:, tnh:]
        res = (gate * _sigmoid(gate) * up).astype(o_ref.dtype)
        if rem == tnh:
            o_ref[rows, :] = res
        else:
            o_ref[rows, :rem] = jnp.where(pl.program_id(0) == last, res[:, tnh - rem:], res[:, :rem])
            o_ref[rows, rem:] = res[:, rem:]


def mm_swiglu(a, w_gu, widx, tnh=FF_TILE):
    m, k = a.shape
    d_ff = w_gu.shape[-1] // 2
    tm = _tile(m, 1024)
    nj = -(-d_ff // tnh)

    lead = tuple(widx)
    lead_blk = (1,) * len(lead)

    assert d_ff % LANE == 0 and tnh % LANE == 0

    def start(j, base=0):
        return LANE * (base // LANE + jnp.minimum(j * (tnh // LANE), (d_ff - tnh) // LANE))

    def elem_spec(block, index_map):
        return pl.BlockSpec(tuple(pl.Element(b) for b in block), index_map)

    return pl.pallas_call(
        functools.partial(_swiglu_kernel, tnh=tnh, rem=d_ff - (nj - 1) * tnh),
        grid=(nj, m // tm),
        in_specs=[pl.BlockSpec((tm, k), lambda j, i: (i, 0)),
                  elem_spec(lead_blk + (k, tnh), lambda j, i: lead + (0, start(j))),
                  elem_spec(lead_blk + (k, tnh), lambda j, i: lead + (0, start(j, d_ff)))],
        out_specs=pl.BlockSpec((tm, tnh), lambda j, i: (i, j)),
        out_shape=jax.ShapeDtypeStruct((m, d_ff), _BF16),
        scratch_shapes=[pltpu.VMEM((k, 2 * tnh), _BF16)],
        compiler_params=_params("arbitrary", "arbitrary"),
        name="mm_swiglu",
    )(a, w_gu, w_gu)


def _down_kernel(a_ref, w_ref, r_ref, o_ref, wbf_ref, *, scale):
    @pl.when(pl.program_id(1) == 0)
    def _():
        wbf_ref[...] = w_ref[...].astype(_BF16)

    sub = a_ref.shape[0] // EPILOGUE_SPLIT
    for r in range(EPILOGUE_SPLIT):
        rows = slice(r * sub, (r + 1) * sub)
        o_ref[rows, :] = r_ref[rows, :] + scale * _dot(a_ref[rows, :], wbf_ref[...])


def mm_down(a, w, widx, resid, scale, tm=512, tn=512, name="ffn_down"):
    m, k = a.shape
    d = w.shape[-1]
    tm = _tile(m, tm)
    tn = _tile(d, tn)
    return pl.pallas_call(
        functools.partial(_down_kernel, scale=scale),
        grid=(d // tn, m // tm),
        in_specs=[pl.BlockSpec((tm, k), lambda j, i: (i, 0)),
                  _wspec(widx, (k, tn), lambda j, i: (0, j)),
                  pl.BlockSpec((tm, tn), lambda j, i: (i, j))],
        out_specs=pl.BlockSpec((tm, tn), lambda j, i: (i, j)),
        out_shape=jax.ShapeDtypeStruct((m, d), _F32),
        scratch_shapes=[pltpu.VMEM((k, tn), _BF16)],
        compiler_params=_params("arbitrary", "arbitrary"),
        name=name,
    )(a, w, resid)


def _resid_norm_kernel(a_ref, w_ref, r_ref, g_ref, x_ref, n_ref, wbf_ref, *, scale):
    @pl.when(pl.program_id(0) == 0)
    def _():
        wbf_ref[...] = w_ref[...].astype(_BF16)

    sub = a_ref.shape[0] // EPILOGUE_SPLIT
    for r in range(EPILOGUE_SPLIT):
        rows = slice(r * sub, (r + 1) * sub)
        x = r_ref[rows, :] + scale * _dot(a_ref[rows, :], wbf_ref[...])
        x_ref[rows, :] = x
        n_ref[rows, :] = _rms(x, g_ref[...]).astype(n_ref.dtype)


def mm_resid_norm(a, w, widx, resid, g, scale, name="mm_resid"):
    m, k = a.shape
    d = w.shape[-1]
    tm = _tile(m, 512)
    return pl.pallas_call(
        functools.partial(_resid_norm_kernel, scale=scale),
        grid=(m // tm,),
        in_specs=[pl.BlockSpec((tm, k), lambda i: (i, 0)),
                  _wspec(widx, (k, d), lambda i: (0, 0), pipeline_mode=pl.Buffered(1)),
                  pl.BlockSpec((tm, d), lambda i: (i, 0)),
                  pl.BlockSpec((1, d), lambda i: (0, 0))],
        out_specs=[pl.BlockSpec((tm, d), lambda i: (i, 0)),
                   pl.BlockSpec((tm, d), lambda i: (i, 0))],
        out_shape=[jax.ShapeDtypeStruct((m, d), _F32),
                   jax.ShapeDtypeStruct((m, d), _BF16)],
        scratch_shapes=[pltpu.VMEM((k, d), _BF16)],
        compiler_params=_params("arbitrary"),
        name=name,
    )(a, w, resid, g.reshape(1, d))


def _lb_kernel(p_ref, o_ref):
    p = p_ref[...]
    depth = p.shape[0]
    e = jnp.exp(p - jnp.max(p, axis=0, keepdims=True))
    sm = e / jnp.sum(e, axis=0, keepdims=True)
    run = jnp.zeros_like(sm[0:1])
    for layer in range(depth):
        run = run + sm[layer:layer + 1]
        o_ref[layer:layer + 1, :] = run - sm[0:1]


def hgrn_lower_bounds(p):
    return pl.pallas_call(
        _lb_kernel,
        out_shape=jax.ShapeDtypeStruct(p.shape, _F32),
        name="hgrn_lb",
    )(p)


def _hgrn_kernel(q_ref, f_ref, v_ref, g_ref, lb_ref, gn_ref, o_ref, st_ref, *, nchunk, nh):
    c = HG_CHUNK
    sb = HG_SUB
    d = HG_HEAD
    ru = 8

    @pl.when(pl.program_id(2) == 0)
    def _():
        st_ref[...] = jnp.zeros_like(st_ref)

    gn = gn_ref[...]
    row = lax.broadcasted_iota(jnp.int32, (c, c), 0)
    col = lax.broadcasted_iota(jnp.int32, (c, c), 1)
    tril = (row >= col).astype(_BF16)
    future = col > row
    lane_u = lax.broadcasted_iota(jnp.int32, (ru, c), 1)

    def head_chunk(r0, hh):
        cs = slice(hh * d, (hh + 1) * d)
        lb = lb_ref[:, cs]
        qp = q_ref[pl.ds(r0, c), cs]
        fp = f_ref[pl.ds(r0, c), cs]
        v = v_ref[pl.ds(r0, c), cs]
        gp = g_ref[pl.ds(r0, c), cs]
        q = qp * _sigmoid(qp)
        e_abs = jnp.exp(-jnp.abs(fp))
        r_abs = 1.0 / (1.0 + e_abs)
        er_abs = e_abs * r_abs
        pos = fp >= 0.0
        f = lb + (1.0 - lb) * jnp.where(pos, r_abs, er_abs)
        k = (1.0 - lb) * jnp.where(pos, er_abs, r_abs)
        lf = jnp.log2(jnp.maximum(f, F_FLOOR))
        b = _dot_exact_lhs(tril, lf)
        b_last = b[c - 1:c, :]
        bk = b - jnp.log2(k)
        st = st_ref[hh]
        v16 = v.astype(_BF16)

        units = []
        for bi in range(c // sb):
            lo = bi * sb
            if bi > 0:
                ref = b[lo - 1:lo, :]
                qt = q[lo:lo + sb, :] * jnp.exp2(b[lo:lo + sb, :] - ref)
                kt = k[:lo, :] * jnp.exp2(ref - b[:lo, :])
                kt = jnp.concatenate([kt, jnp.zeros((c - lo, d), _F32)], axis=0)
                off = _dot_nt(qt.astype(_BF16), kt.astype(_BF16))
            for uu in range(sb // ru):
                t0 = lo + uu * ru
                b_u = b[t0:t0 + ru, :]
                q_u = q[t0:t0 + ru, :]
                sc = off[uu * ru:(uu + 1) * ru, :] if bi > 0 else jnp.zeros((ru, c), _F32)
                for s in range(lo, t0 + ru):
                    w = q_u * jnp.exp2(b_u - bk[s:s + 1, :])
                    sc = jnp.where(lane_u == s, jnp.sum(w, axis=-1, keepdims=True), sc)
                units.append(sc)
        scores = jnp.where(future, 0.0, jnp.concatenate(units, axis=0))

        q_state = (q * jnp.exp2(b)).astype(_BF16)
        o = _dot(scores.astype(_BF16), v16) + _dot_nt(q_state, st.astype(_BF16))
        k_state = (k * jnp.exp2(b_last - b)).astype(_BF16)
        st_ref[hh] = st * jnp.exp2(b_last) + _dot_tn(v16, k_state)

        on = o * lax.rsqrt(jnp.mean(o * o, axis=-1, keepdims=True) + EPS) * gn
        o_ref[pl.ds(r0, c), cs] = (on * (gp * _sigmoid(gp))).astype(o_ref.dtype)

    def chunk_pair(ci, carry):
        for half in range(2):
            r0 = pl.multiple_of((2 * ci + half) * c, c)
            for hh in range(nh):
                head_chunk(r0, hh)
        return carry

    assert nchunk % 2 == 0
    lax.fori_loop(0, nchunk // 2, chunk_pair, 0)


def hgrn_core(proj, lb, g_norm, bsz, seq):
    m = proj.shape[0]
    width = proj.shape[1] // 4
    heads = width // HG_HEAD
    nh = HG_GROUP
    gw = nh * HG_HEAD
    ngrp = heads // nh
    tb = _tile(seq, 512)
    nsblk = seq // tb

    def spec(off):
        return pl.BlockSpec((tb, gw), lambda bb, h, s: (bb * nsblk + s, off * ngrp + h))

    return pl.pallas_call(
        functools.partial(_hgrn_kernel, nchunk=tb // HG_CHUNK, nh=nh),
        grid=(bsz, ngrp, nsblk),
        in_specs=[spec(0), spec(1), spec(2), spec(3),
                  pl.BlockSpec((1, gw), lambda bb, h, s: (0, h)),
                  pl.BlockSpec((1, HG_HEAD), lambda bb, h, s: (0, 0))],
        out_specs=pl.BlockSpec((tb, gw), lambda bb, h, s: (bb * nsblk + s, h)),
        out_shape=jax.ShapeDtypeStruct((m, width), _BF16),
        scratch_shapes=[pltpu.VMEM((nh, HG_HEAD, HG_HEAD), _F32)],
        compiler_params=_params("parallel", "parallel", "arbitrary"),
        name="hgrn_core",
    )(proj, proj, proj, proj, lb.reshape(1, width), g_norm.reshape(1, HG_HEAD))


def _gelu_tanh(x):
    c0 = math.sqrt(2.0 / math.pi)
    return 0.5 * x * (1.0 + jnp.tanh(c0 * (x + 0.044715 * (x * x * x))))


def _lru_kernel(gate_ref, u_ref, cw_ref, cb_ref, wa_ref, ba_ref, wx_ref, bx_ref, lam_ref,
                o_ref, ext_ref, sa_ref, sx_ref, h_ref, *, ts, pad):
    width = u_ref.shape[1]
    blk = width // LRU_BLOCKS
    halo = 8

    @pl.when(pl.program_id(1) == 0)
    def _():
        ext_ref[0:halo, :] = jnp.zeros((halo, width), _F32)
        h_ref[...] = jnp.zeros_like(h_ref)

    @pl.when(pl.program_id(1) > 0)
    def _():
        ext_ref[0:halo, :] = ext_ref[ts:ts + halo, :]

    ext_ref[halo:halo + ts, :] = u_ref[...]
    uc = cb_ref[...] + cw_ref[0:1, :] * ext_ref[pl.ds(halo - 3, ts), :]
    for j in range(1, CONV_W):
        uc = uc + cw_ref[j:j + 1, :] * ext_ref[pl.ds(halo - 3 + j, ts), :]

    u16 = uc.astype(_BF16)
    r_parts = []
    i_parts = []
    for n in range(LRU_BLOCKS):
        ub = u16[:, n * blk:(n + 1) * blk]
        r_parts.append(_dot(ub, wa_ref[n].astype(_BF16)))
        i_parts.append(_dot(ub, wx_ref[n].astype(_BF16)))
    r = _sigmoid(jnp.concatenate(r_parts, axis=1) + ba_ref[...])
    ig = _sigmoid(jnp.concatenate(i_parts, axis=1) + bx_ref[...])

    lam = lam_ref[...]
    softplus = jnp.maximum(-lam, 0.0) + jnp.log(1.0 + jnp.exp(-jnp.abs(lam)))
    log_a = -LRU_C * r * softplus
    a = jnp.exp(log_a)
    inp = jnp.sqrt(jnp.maximum(1.0 - a * a, 0.0)) * (ig * uc)

    sa_ref[0:pad, :] = jnp.ones((pad, width), _F32)
    sx_ref[0:pad, :] = jnp.zeros((pad, width), _F32)
    shift = 1
    while shift < ts:
        sa_ref[pad:pad + ts, :] = a
        sx_ref[pad:pad + ts, :] = inp
        a_sh = sa_ref[pl.ds(pad - shift, ts), :]
        x_sh = sx_ref[pl.ds(pad - shift, ts), :]
        inp = a * x_sh + inp
        a = a * a_sh
        shift *= 2

    h = inp + a * h_ref[0:1, :]
    h_ref[0:1, :] = h[ts - 1:ts, :]
    o_ref[...] = (_gelu_tanh(gate_ref[...]) * h).astype(o_ref.dtype)


def lru_core(proj, conv_w, conv_b, w_a, b_a, w_x, b_x, lam, bsz, seq):
    m = proj.shape[0]
    width = proj.shape[1] // 2
    blk = width // LRU_BLOCKS
    ts = _tile(seq, 256)
    pad = ts // 2 if ts >= 16 else 8
    pad = max(pad, 8)
    nsblk = seq // ts
    vec = lambda t: t.reshape(1, width)
    row_spec = pl.BlockSpec((1, width), lambda bb, s: (0, 0))
    w_spec = pl.BlockSpec((LRU_BLOCKS, blk, blk), lambda bb, s: (0, 0, 0))
    return pl.pallas_call(
        functools.partial(_lru_kernel, ts=ts, pad=pad),
        grid=(bsz, nsblk),
        in_specs=[pl.BlockSpec((ts, width), lambda bb, s: (bb * nsblk + s, 0)),
                  pl.BlockSpec((ts, width), lambda bb, s: (bb * nsblk + s, 1)),
                  pl.BlockSpec((CONV_W, width), lambda bb, s: (0, 0)),
                  row_spec, w_spec, row_spec, w_spec, row_spec, row_spec],
        out_specs=pl.BlockSpec((ts, width), lambda bb, s: (bb * nsblk + s, 0)),
        out_shape=jax.ShapeDtypeStruct((m, width), _BF16),
        scratch_shapes=[pltpu.VMEM((ts + 8, width), _F32),
                        pltpu.VMEM((ts + pad, width), _F32),
                        pltpu.VMEM((ts + pad, width), _F32),
                        pltpu.VMEM((8, width), _F32)],
        compiler_params=_params("parallel", "arbitrary"),
        name="lru_core",
    )(proj, proj, conv_w, vec(conv_b), w_a, vec(b_a), w_x, vec(b_x), vec(lam))


def _softcap(t):
    return GATE_CAP * jnp.tanh(t / GATE_CAP)


def _log_sigmoid(x):
    return jnp.minimum(x, 0.0) - jnp.log(1.0 + jnp.exp(-jnp.abs(x)))


def _mlstm_kernel(q_ref, k_ref, v_ref, og_ref, gt_ref, bif_ref, gn_ref, o_ref,
                  c_ref, n_ref, m_ref, *, nchunk, c, dqk, dv, nh):
    h_base = pl.program_id(1) * nh

    @pl.when(pl.program_id(2) == 0)
    def _():
        c_ref[...] = jnp.zeros_like(c_ref)
        n_ref[...] = jnp.zeros_like(n_ref)
        m_ref[...] = jnp.zeros_like(m_ref)

    row = lax.broadcasted_iota(jnp.int32, (c, c), 0)
    col = lax.broadcasted_iota(jnp.int32, (c, c), 1)
    causal = row >= col
    tril = causal.astype(_BF16)
    eye = (lax.broadcasted_iota(jnp.int32, (2 * ML_HEADS, LANE), 0)
           == lax.broadcasted_iota(jnp.int32, (2 * ML_HEADS, LANE), 1)).astype(_BF16)
    sub16 = lax.broadcasted_iota(jnp.int32, (2 * ML_HEADS, c), 0)
    lane1 = lax.broadcasted_iota(jnp.int32, (1, LANE), 1)
    gn = gn_ref[...]
    bif = bif_ref[...]
    kscale = dqk ** -0.5

    def head_chunk(r0, hh, gl, cum, gl_rows, cum_rows):
        h_id = h_base + hh
        qs = slice(hh * dqk, (hh + 1) * dqk)
        vs = slice(hh * dv, (hh + 1) * dv)
        q = q_ref[pl.ds(r0, c), qs]
        k = k_ref[pl.ds(r0, c), qs] * kscale
        v = v_ref[pl.ds(r0, c), vs]
        og = og_ref[pl.ds(r0, c), vs]
        b_col = jnp.sum(jnp.where(lane1 == h_id + ML_HEADS, cum, 0.0), axis=-1, keepdims=True)
        i_col = jnp.sum(jnp.where(lane1 == h_id, gl, 0.0), axis=-1, keepdims=True)
        b_row = jnp.sum(jnp.where(sub16 == h_id + ML_HEADS, cum_rows, 0.0), axis=0, keepdims=True)
        i_row = jnp.sum(jnp.where(sub16 == h_id, gl_rows, 0.0), axis=0, keepdims=True)

        m_st = m_ref[hh]
        c_st = c_ref[hh]
        n_st = n_ref[hh]

        d_mat = jnp.where(causal, b_col - b_row + i_row, NEG_BIG)
        inter = b_col + m_st
        m_t = jnp.maximum(inter, jnp.max(d_mat, axis=-1, keepdims=True))
        w_intra = jnp.where(causal, jnp.exp(jnp.minimum(d_mat - m_t, 0.0)), 0.0)
        w_inter = jnp.exp(inter - m_t)
        q16 = q.astype(_BF16)
        k16 = k.astype(_BF16)
        v16 = v.astype(_BF16)
        qk = _dot_nt(q16, k16) * w_intra
        num = _dot(qk.astype(_BF16), v16) + w_inter * _dot(q16, c_st.astype(_BF16))
        den = jnp.sum(qk, axis=-1, keepdims=True) + w_inter * jnp.sum(q * n_st, axis=-1, keepdims=True)
        hval = num / jnp.maximum(jnp.abs(den), jnp.exp(-m_t))

        g_tot = b_col[c - 1:c, :]
        upd_col = g_tot - b_col + i_col
        m_new = jnp.maximum(g_tot + m_st, jnp.max(upd_col, axis=0, keepdims=True))
        w_upd = jnp.exp(upd_col - m_new)
        decay = jnp.exp(g_tot + m_st - m_new)
        kw = k * w_upd
        c_ref[hh] = decay * c_st + _dot_tn(kw.astype(_BF16), v16)
        n_ref[hh] = decay * n_st + jnp.sum(kw, axis=0, keepdims=True)
        m_ref[hh] = m_new

        hn = hval * lax.rsqrt(jnp.mean(hval * hval, axis=-1, keepdims=True) + EPS) * gn
        o_ref[pl.ds(r0, c), vs] = (hn * _sigmoid(og)).astype(o_ref.dtype)

    def chunk(ci, carry):
        r0 = pl.multiple_of(ci * c, c)
        gt = jnp.where(lane1 < 2 * ML_HEADS, gt_ref[pl.ds(r0, c), :], 0.0)
        pre = _softcap(gt + bif)
        is_f = jnp.logical_and(lane1 >= ML_HEADS, lane1 < 2 * ML_HEADS)
        gl = jnp.where(is_f, _log_sigmoid(pre), pre)
        cum = _dot_exact_lhs(tril, gl)
        ch, cm, cl = _split3(cum)
        cum_rows = _dot_nt(eye, ch) + _dot_nt(eye, cm) + _dot_nt(eye, cl)
        gh, gm, gl3 = _split3(gl)
        gl_rows = _dot_nt(eye, gh) + _dot_nt(eye, gm) + _dot_nt(eye, gl3)
        for hh in range(nh):
            head_chunk(r0, hh, gl, cum, gl_rows, cum_rows)
        return carry

    lax.fori_loop(0, nchunk, chunk, 0)


def mlstm_core(proj, b_if, g_norm, bsz, seq, dqk, dv):
    m = proj.shape[0]
    heads = ML_HEADS
    hv = heads * dv
    c = _tile(seq, ML_CHUNK)
    tb = _tile(seq, 512)
    nsblk = seq // tb
    gate_blk = (2 * heads * dqk + 2 * hv) // LANE
    bif = jnp.zeros((1, LANE), _F32).at[0, :2 * heads].set(b_if.reshape(-1))
    nh = ML_GROUP
    ngrp = heads // nh
    qw = nh * dqk
    vw = nh * dv
    v_off = (2 * heads * dqk) // vw
    return pl.pallas_call(
        functools.partial(_mlstm_kernel, nchunk=tb // c, c=c, dqk=dqk, dv=dv, nh=nh),
        grid=(bsz, ngrp, nsblk),
        in_specs=[pl.BlockSpec((tb, qw), lambda bb, h, s: (bb * nsblk + s, h)),
                  pl.BlockSpec((tb, qw), lambda bb, h, s: (bb * nsblk + s, ngrp + h)),
                  pl.BlockSpec((tb, vw), lambda bb, h, s: (bb * nsblk + s, v_off + h)),
                  pl.BlockSpec((tb, vw), lambda bb, h, s: (bb * nsblk + s, v_off + ngrp + h)),
                  pl.BlockSpec((tb, LANE), lambda bb, h, s: (bb * nsblk + s, gate_blk)),
                  pl.BlockSpec((1, LANE), lambda bb, h, s: (0, 0)),
                  pl.BlockSpec((1, dv), lambda bb, h, s: (0, 0))],
        out_specs=pl.BlockSpec((tb, vw), lambda bb, h, s: (bb * nsblk + s, h)),
        out_shape=jax.ShapeDtypeStruct((m, hv), _BF16),
        scratch_shapes=[pltpu.VMEM((nh, dqk, dv), _F32),
                        pltpu.VMEM((nh, 1, dqk), _F32),
                        pltpu.VMEM((nh, 1, 1), _F32)],
        compiler_params=_params("parallel", "parallel", "arbitrary"),
        name="mlstm_core",
    )(proj, proj, proj, proj, proj, bif, g_norm.reshape(1, dv))


def _xattn_kernel(q_ref, k_ref, v_ref, o_ref, *, heads):
    d = q_ref.shape[1]
    hd = d // heads
    scale = hd ** -0.5
    for h in range(heads):
        qh = q_ref[:, h * hd:(h + 1) * hd]
        kh = k_ref[:, h * hd:(h + 1) * hd]
        vh = v_ref[:, h * hd:(h + 1) * hd]
        s = _dot_nt(qh, kh) * scale
        s = s - jnp.max(s, axis=-1, keepdims=True)
        e = jnp.exp(s)
        p = e / jnp.sum(e, axis=-1, keepdims=True)
        o_ref[:, h * hd:(h + 1) * hd] = _dot(p.astype(_BF16), vh).astype(o_ref.dtype)


def xattn_core(q, kv, bsz, seq, n_mem):
    m, d = q.shape
    tq = _tile(seq, 512)
    nq = seq // tq
    return pl.pallas_call(
        functools.partial(_xattn_kernel, heads=XA_HEADS),
        grid=(bsz, nq),
        in_specs=[pl.BlockSpec((tq, d), lambda bb, s: (bb * nq + s, 0)),
                  pl.BlockSpec((n_mem, d), lambda bb, s: (bb, 0)),
                  pl.BlockSpec((n_mem, d), lambda bb, s: (bb, 1))],
        out_specs=pl.BlockSpec((tq, d), lambda bb, s: (bb * nq + s, 0)),
        out_shape=jax.ShapeDtypeStruct((m, d), _BF16),
        compiler_params=_params("parallel", "parallel"),
        name="xattn_core",
    )(q, kv, kv)


def kernel(x, mem, mem_norm_g, norm_g, final_norm_g, ffn_w_gu, ffn_w_down, xa_w_q, xa_w_kv, xa_w_o,
           hg_lb_param, hg_w_in, hg_g_norm, hg_w_out,
           lru_w_in, lru_conv_w, lru_conv_b, lru_w_a, lru_b_a, lru_w_x, lru_b_x, lru_lambda, lru_w_out,
           ml_w_in, ml_b_if, ml_g_norm, ml_w_out):
    bsz, seq, d = x.shape
    n_mem = mem.shape[1]
    depth = norm_g.shape[0]
    m = bsz * seq
    ml_dqk = d // (2 * ML_HEADS)
    ml_dv = d // ML_HEADS

    xf = x.reshape(m, d)
    mem_n = rms_cast(mem.reshape(bsz * n_mem, d), mem_norm_g)
    lb_all = hgrn_lower_bounds(hg_lb_param)

    xn = rms_cast(xf, norm_g[0, 0])
    for layer in range(depth):
        kind = layer % N_MIXERS
        idx = layer // N_MIXERS

        h = mm_swiglu(xn, ffn_w_gu, (layer, 0))
        xf = mm_down(h, ffn_w_down, (layer, 0), xf, 0.5)
        xn = rms_cast(xf, norm_g[layer, 1])

        if kind == 0:
            proj = mm_ws(xn, hg_w_in, (idx,), _F32, name="hg_in")
            y = hgrn_core(proj, lb_all[layer], hg_g_norm[idx], bsz, seq)
            w_out = hg_w_out
        elif kind == 1:
            proj = mm_ws(xn, lru_w_in, (idx,), _F32, name="lru_in")
            y = lru_core(proj, lru_conv_w[idx], lru_conv_b[idx], lru_w_a[idx], lru_b_a[idx],
                         lru_w_x[idx], lru_b_x[idx], lru_lambda[idx], bsz, seq)
            w_out = lru_w_out
        else:
            proj = mm_ws(xn, ml_w_in, (idx,), _F32, tn=896, name="ml_in")
            y = mlstm_core(proj, ml_b_if[idx], ml_g_norm[idx], bsz, seq, ml_dqk, ml_dv)
            w_out = ml_w_out
        xf, xn = mm_resid_norm(y, w_out, (idx,), xf, norm_g[layer, 2], 1.0, name="mixer_out")

        q = mm_ws(xn, xa_w_q, (layer,), _BF16, name="xa_q")
        kv = mm_ws(mem_n, xa_w_kv, (layer,), _BF16, name="xa_kv")
        o = xattn_core(q, kv, bsz, seq, n_mem)
        xf, xn = mm_resid_norm(o, xa_w_o, (layer,), xf, norm_g[layer, 3], 1.0, name="xa_out")

        h = mm_swiglu(xn, ffn_w_gu, (layer, 1))
        xf = mm_down(h, ffn_w_down, (layer, 1), xf, 0.5)
        if layer + 1 < depth:
            xn = rms_cast(xf, norm_g[layer + 1, 0])
    out = rms_cast(xf, final_norm_g, _F32)
    return out.reshape(bsz, seq, d)
```

```python
import functools
import math

import jax
import jax.numpy as jnp
from jax import lax
from jax.experimental import pallas as pl
from jax.experimental.pallas import tpu as pltpu

EPS = 1e-6
N_MIXERS = 3
HG_HEAD = 128
HG_CHUNK = 64
HG_SUB = 16
HG_GROUP = 4
F_FLOOR = 1e-12
LRU_BLOCKS = 8
CONV_W = 4
LRU_C = 8.0
ML_HEADS = 8
ML_CHUNK = 256
ML_GROUP = 8
GATE_CAP = 15.0
NEG_BIG = -1e30
XA_HEADS = 4
LANE = 128
FF_TILE = 512
SWIGLU_SPLIT = 4
EPILOGUE_SPLIT = 2
DOWN_ROWS = 512
DOWN_SPLIT = 4
VMEM_LIMIT = 56 * 1024 * 1024

_BF16 = jnp.bfloat16
_F32 = jnp.float32


def _params(*sem):
    return pltpu.CompilerParams(dimension_semantics=sem, vmem_limit_bytes=VMEM_LIMIT)


def _tile(dim, pref):
    t = min(dim, pref)
    while dim % t:
        t //= 2
    return t


def _sigmoid(x):
    return 1.0 / (1.0 + jnp.exp(-x))


def _dot(a, b):
    return jnp.dot(a, b, preferred_element_type=_F32)


def _dot_nt(a, b):
    return lax.dot_general(a, b, (((1,), (1,)), ((), ())), preferred_element_type=_F32)


def _dot_tn(a, b):
    return lax.dot_general(a, b, (((0,), (0,)), ((), ())), preferred_element_type=_F32)


def _split3(x):
    hi = x.astype(_BF16)
    r1 = x - hi.astype(_F32)
    mid = r1.astype(_BF16)
    lo = (r1 - mid.astype(_F32)).astype(_BF16)
    return hi, mid, lo


def _dot_exact_lhs(sel, x):
    hi, mid, lo = _split3(x)
    return _dot(sel, hi) + _dot(sel, mid) + _dot(sel, lo)


def _rms(x, g):
    return x * lax.rsqrt(jnp.mean(x * x, axis=-1, keepdims=True) + EPS) * g


def _rms_kernel(x_ref, g_ref, o_ref):
    o_ref[...] = _rms(x_ref[...], g_ref[...]).astype(o_ref.dtype)


def rms_cast(x, g, out_dtype=_BF16):
    m, d = x.shape
    tm = _tile(m, 512)
    return pl.pallas_call(
        _rms_kernel,
        grid=(m // tm,),
        in_specs=[pl.BlockSpec((tm, d), lambda i: (i, 0)),
                  pl.BlockSpec((1, d), lambda i: (0, 0))],
        out_specs=pl.BlockSpec((tm, d), lambda i: (i, 0)),
        out_shape=jax.ShapeDtypeStruct((m, d), out_dtype),
        compiler_params=_params("parallel"),
        name="rms_cast",
    )(x, g.reshape(1, d))


def _wspec(widx, block, index_map, **kw):
    lead = tuple(widx)
    return pl.BlockSpec((None,) * len(lead) + tuple(block), lambda *g: lead + tuple(index_map(*g)), **kw)


def _mm_ws_kernel(a_ref, w_ref, o_ref, wbf_ref):
    @pl.when(pl.program_id(1) == 0)
    def _():
        wbf_ref[...] = w_ref[...].astype(_BF16)

    o_ref[...] = _dot(a_ref[...], wbf_ref[...]).astype(o_ref.dtype)


def mm_ws(a, w, widx, out_dtype, tm=1024, tn=1024, name="mm_ws"):
    m, k = a.shape
    n = w.shape[-1]
    tm = _tile(m, tm)
    nj = -(-n // tn)
    return pl.pallas_call(
        _mm_ws_kernel,
        grid=(nj, m // tm),
        in_specs=[pl.BlockSpec((tm, k), lambda j, i: (i, 0)),
                  _wspec(widx, (k, tn), lambda j, i: (0, j))],
        out_specs=pl.BlockSpec((tm, tn), lambda j, i: (i, j)),
        out_shape=jax.ShapeDtypeStruct((m, nj * tn), out_dtype),
        scratch_shapes=[pltpu.VMEM((k, tn), _BF16)],
        compiler_params=_params("arbitrary", "arbitrary"),
        name=name,
    )(a, w)


def _swiglu_kernel(a_ref, wg_ref, wu_ref, o_ref, wbf_ref, *, tnh, rem):
    @pl.when(pl.program_id(1) == 0)
    def _():
        lead = (0,) * (len(wg_ref.shape) - 2)
        wbf_ref[:, :tnh] = wg_ref[lead].astype(_BF16)
        wbf_ref[:, tnh:] = wu_ref[lead].astype(_BF16)

    tm = a_ref.shape[0]
    sub = tm // SWIGLU_SPLIT
    last = pl.num_programs(0) - 1
    for r in range(SWIGLU_SPLIT):
        rows = slice(r * sub, (r + 1) * sub)
        acc = _dot(a_ref[rows, :], wbf_ref[...])
        gate = acc[:, :tnh]
        up = acc[:, tnh:]
        res = (gate * _sigmoid(gate) * up).astype(o_ref.dtype)
        if rem == tnh:
            o_ref[rows, :] = res
        else:
            o_ref[rows, :rem] = jnp.where(pl.program_id(0) == last, res[:, tnh - rem:], res[:, :rem])
            o_ref[rows, rem:] = res[:, rem:]


def mm_swiglu(a, w_gu, widx, tnh=FF_TILE):
    m, k = a.shape
    d_ff = w_gu.shape[-1] // 2
    tm = _tile(m, 1024)
    nj = -(-d_ff // tnh)

    lead = tuple(widx)
    lead_blk = (1,) * len(lead)

    assert d_ff % LANE == 0 and tnh % LANE == 0

    def start(j, base=0):
        return LANE * (base // LANE + jnp.minimum(j * (tnh // LANE), (d_ff - tnh) // LANE))

    def elem_spec(block, index_map):
        return pl.BlockSpec(tuple(pl.Element(b) for b in block), index_map)

    return pl.pallas_call(
        functools.partial(_swiglu_kernel, tnh=tnh, rem=d_ff - (nj - 1) * tnh),
        grid=(nj, m // tm),
        in_specs=[pl.BlockSpec((tm, k), lambda j, i: (i, 0)),
                  elem_spec(lead_blk + (k, tnh), lambda j, i: lead + (0, start(j))),
                  elem_spec(lead_blk + (k, tnh), lambda j, i: lead + (0, start(j, d_ff)))],
        out_specs=pl.BlockSpec((tm, tnh), lambda j, i: (i, j)),
        out_shape=jax.ShapeDtypeStruct((m, d_ff), _BF16),
        scratch_shapes=[pltpu.VMEM((k, 2 * tnh), _BF16)],
        compiler_params=_params("arbitrary", "arbitrary"),
        name="mm_swiglu",
    )(a, w_gu, w_gu)


def _cast_kernel(w_ref, o_ref):
    o_ref[...] = w_ref[...].astype(o_ref.dtype)


def cast_weight(w, widx, row_blocks=8):
    k, d = w.shape[-2:]
    tk = k // row_blocks
    assert tk * row_blocks == k and tk % 16 == 0
    return pl.pallas_call(
        _cast_kernel,
        grid=(row_blocks,),
        in_specs=[_wspec(widx, (tk, d), lambda i: (i, 0))],
        out_specs=pl.BlockSpec((tk, d), lambda i: (i, 0)),
        out_shape=jax.ShapeDtypeStruct((k, d), _BF16),
        compiler_params=_params("parallel"),
        name="cast_weight",
    )(w)


def _down_norm_kernel(a_ref, w_ref, r_ref, g_ref, *out_refs, scale, emit_x):
    n_ref = out_refs[-1]
    sub = a_ref.shape[0] // DOWN_SPLIT
    for r in range(DOWN_SPLIT):
        rows = slice(r * sub, (r + 1) * sub)
        x = r_ref[rows, :] + scale * _dot(a_ref[rows, :], w_ref[...])
        if emit_x:
            out_refs[0][rows, :] = x
        n_ref[rows, :] = _rms(x, g_ref[...]).astype(n_ref.dtype)


def mm_down_norm(a, w16, resid, g, scale, *, emit_x=True, norm_dtype=_BF16, name="ffn_down"):
    m, k = a.shape
    d = w16.shape[-1]
    tm = _tile(m, DOWN_ROWS)
    row_spec = pl.BlockSpec((tm, d), lambda i: (i, 0))
    out_shape = [jax.ShapeDtypeStruct((m, d), norm_dtype)]
    if emit_x:
        out_shape.insert(0, jax.ShapeDtypeStruct((m, d), _F32))
    res = pl.pallas_call(
        functools.partial(_down_norm_kernel, scale=scale, emit_x=emit_x),
        grid=(m // tm,),
        in_specs=[pl.BlockSpec((tm, k), lambda i: (i, 0)),
                  pl.BlockSpec((k, d), lambda i: (0, 0), pipeline_mode=pl.Buffered(1)),
                  row_spec,
                  pl.BlockSpec((1, d), lambda i: (0, 0))],
        out_specs=[row_spec] * len(out_shape),
        out_shape=out_shape,
        compiler_params=_params("parallel"),
        name=name,
    )(a, w16, resid, g.reshape(1, d))
    return (res[0], res[1]) if emit_x else (None, res[0])


def _resid_norm_kernel(a_ref, w_ref, r_ref, g_ref, x_ref, n_ref, wbf_ref, *, scale):
    @pl.when(pl.program_id(0) == 0)
    def _():
        wbf_ref[...] = w_ref[...].astype(_BF16)

    sub = a_ref.shape[0] // EPILOGUE_SPLIT
    for r in range(EPILOGUE_SPLIT):
        rows = slice(r * sub, (r + 1) * sub)
        x = r_ref[rows, :] + scale * _dot(a_ref[rows, :], wbf_ref[...])
        x_ref[rows, :] = x
        n_ref[rows, :] = _rms(x, g_ref[...]).astype(n_ref.dtype)


def mm_resid_norm(a, w, widx, resid, g, scale, name="mm_resid"):
    m, k = a.shape
    d = w.shape[-1]
    tm = _tile(m, 512)
    return pl.pallas_call(
        functools.partial(_resid_norm_kernel, scale=scale),
        grid=(m // tm,),
        in_specs=[pl.BlockSpec((tm, k), lambda i: (i, 0)),
                  _wspec(widx, (k, d), lambda i: (0, 0), pipeline_mode=pl.Buffered(1)),
                  pl.BlockSpec((tm, d), lambda i: (i, 0)),
                  pl.BlockSpec((1, d), lambda i: (0, 0))],
        out_specs=[pl.BlockSpec((tm, d), lambda i: (i, 0)),
                   pl.BlockSpec((tm, d), lambda i: (i, 0))],
        out_shape=[jax.ShapeDtypeStruct((m, d), _F32),
                   jax.ShapeDtypeStruct((m, d), _BF16)],
        scratch_shapes=[pltpu.VMEM((k, d), _BF16)],
        compiler_params=_params("arbitrary"),
        name=name,
    )(a, w, resid, g.reshape(1, d))


def _lb_kernel(p_ref, o_ref):
    p = p_ref[...]
    depth = p.shape[0]
    e = jnp.exp(p - jnp.max(p, axis=0, keepdims=True))
    sm = e / jnp.sum(e, axis=0, keepdims=True)
    run = jnp.zeros_like(sm[0:1])
    for layer in range(depth):
        run = run + sm[layer:layer + 1]
        o_ref[layer:layer + 1, :] = run - sm[0:1]


def hgrn_lower_bounds(p):
    return pl.pallas_call(
        _lb_kernel,
        out_shape=jax.ShapeDtypeStruct(p.shape, _F32),
        name="hgrn_lb",
    )(p)


def _hgrn_kernel(q_ref, f_ref, v_ref, g_ref, lb_ref, gn_ref, o_ref, st_ref, *, nchunk, nh):
    c = HG_CHUNK
    sb = HG_SUB
    d = HG_HEAD
    ru = 8

    @pl.when(pl.program_id(2) == 0)
    def _():
        st_ref[...] = jnp.zeros_like(st_ref)

    gn = gn_ref[...]
    row = lax.broadcasted_iota(jnp.int32, (c, c), 0)
    col = lax.broadcasted_iota(jnp.int32, (c, c), 1)
    tril = (row >= col).astype(_BF16)
    future = col > row
    lane_u = lax.broadcasted_iota(jnp.int32, (ru, c), 1)

    def head_chunk(r0, hh):
        cs = slice(hh * d, (hh + 1) * d)
        lb = lb_ref[:, cs]
        qp = q_ref[pl.ds(r0, c), cs]
        fp = f_ref[pl.ds(r0, c), cs]
        v = v_ref[pl.ds(r0, c), cs]
        gp = g_ref[pl.ds(r0, c), cs]
        q = qp * _sigmoid(qp)
        e_abs = jnp.exp(-jnp.abs(fp))
        r_abs = 1.0 / (1.0 + e_abs)
        er_abs = e_abs * r_abs
        pos = fp >= 0.0
        f = lb + (1.0 - lb) * jnp.where(pos, r_abs, er_abs)
        k = (1.0 - lb) * jnp.where(pos, er_abs, r_abs)
        lf = jnp.log2(jnp.maximum(f, F_FLOOR))
        b = _dot_exact_lhs(tril, lf)
        b_last = b[c - 1:c, :]
        bk = b - jnp.log2(k)
        st = st_ref[hh]
        v16 = v.astype(_BF16)

        units = []
        for bi in range(c // sb):
            lo = bi * sb
            if bi > 0:
                ref = b[lo - 1:lo, :]
                qt = q[lo:lo + sb, :] * jnp.exp2(b[lo:lo + sb, :] - ref)
                kt = k[:lo, :] * jnp.exp2(ref - b[:lo, :])
                kt = jnp.concatenate([kt, jnp.zeros((c - lo, d), _F32)], axis=0)
                off = _dot_nt(qt.astype(_BF16), kt.astype(_BF16))
            for uu in range(sb // ru):
                t0 = lo + uu * ru
                b_u = b[t0:t0 + ru, :]
                q_u = q[t0:t0 + ru, :]
                sc = off[uu * ru:(uu + 1) * ru, :] if bi > 0 else jnp.zeros((ru, c), _F32)
                for s in range(lo, t0 + ru):
                    w = q_u * jnp.exp2(b_u - bk[s:s + 1, :])
                    sc = jnp.where(lane_u == s, jnp.sum(w, axis=-1, keepdims=True), sc)
                units.append(sc)
        scores = jnp.where(future, 0.0, jnp.concatenate(units, axis=0))

        q_state = (q * jnp.exp2(b)).astype(_BF16)
        o = _dot(scores.astype(_BF16), v16) + _dot_nt(q_state, st.astype(_BF16))
        k_state = (k * jnp.exp2(b_last - b)).astype(_BF16)
        st_ref[hh] = st * jnp.exp2(b_last) + _dot_tn(v16, k_state)

        on = o * lax.rsqrt(jnp.mean(o * o, axis=-1, keepdims=True) + EPS) * gn
        o_ref[pl.ds(r0, c), cs] = (on * (gp * _sigmoid(gp))).astype(o_ref.dtype)

    def chunk_pair(ci, carry):
        for half in range(2):
            r0 = pl.multiple_of((2 * ci + half) * c, c)
            for hh in range(nh):
                head_chunk(r0, hh)
        return carry

    assert nchunk % 2 == 0
    lax.fori_loop(0, nchunk // 2, chunk_pair, 0)


def hgrn_core(proj, lb, g_norm, bsz, seq):
    m = proj.shape[0]
    width = proj.shape[1] // 4
    heads = width // HG_HEAD
    nh = HG_GROUP
    gw = nh * HG_HEAD
    ngrp = heads // nh
    tb = _tile(seq, 512)
    nsblk = seq // tb

    def spec(off):
        return pl.BlockSpec((tb, gw), lambda bb, h, s: (bb * nsblk + s, off * ngrp + h))

    return pl.pallas_call(
        functools.partial(_hgrn_kernel, nchunk=tb // HG_CHUNK, nh=nh),
        grid=(bsz, ngrp, nsblk),
        in_specs=[spec(0), spec(1), spec(2), spec(3),
                  pl.BlockSpec((1, gw), lambda bb, h, s: (0, h)),
                  pl.BlockSpec((1, HG_HEAD), lambda bb, h, s: (0, 0))],
        out_specs=pl.BlockSpec((tb, gw), lambda bb, h, s: (bb * nsblk + s, h)),
        out_shape=jax.ShapeDtypeStruct((m, width), _BF16),
        scratch_shapes=[pltpu.VMEM((nh, HG_HEAD, HG_HEAD), _F32)],
        compiler_params=_params("parallel", "parallel", "arbitrary"),
        name="hgrn_core",
    )(proj, proj, proj, proj, lb.reshape(1, width), g_norm.reshape(1, HG_HEAD))


def _gelu_tanh(x):
    c0 = math.sqrt(2.0 / math.pi)
    return 0.5 * x * (1.0 + jnp.tanh(c0 * (x + 0.044715 * (x * x * x))))


def _lru_kernel(gate_ref, u_ref, cw_ref, cb_ref, wa_ref, ba_ref, wx_ref, bx_ref, lam_ref,
                o_ref, ext_ref, sa_ref, sx_ref, h_ref, *, ts, pad):
    width = u_ref.shape[1]
    blk = width // LRU_BLOCKS
    halo = 8

    @pl.when(pl.program_id(1) == 0)
    def _():
        ext_ref[0:halo, :] = jnp.zeros((halo, width), _F32)
        h_ref[...] = jnp.zeros_like(h_ref)

    @pl.when(pl.program_id(1) > 0)
    def _():
        ext_ref[0:halo, :] = ext_ref[ts:ts + halo, :]

    ext_ref[halo:halo + ts, :] = u_ref[...]
    uc = cb_ref[...] + cw_ref[0:1, :] * ext_ref[pl.ds(halo - 3, ts), :]
    for j in range(1, CONV_W):
        uc = uc + cw_ref[j:j + 1, :] * ext_ref[pl.ds(halo - 3 + j, ts), :]

    u16 = uc.astype(_BF16)
    r_parts = []
    i_parts = []
    for n in range(LRU_BLOCKS):
        ub = u16[:, n * blk:(n + 1) * blk]
        r_parts.append(_dot(ub, wa_ref[n].astype(_BF16)))
        i_parts.append(_dot(ub, wx_ref[n].astype(_BF16)))
    r = _sigmoid(jnp.concatenate(r_parts, axis=1) + ba_ref[...])
    ig = _sigmoid(jnp.concatenate(i_parts, axis=1) + bx_ref[...])

    lam = lam_ref[...]
    softplus = jnp.maximum(-lam, 0.0) + jnp.log(1.0 + jnp.exp(-jnp.abs(lam)))
    log_a = -LRU_C * r * softplus
    a = jnp.exp(log_a)
    inp = jnp.sqrt(jnp.maximum(1.0 - a * a, 0.0)) * (ig * uc)

    sa_ref[0:pad, :] = jnp.ones((pad, width), _F32)
    sx_ref[0:pad, :] = jnp.zeros((pad, width), _F32)
    shift = 1
    while shift < ts:
        sa_ref[pad:pad + ts, :] = a
        sx_ref[pad:pad + ts, :] = inp
        a_sh = sa_ref[pl.ds(pad - shift, ts), :]
        x_sh = sx_ref[pl.ds(pad - shift, ts), :]
        inp = a * x_sh + inp
        a = a * a_sh
        shift *= 2

    h = inp + a * h_ref[0:1, :]
    h_ref[0:1, :] = h[ts - 1:ts, :]
    o_ref[...] = (_gelu_tanh(gate_ref[...]) * h).astype(o_ref.dtype)


def lru_core(proj, conv_w, conv_b, w_a, b_a, w_x, b_x, lam, bsz, seq):
    m = proj.shape[0]
    width = proj.shape[1] // 2
    blk = width // LRU_BLOCKS
    ts = _tile(seq, 256)
    pad = ts // 2 if ts >= 16 else 8
    pad = max(pad, 8)
    nsblk = seq // ts
    vec = lambda t: t.reshape(1, width)
    row_spec = pl.BlockSpec((1, width), lambda bb, s: (0, 0))
    w_spec = pl.BlockSpec((LRU_BLOCKS, blk, blk), lambda bb, s: (0, 0, 0))
    return pl.pallas_call(
        functools.partial(_lru_kernel, ts=ts, pad=pad),
        grid=(bsz, nsblk),
        in_specs=[pl.BlockSpec((ts, width), lambda bb, s: (bb * nsblk + s, 0)),
                  pl.BlockSpec((ts, width), lambda bb, s: (bb * nsblk + s, 1)),
                  pl.BlockSpec((CONV_W, width), lambda bb, s: (0, 0)),
                  row_spec, w_spec, row_spec, w_spec, row_spec, row_spec],
        out_specs=pl.BlockSpec((ts, width), lambda bb, s: (bb * nsblk + s, 0)),
        out_shape=jax.ShapeDtypeStruct((m, width), _BF16),
        scratch_shapes=[pltpu.VMEM((ts + 8, width), _F32),
                        pltpu.VMEM((ts + pad, width), _F32),
                        pltpu.VMEM((ts + pad, width), _F32),
                        pltpu.VMEM((8, width), _F32)],
        compiler_params=_params("parallel", "arbitrary"),
        name="lru_core",
    )(proj, proj, conv_w, vec(conv_b), w_a, vec(b_a), w_x, vec(b_x), vec(lam))


def _softcap(t):
    return GATE_CAP * jnp.tanh(t / GATE_CAP)


def _log_sigmoid(x):
    return jnp.minimum(x, 0.0) - jnp.log(1.0 + jnp.exp(-jnp.abs(x)))


def _mlstm_kernel(q_ref, k_ref, v_ref, og_ref, gt_ref, bif_ref, gn_ref, o_ref,
                  c_ref, n_ref, m_ref, *, nchunk, c, dqk, dv, nh):
    h_base = pl.program_id(1) * nh

    @pl.when(pl.program_id(2) == 0)
    def _():
        c_ref[...] = jnp.zeros_like(c_ref)
        n_ref[...] = jnp.zeros_like(n_ref)
        m_ref[...] = jnp.zeros_like(m_ref)

    row = lax.broadcasted_iota(jnp.int32, (c, c), 0)
    col = lax.broadcasted_iota(jnp.int32, (c, c), 1)
    causal = row >= col
    tril = causal.astype(_BF16)
    eye = (lax.broadcasted_iota(jnp.int32, (2 * ML_HEADS, LANE), 0)
           == lax.broadcasted_iota(jnp.int32, (2 * ML_HEADS, LANE), 1)).astype(_BF16)
    sub16 = lax.broadcasted_iota(jnp.int32, (2 * ML_HEADS, c), 0)
    lane1 = lax.broadcasted_iota(jnp.int32, (1, LANE), 1)
    gn = gn_ref[...]
    bif = bif_ref[...]
    kscale = dqk ** -0.5

    def head_chunk(r0, hh, gl, cum, gl_rows, cum_rows):
        h_id = h_base + hh
        qs = slice(hh * dqk, (hh + 1) * dqk)
        vs = slice(hh * dv, (hh + 1) * dv)
        q = q_ref[pl.ds(r0, c), qs]
        k = k_ref[pl.ds(r0, c), qs] * kscale
        v = v_ref[pl.ds(r0, c), vs]
        og = og_ref[pl.ds(r0, c), vs]
        b_col = jnp.sum(jnp.where(lane1 == h_id + ML_HEADS, cum, 0.0), axis=-1, keepdims=True)
        i_col = jnp.sum(jnp.where(lane1 == h_id, gl, 0.0), axis=-1, keepdims=True)
        b_row = jnp.sum(jnp.where(sub16 == h_id + ML_HEADS, cum_rows, 0.0), axis=0, keepdims=True)
        i_row = jnp.sum(jnp.where(sub16 == h_id, gl_rows, 0.0), axis=0, keepdims=True)

        m_st = m_ref[hh]
        c_st = c_ref[hh]
        n_st = n_ref[hh]

        d_mat = jnp.where(causal, b_col - b_row + i_row, NEG_BIG)
        inter = b_col + m_st
        m_t = jnp.maximum(inter, jnp.max(d_mat, axis=-1, keepdims=True))
        w_intra = jnp.where(causal, jnp.exp(jnp.minimum(d_mat - m_t, 0.0)), 0.0)
        w_inter = jnp.exp(inter - m_t)
        q16 = q.astype(_BF16)
        k16 = k.astype(_BF16)
        v16 = v.astype(_BF16)
        qk = _dot_nt(q16, k16) * w_intra
        num = _dot(qk.astype(_BF16), v16) + w_inter * _dot(q16, c_st.astype(_BF16))
        den = jnp.sum(qk, axis=-1, keepdims=True) + w_inter * jnp.sum(q * n_st, axis=-1, keepdims=True)
        hval = num / jnp.maximum(jnp.abs(den), jnp.exp(-m_t))

        g_tot = b_col[c - 1:c, :]
        upd_col = g_tot - b_col + i_col
        m_new = jnp.maximum(g_tot + m_st, jnp.max(upd_col, axis=0, keepdims=True))
        w_upd = jnp.exp(upd_col - m_new)
        decay = jnp.exp(g_tot + m_st - m_new)
        kw = k * w_upd
        c_ref[hh] = decay * c_st + _dot_tn(kw.astype(_BF16), v16)
        n_ref[hh] = decay * n_st + jnp.sum(kw, axis=0, keepdims=True)
        m_ref[hh] = m_new

        hn = hval * lax.rsqrt(jnp.mean(hval * hval, axis=-1, keepdims=True) + EPS) * gn
        o_ref[pl.ds(r0, c), vs] = (hn * _sigmoid(og)).astype(o_ref.dtype)

    def chunk(ci, carry):
        r0 = pl.multiple_of(ci * c, c)
        gt = jnp.where(lane1 < 2 * ML_HEADS, gt_ref[pl.ds(r0, c), :], 0.0)
        pre = _softcap(gt + bif)
        is_f = jnp.logical_and(lane1 >= ML_HEADS, lane1 < 2 * ML_HEADS)
        gl = jnp.where(is_f, _log_sigmoid(pre), pre)
        cum = _dot_exact_lhs(tril, gl)
        ch, cm, cl = _split3(cum)
        cum_rows = _dot_nt(eye, ch) + _dot_nt(eye, cm) + _dot_nt(eye, cl)
        gh, gm, gl3 = _split3(gl)
        gl_rows = _dot_nt(eye, gh) + _dot_nt(eye, gm) + _dot_nt(eye, gl3)
        for hh in range(nh):
            head_chunk(r0, hh, gl, cum, gl_rows, cum_rows)
        return carry

    lax.fori_loop(0, nchunk, chunk, 0)


def mlstm_core(proj, b_if, g_norm, bsz, seq, dqk, dv):
    m = proj.shape[0]
    heads = ML_HEADS
    hv = heads * dv
    c = _tile(seq, ML_CHUNK)
    tb = _tile(seq, 512)
    nsblk = seq // tb
    gate_blk = (2 * heads * dqk + 2 * hv) // LANE
    bif = jnp.zeros((1, LANE), _F32).at[0, :2 * heads].set(b_if.reshape(-1))
    nh = ML_GROUP
    ngrp = heads // nh
    qw = nh * dqk
    vw = nh * dv
    v_off = (2 * heads * dqk) // vw
    return pl.pallas_call(
        functools.partial(_mlstm_kernel, nchunk=tb // c, c=c, dqk=dqk, dv=dv, nh=nh),
        grid=(bsz, ngrp, nsblk),
        in_specs=[pl.BlockSpec((tb, qw), lambda bb, h, s: (bb * nsblk + s, h)),
                  pl.BlockSpec((tb, qw), lambda bb, h, s: (bb * nsblk + s, ngrp + h)),
                  pl.BlockSpec((tb, vw), lambda bb, h, s: (bb * nsblk + s, v_off + h)),
                  pl.BlockSpec((tb, vw), lambda bb, h, s: (bb * nsblk + s, v_off + ngrp + h)),
                  pl.BlockSpec((tb, LANE), lambda bb, h, s: (bb * nsblk + s, gate_blk)),
                  pl.BlockSpec((1, LANE), lambda bb, h, s: (0, 0)),
                  pl.BlockSpec((1, dv), lambda bb, h, s: (0, 0))],
        out_specs=pl.BlockSpec((tb, vw), lambda bb, h, s: (bb * nsblk + s, h)),
        out_shape=jax.ShapeDtypeStruct((m, hv), _BF16),
        scratch_shapes=[pltpu.VMEM((nh, dqk, dv), _F32),
                        pltpu.VMEM((nh, 1, dqk), _F32),
                        pltpu.VMEM((nh, 1, 1), _F32)],
        compiler_params=_params("parallel", "parallel", "arbitrary"),
        name="mlstm_core",
    )(proj, proj, proj, proj, proj, bif, g_norm.reshape(1, dv))


def _xattn_kernel(q_ref, k_ref, v_ref, o_ref, *, heads):
    d = q_ref.shape[1]
    hd = d // heads
    scale = hd ** -0.5
    for h in range(heads):
        qh = q_ref[:, h * hd:(h + 1) * hd]
        kh = k_ref[:, h * hd:(h + 1) * hd]
        vh = v_ref[:, h * hd:(h + 1) * hd]
        s = _dot_nt(qh, kh) * scale
        s = s - jnp.max(s, axis=-1, keepdims=True)
        e = jnp.exp(s)
        p = e / jnp.sum(e, axis=-1, keepdims=True)
        o_ref[:, h * hd:(h + 1) * hd] = _dot(p.astype(_BF16), vh).astype(o_ref.dtype)


def xattn_core(q, kv, bsz, seq, n_mem):
    m, d = q.shape
    tq = _tile(seq, 512)
    nq = seq // tq
    return pl.pallas_call(
        functools.partial(_xattn_kernel, heads=XA_HEADS),
        grid=(bsz, nq),
        in_specs=[pl.BlockSpec((tq, d), lambda bb, s: (bb * nq + s, 0)),
                  pl.BlockSpec((n_mem, d), lambda bb, s: (bb, 0)),
                  pl.BlockSpec((n_mem, d), lambda bb, s: (bb, 1))],
        out_specs=pl.BlockSpec((tq, d), lambda bb, s: (bb * nq + s, 0)),
        out_shape=jax.ShapeDtypeStruct((m, d), _BF16),
        compiler_params=_params("parallel", "parallel"),
        name="xattn_core",
    )(q, kv, kv)


def kernel(x, mem, mem_norm_g, norm_g, final_norm_g, ffn_w_gu, ffn_w_down, xa_w_q, xa_w_kv, xa_w_o,
           hg_lb_param, hg_w_in, hg_g_norm, hg_w_out,
           lru_w_in, lru_conv_w, lru_conv_b, lru_w_a, lru_b_a, lru_w_x, lru_b_x, lru_lambda, lru_w_out,
           ml_w_in, ml_b_if, ml_g_norm, ml_w_out):
    bsz, seq, d = x.shape
    n_mem = mem.shape[1]
    depth = norm_g.shape[0]
    m = bsz * seq
    ml_dqk = d // (2 * ML_HEADS)
    ml_dv = d // ML_HEADS

    xf = x.reshape(m, d)
    mem_n = rms_cast(mem.reshape(bsz * n_mem, d), mem_norm_g)
    lb_all = hgrn_lower_bounds(hg_lb_param)

    xn = rms_cast(xf, norm_g[0, 0])
    for layer in range(depth):
        kind = layer % N_MIXERS
        idx = layer // N_MIXERS

        h = mm_swiglu(xn, ffn_w_gu, (layer, 0))
        xf, xn = mm_down_norm(h, cast_weight(ffn_w_down, (layer, 0)), xf, norm_g[layer, 1], 0.5)

        if kind == 0:
            proj = mm_ws(xn, hg_w_in, (idx,), _F32, name="hg_in")
            y = hgrn_core(proj, lb_all[layer], hg_g_norm[idx], bsz, seq)
            w_out = hg_w_out
        elif kind == 1:
            proj = mm_ws(xn, lru_w_in, (idx,), _F32, name="lru_in")
            y = lru_core(proj, lru_conv_w[idx], lru_conv_b[idx], lru_w_a[idx], lru_b_a[idx],
                         lru_w_x[idx], lru_b_x[idx], lru_lambda[idx], bsz, seq)
            w_out = lru_w_out
        else:
            proj = mm_ws(xn, ml_w_in, (idx,), _F32, tn=896, name="ml_in")
            y = mlstm_core(proj, ml_b_if[idx], ml_g_norm[idx], bsz, seq, ml_dqk, ml_dv)
            w_out = ml_w_out
        xf, xn = mm_resid_norm(y, w_out, (idx,), xf, norm_g[layer, 2], 1.0, name="mixer_out")

        q = mm_ws(xn, xa_w_q, (layer,), _BF16, name="xa_q")
        kv = mm_ws(mem_n, xa_w_kv, (layer,), _BF16, name="xa_kv")
        o = xattn_core(q, kv, bsz, seq, n_mem)
        xf, xn = mm_resid_norm(o, xa_w_o, (layer,), xf, norm_g[layer, 3], 1.0, name="xa_out")

        h = mm_swiglu(xn, ffn_w_gu, (layer, 1))
        w16 = cast_weight(ffn_w_down, (layer, 1))
        if layer + 1 < depth:
            xf, xn = mm_down_norm(h, w16, xf, norm_g[layer + 1, 0], 0.5)
        else:
            _, out = mm_down_norm(h, w16, xf, final_norm_g, 0.5, emit_x=False, norm_dtype=_F32,
                                  name="ffn_down_final")
    return out.reshape(bsz, seq, d)
```

```python
import functools
import math

import jax
import jax.numpy as jnp
from jax import lax
from jax.experimental import pallas as pl
from jax.experimental.pallas import tpu as pltpu

EPS = 1e-6
N_MIXERS = 3
HG_HEAD = 128
HG_CHUNK = 64
HG_SUB = 16
HG_GROUP = 4
HG_SAFE_LOG2 = 100.0
F_FLOOR = 1e-12
LRU_BLOCKS = 8
CONV_W = 4
LRU_C = 8.0
ML_HEADS = 8
ML_CHUNK = 256
ML_GROUP = 8
GATE_CAP = 15.0
NEG_BIG = -1e30
XA_HEADS = 4
LANE = 128
FF_TILE = 512
SWIGLU_SPLIT = 4
EPILOGUE_SPLIT = 2
DOWN_ROWS = 512
DOWN_SPLIT = 4
VMEM_LIMIT = 56 * 1024 * 1024

_BF16 = jnp.bfloat16
_F32 = jnp.float32


def _params(*sem):
    return pltpu.CompilerParams(dimension_semantics=sem, vmem_limit_bytes=VMEM_LIMIT)


def _tile(dim, pref):
    t = min(dim, pref)
    while dim % t:
        t //= 2
    return t


def _sigmoid(x):
    return 1.0 / (1.0 + jnp.exp(-x))


def _dot(a, b):
    return jnp.dot(a, b, preferred_element_type=_F32)


def _dot_nt(a, b):
    return lax.dot_general(a, b, (((1,), (1,)), ((), ())), preferred_element_type=_F32)


def _dot_tn(a, b):
    return lax.dot_general(a, b, (((0,), (0,)), ((), ())), preferred_element_type=_F32)


def _split3(x):
    hi = x.astype(_BF16)
    r1 = x - hi.astype(_F32)
    mid = r1.astype(_BF16)
    lo = (r1 - mid.astype(_F32)).astype(_BF16)
    return hi, mid, lo


def _dot_exact_lhs(sel, x):
    hi, mid, lo = _split3(x)
    return _dot(sel, hi) + _dot(sel, mid) + _dot(sel, lo)


def _rms(x, g):
    return x * lax.rsqrt(jnp.mean(x * x, axis=-1, keepdims=True) + EPS) * g


def _rms_kernel(x_ref, g_ref, o_ref):
    o_ref[...] = _rms(x_ref[...], g_ref[...]).astype(o_ref.dtype)


def rms_cast(x, g, out_dtype=_BF16):
    m, d = x.shape
    tm = _tile(m, 512)
    return pl.pallas_call(
        _rms_kernel,
        grid=(m // tm,),
        in_specs=[pl.BlockSpec((tm, d), lambda i: (i, 0)),
                  pl.BlockSpec((1, d), lambda i: (0, 0))],
        out_specs=pl.BlockSpec((tm, d), lambda i: (i, 0)),
        out_shape=jax.ShapeDtypeStruct((m, d), out_dtype),
        compiler_params=_params("parallel"),
        name="rms_cast",
    )(x, g.reshape(1, d))


def _wspec(widx, block, index_map, **kw):
    lead = tuple(widx)
    return pl.BlockSpec((None,) * len(lead) + tuple(block), lambda *g: lead + tuple(index_map(*g)), **kw)


def _mm_ws_kernel(a_ref, w_ref, o_ref, wbf_ref):
    @pl.when(pl.program_id(1) == 0)
    def _():
        wbf_ref[...] = w_ref[...].astype(_BF16)

    o_ref[...] = _dot(a_ref[...], wbf_ref[...]).astype(o_ref.dtype)


def mm_ws(a, w, widx, out_dtype, tm=1024, tn=1024, name="mm_ws"):
    m, k = a.shape
    n = w.shape[-1]
    tm = _tile(m, tm)
    nj = -(-n // tn)
    return pl.pallas_call(
        _mm_ws_kernel,
        grid=(nj, m // tm),
        in_specs=[pl.BlockSpec((tm, k), lambda j, i: (i, 0)),
                  _wspec(widx, (k, tn), lambda j, i: (0, j))],
        out_specs=pl.BlockSpec((tm, tn), lambda j, i: (i, j)),
        out_shape=jax.ShapeDtypeStruct((m, nj * tn), out_dtype),
        scratch_shapes=[pltpu.VMEM((k, tn), _BF16)],
        compiler_params=_params("arbitrary", "arbitrary"),
        name=name,
    )(a, w)


def _swiglu_kernel(a_ref, wg_ref, wu_ref, o_ref, wbf_ref, *, tnh, rem):
    @pl.when(pl.program_id(1) == 0)
    def _():
        lead = (0,) * (len(wg_ref.shape) - 2)
        wbf_ref[:, :tnh] = wg_ref[lead].astype(_BF16)
        wbf_ref[:, tnh:] = wu_ref[lead].astype(_BF16)

    tm = a_ref.shape[0]
    sub = tm // SWIGLU_SPLIT
    last = pl.num_programs(0) - 1
    for r in range(SWIGLU_SPLIT):
        rows = slice(r * sub, (r + 1) * sub)
        acc = _dot(a_ref[rows, :], wbf_ref[...])
        gate = acc[:, :tnh]
        up = acc[:, tnh:]
        res = (gate * _sigmoid(gate) * up).astype(o_ref.dtype)
        if rem == tnh:
            o_ref[rows, :] = res
        else:
            o_ref[rows, :rem] = jnp.where(pl.program_id(0) == last, res[:, tnh - rem:], res[:, :rem])
            o_ref[rows, rem:] = res[:, rem:]


def mm_swiglu(a, w_gu, widx, tnh=FF_TILE):
    m, k = a.shape
    d_ff = w_gu.shape[-1] // 2
    tm = _tile(m, 1024)
    nj = -(-d_ff // tnh)

    lead = tuple(widx)
    lead_blk = (1,) * len(lead)

    assert d_ff % LANE == 0 and tnh % LANE == 0

    def start(j, base=0):
        return LANE * (base // LANE + jnp.minimum(j * (tnh // LANE), (d_ff - tnh) // LANE))

    def elem_spec(block, index_map):
        return pl.BlockSpec(tuple(pl.Element(b) for b in block), index_map)

    return pl.pallas_call(
        functools.partial(_swiglu_kernel, tnh=tnh, rem=d_ff - (nj - 1) * tnh),
        grid=(nj, m // tm),
        in_specs=[pl.BlockSpec((tm, k), lambda j, i: (i, 0)),
                  elem_spec(lead_blk + (k, tnh), lambda j, i: lead + (0, start(j))),
                  elem_spec(lead_blk + (k, tnh), lambda j, i: lead + (0, start(j, d_ff)))],
        out_specs=pl.BlockSpec((tm, tnh), lambda j, i: (i, j)),
        out_shape=jax.ShapeDtypeStruct((m, d_ff), _BF16),
        scratch_shapes=[pltpu.VMEM((k, 2 * tnh), _BF16)],
        compiler_params=_params("arbitrary", "arbitrary"),
        name="mm_swiglu",
    )(a, w_gu, w_gu)


def _cast_kernel(w_ref, o_ref):
    o_ref[...] = w_ref[...].astype(o_ref.dtype)


def cast_weight(w, widx, row_blocks=8):
    k, d = w.shape[-2:]
    tk = k // row_blocks
    assert tk * row_blocks == k and tk % 16 == 0
    return pl.pallas_call(
        _cast_kernel,
        grid=(row_blocks,),
        in_specs=[_wspec(widx, (tk, d), lambda i: (i, 0))],
        out_specs=pl.BlockSpec((tk, d), lambda i: (i, 0)),
        out_shape=jax.ShapeDtypeStruct((k, d), _BF16),
        compiler_params=_params("parallel"),
        name="cast_weight",
    )(w)


def _down_norm_kernel(a_ref, w_ref, r_ref, g_ref, *out_refs, scale, emit_x):
    n_ref = out_refs[-1]
    sub = a_ref.shape[0] // DOWN_SPLIT
    for r in range(DOWN_SPLIT):
        rows = slice(r * sub, (r + 1) * sub)
        x = r_ref[rows, :] + scale * _dot(a_ref[rows, :], w_ref[...])
        if emit_x:
            out_refs[0][rows, :] = x
        n_ref[rows, :] = _rms(x, g_ref[...]).astype(n_ref.dtype)


def mm_down_norm(a, w16, resid, g, scale, *, emit_x=True, norm_dtype=_BF16, name="ffn_down"):
    m, k = a.shape
    d = w16.shape[-1]
    tm = _tile(m, DOWN_ROWS)
    row_spec = pl.BlockSpec((tm, d), lambda i: (i, 0))
    out_shape = [jax.ShapeDtypeStruct((m, d), norm_dtype)]
    if emit_x:
        out_shape.insert(0, jax.ShapeDtypeStruct((m, d), _F32))
    res = pl.pallas_call(
        functools.partial(_down_norm_kernel, scale=scale, emit_x=emit_x),
        grid=(m // tm,),
        in_specs=[pl.BlockSpec((tm, k), lambda i: (i, 0)),
                  pl.BlockSpec((k, d), lambda i: (0, 0), pipeline_mode=pl.Buffered(1)),
                  row_spec,
                  pl.BlockSpec((1, d), lambda i: (0, 0))],
        out_specs=[row_spec] * len(out_shape),
        out_shape=out_shape,
        compiler_params=_params("parallel"),
        name=name,
    )(a, w16, resid, g.reshape(1, d))
    return (res[0], res[1]) if emit_x else (None, res[0])


def _resid_norm_kernel(a_ref, w_ref, r_ref, g_ref, x_ref, n_ref, wbf_ref, *, scale):
    @pl.when(pl.program_id(0) == 0)
    def _():
        wbf_ref[...] = w_ref[...].astype(_BF16)

    sub = a_ref.shape[0] // EPILOGUE_SPLIT
    for r in range(EPILOGUE_SPLIT):
        rows = slice(r * sub, (r + 1) * sub)
        x = r_ref[rows, :] + scale * _dot(a_ref[rows, :], wbf_ref[...])
        x_ref[rows, :] = x
        n_ref[rows, :] = _rms(x, g_ref[...]).astype(n_ref.dtype)


def mm_resid_norm(a, w, widx, resid, g, scale, name="mm_resid"):
    m, k = a.shape
    d = w.shape[-1]
    tm = _tile(m, 512)
    return pl.pallas_call(
        functools.partial(_resid_norm_kernel, scale=scale),
        grid=(m // tm,),
        in_specs=[pl.BlockSpec((tm, k), lambda i: (i, 0)),
                  _wspec(widx, (k, d), lambda i: (0, 0), pipeline_mode=pl.Buffered(1)),
                  pl.BlockSpec((tm, d), lambda i: (i, 0)),
                  pl.BlockSpec((1, d), lambda i: (0, 0))],
        out_specs=[pl.BlockSpec((tm, d), lambda i: (i, 0)),
                   pl.BlockSpec((tm, d), lambda i: (i, 0))],
        out_shape=[jax.ShapeDtypeStruct((m, d), _F32),
                   jax.ShapeDtypeStruct((m, d), _BF16)],
        scratch_shapes=[pltpu.VMEM((k, d), _BF16)],
        compiler_params=_params("arbitrary"),
        name=name,
    )(a, w, resid, g.reshape(1, d))


def _lb_kernel(p_ref, o_ref):
    p = p_ref[...]
    depth = p.shape[0]
    e = jnp.exp(p - jnp.max(p, axis=0, keepdims=True))
    sm = e / jnp.sum(e, axis=0, keepdims=True)
    run = jnp.zeros_like(sm[0:1])
    for layer in range(depth):
        run = run + sm[layer:layer + 1]
        o_ref[layer:layer + 1, :] = run - sm[0:1]


def hgrn_lower_bounds(p):
    return pl.pallas_call(
        _lb_kernel,
        out_shape=jax.ShapeDtypeStruct(p.shape, _F32),
        name="hgrn_lb",
    )(p)


def _hgrn_kernel(q_ref, f_ref, v_ref, g_ref, lb_ref, gn_ref, o_ref,
                 st_ref, sc_ref, qs_ref, ks_ref, dl_ref, *, nchunk, nh):
    c = HG_CHUNK
    sb = HG_SUB
    d = HG_HEAD
    ru = 8

    @pl.when(pl.program_id(2) == 0)
    def _():
        st_ref[...] = jnp.zeros_like(st_ref)

    gn = gn_ref[...]
    row = lax.broadcasted_iota(jnp.int32, (c, c), 0)
    col = lax.broadcasted_iota(jnp.int32, (c, c), 1)
    future = col > row
    lane_u = lax.broadcasted_iota(jnp.int32, (ru, c), 1)
    row3 = lax.broadcasted_iota(jnp.int32, (c, 3 * c), 0)
    col3 = lax.broadcasted_iota(jnp.int32, (c, 3 * c), 1)
    tril3 = (row3 >= (col3 & (c - 1))).astype(_BF16)

    def gates(r0, hh):
        cs = slice(hh * d, (hh + 1) * d)
        lb = lb_ref[:, cs]
        qp = q_ref[pl.ds(r0, c), cs]
        fp = f_ref[pl.ds(r0, c), cs]
        q = qp * _sigmoid(qp)
        e_abs = jnp.exp(-jnp.abs(fp))
        r_abs = 1.0 / (1.0 + e_abs)
        er_abs = e_abs * r_abs
        pos = fp >= 0.0
        f = lb + (1.0 - lb) * jnp.where(pos, r_abs, er_abs)
        k = (1.0 - lb) * jnp.where(pos, er_abs, r_abs)
        lf = jnp.log2(jnp.maximum(f, F_FLOOR))
        return q, k, lf

    def cumsum_rows(lf):
        return _dot(tril3, jnp.concatenate(_split3(lf), axis=0))

    def block_ref(b, lo):
        return b[lo - 1:lo, :] if lo > 0 else jnp.zeros((1, d), _F32)

    def prepare(r0, half):
        qkl = [gates(r0, hh) for hh in range(nh)]
        b_all = cumsum_rows(jnp.concatenate([lf for _, _, lf in qkl], axis=1))
        risks = []
        for hh, (q, k, _) in enumerate(qkl):
            slot = half * nh + hh
            b = b_all[:, hh * d:(hh + 1) * d]
            b_last = b[c - 1:c, :]
            risk = jnp.zeros((1, d), _F32)
            q_rows = []
            k_cols = []
            nblk = c // sb
            for bi in range(nblk):
                lo = bi * sb
                hi = lo + sb
                ref = block_ref(b, lo)
                risk = jnp.maximum(risk, ref - b[hi - 1:hi, :])
                qt = (q[lo:hi, :] * jnp.exp2(b[lo:hi, :] - ref)).astype(_BF16)
                kt = (k[:hi, :] * jnp.exp2(ref - b[:hi, :])).astype(_BF16)
                if hi < c:
                    kt = jnp.concatenate([kt, jnp.zeros((c - hi, d), _BF16)], axis=0)
                zero = jnp.zeros((sb, d), _BF16)
                q_rows.append(jnp.concatenate([zero] * bi + [qt] + [zero] * (nblk - 1 - bi), axis=1))
                k_cols.append(kt)
            scores = _dot_nt(jnp.concatenate(q_rows, axis=0), jnp.concatenate(k_cols, axis=1))
            sc_ref[slot] = jnp.where(future, 0.0, scores).astype(_BF16)
            qs_ref[slot] = (q * jnp.exp2(b)).astype(_BF16)
            ks_ref[slot] = (k * jnp.exp2(b_last - b)).astype(_BF16)
            dl_ref[slot] = jnp.broadcast_to(jnp.exp2(b_last), (8, d))
            risks.append(jnp.max(risk))
        return risks

    def direct_scores(r0, hh, slot):
        q, k, lf = gates(r0, hh)
        b = cumsum_rows(lf)
        bk = b - jnp.log2(k)
        units = []
        for bi in range(c // sb):
            lo = bi * sb
            if bi > 0:
                ref = block_ref(b, lo)
                qt = q[lo:lo + sb, :] * jnp.exp2(b[lo:lo + sb, :] - ref)
                kt = k[:lo, :] * jnp.exp2(ref - b[:lo, :])
                kt = jnp.concatenate([kt, jnp.zeros((c - lo, d), _F32)], axis=0)
                off = _dot_nt(qt.astype(_BF16), kt.astype(_BF16))
            for uu in range(sb // ru):
                t0 = lo + uu * ru
                b_u = b[t0:t0 + ru, :]
                q_u = q[t0:t0 + ru, :]
                sc = off[uu * ru:(uu + 1) * ru, :] if bi > 0 else jnp.zeros((ru, c), _F32)
                for s in range(lo, t0 + ru):
                    w = q_u * jnp.exp2(b_u - bk[s:s + 1, :])
                    sc = jnp.where(lane_u == s, jnp.sum(w, axis=-1, keepdims=True), sc)
                units.append(sc)
        sc_ref[slot] = jnp.where(future, 0.0, jnp.concatenate(units, axis=0)).astype(_BF16)

    def finish(r0, hh, slot):
        cs = slice(hh * d, (hh + 1) * d)
        v16 = v_ref[pl.ds(r0, c), cs].astype(_BF16)
        gp = g_ref[pl.ds(r0, c), cs]
        st = st_ref[hh]
        o = _dot(sc_ref[slot], v16) + _dot_nt(qs_ref[slot], st.astype(_BF16))
        st_ref[hh] = st * dl_ref[slot, 0:1, :] + _dot_tn(v16, ks_ref[slot])
        on = o * lax.rsqrt(jnp.mean(o * o, axis=-1, keepdims=True) + EPS) * gn
        o_ref[pl.ds(r0, c), cs] = (on * (gp * _sigmoid(gp))).astype(o_ref.dtype)

    def chunk_pair(ci, carry):
        starts = [pl.multiple_of((2 * ci + half) * c, c) for half in range(2)]
        work = [(starts[half], hh, half * nh + hh) for half in range(2) for hh in range(nh)]
        risks = prepare(starts[0], 0) + prepare(starts[1], 1)
        worst = functools.reduce(jnp.maximum, risks)

        @pl.when(worst > HG_SAFE_LOG2)
        def _():
            for (r0, hh, slot), risk in zip(work, risks):
                @pl.when(risk > HG_SAFE_LOG2)
                def _():
                    direct_scores(r0, hh, slot)

        for r0, hh, slot in work:
            finish(r0, hh, slot)
        return carry

    assert nchunk % 2 == 0
    lax.fori_loop(0, nchunk // 2, chunk_pair, 0)


def hgrn_core(proj, lb, g_norm, bsz, seq):
    m = proj.shape[0]
    width = proj.shape[1] // 4
    heads = width // HG_HEAD
    nh = HG_GROUP
    gw = nh * HG_HEAD
    ngrp = heads // nh
    tb = _tile(seq, 512)
    nsblk = seq // tb

    def spec(off):
        return pl.BlockSpec((tb, gw), lambda bb, h, s: (bb * nsblk + s, off * ngrp + h))

    return pl.pallas_call(
        functools.partial(_hgrn_kernel, nchunk=tb // HG_CHUNK, nh=nh),
        grid=(bsz, ngrp, nsblk),
        in_specs=[spec(0), spec(1), spec(2), spec(3),
                  pl.BlockSpec((1, gw), lambda bb, h, s: (0, h)),
                  pl.BlockSpec((1, HG_HEAD), lambda bb, h, s: (0, 0))],
        out_specs=pl.BlockSpec((tb, gw), lambda bb, h, s: (bb * nsblk + s, h)),
        out_shape=jax.ShapeDtypeStruct((m, width), _BF16),
        scratch_shapes=[pltpu.VMEM((nh, HG_HEAD, HG_HEAD), _F32),
                        pltpu.VMEM((2 * nh, HG_CHUNK, HG_CHUNK), _BF16),
                        pltpu.VMEM((2 * nh, HG_CHUNK, HG_HEAD), _BF16),
                        pltpu.VMEM((2 * nh, HG_CHUNK, HG_HEAD), _BF16),
                        pltpu.VMEM((2 * nh, 8, HG_HEAD), _F32)],
        compiler_params=_params("parallel", "parallel", "arbitrary"),
        name="hgrn_core",
    )(proj, proj, proj, proj, lb.reshape(1, width), g_norm.reshape(1, HG_HEAD))


def _gelu_tanh(x):
    c0 = math.sqrt(2.0 / math.pi)
    return 0.5 * x * (1.0 + jnp.tanh(c0 * (x + 0.044715 * (x * x * x))))


def _lru_kernel(gate_ref, u_ref, cw_ref, cb_ref, wa_ref, ba_ref, wx_ref, bx_ref, lam_ref,
                o_ref, ext_ref, sa_ref, sx_ref, h_ref, *, ts, pad):
    width = u_ref.shape[1]
    blk = width // LRU_BLOCKS
    halo = 8

    @pl.when(pl.program_id(1) == 0)
    def _():
        ext_ref[0:halo, :] = jnp.zeros((halo, width), _F32)
        h_ref[...] = jnp.zeros_like(h_ref)

    @pl.when(pl.program_id(1) > 0)
    def _():
        ext_ref[0:halo, :] = ext_ref[ts:ts + halo, :]

    ext_ref[halo:halo + ts, :] = u_ref[...]
    uc = cb_ref[...] + cw_ref[0:1, :] * ext_ref[pl.ds(halo - 3, ts), :]
    for j in range(1, CONV_W):
        uc = uc + cw_ref[j:j + 1, :] * ext_ref[pl.ds(halo - 3 + j, ts), :]

    u16 = uc.astype(_BF16)
    r_parts = []
    i_parts = []
    for n in range(LRU_BLOCKS):
        ub = u16[:, n * blk:(n + 1) * blk]
        r_parts.append(_dot(ub, wa_ref[n].astype(_BF16)))
        i_parts.append(_dot(ub, wx_ref[n].astype(_BF16)))
    r = _sigmoid(jnp.concatenate(r_parts, axis=1) + ba_ref[...])
    ig = _sigmoid(jnp.concatenate(i_parts, axis=1) + bx_ref[...])

    lam = lam_ref[...]
    softplus = jnp.maximum(-lam, 0.0) + jnp.log(1.0 + jnp.exp(-jnp.abs(lam)))
    log_a = -LRU_C * r * softplus
    a = jnp.exp(log_a)
    inp = jnp.sqrt(jnp.maximum(1.0 - a * a, 0.0)) * (ig * uc)

    sa_ref[0:pad, :] = jnp.ones((pad, width), _F32)
    sx_ref[0:pad, :] = jnp.zeros((pad, width), _F32)
    shift = 1
    while shift < ts:
        sa_ref[pad:pad + ts, :] = a
        sx_ref[pad:pad + ts, :] = inp
        a_sh = sa_ref[pl.ds(pad - shift, ts), :]
        x_sh = sx_ref[pl.ds(pad - shift, ts), :]
        inp = a * x_sh + inp
        a = a * a_sh
        shift *= 2

    h = inp + a * h_ref[0:1, :]
    h_ref[0:1, :] = h[ts - 1:ts, :]
    o_ref[...] = (_gelu_tanh(gate_ref[...]) * h).astype(o_ref.dtype)


def lru_core(proj, conv_w, conv_b, w_a, b_a, w_x, b_x, lam, bsz, seq):
    m = proj.shape[0]
    width = proj.shape[1] // 2
    blk = width // LRU_BLOCKS
    ts = _tile(seq, 256)
    pad = ts // 2 if ts >= 16 else 8
    pad = max(pad, 8)
    nsblk = seq // ts
    vec = lambda t: t.reshape(1, width)
    row_spec = pl.BlockSpec((1, width), lambda bb, s: (0, 0))
    w_spec = pl.BlockSpec((LRU_BLOCKS, blk, blk), lambda bb, s: (0, 0, 0))
    return pl.pallas_call(
        functools.partial(_lru_kernel, ts=ts, pad=pad),
        grid=(bsz, nsblk),
        in_specs=[pl.BlockSpec((ts, width), lambda bb, s: (bb * nsblk + s, 0)),
                  pl.BlockSpec((ts, width), lambda bb, s: (bb * nsblk + s, 1)),
                  pl.BlockSpec((CONV_W, width), lambda bb, s: (0, 0)),
                  row_spec, w_spec, row_spec, w_spec, row_spec, row_spec],
        out_specs=pl.BlockSpec((ts, width), lambda bb, s: (bb * nsblk + s, 0)),
        out_shape=jax.ShapeDtypeStruct((m, width), _BF16),
        scratch_shapes=[pltpu.VMEM((ts + 8, width), _F32),
                        pltpu.VMEM((ts + pad, width), _F32),
                        pltpu.VMEM((ts + pad, width), _F32),
                        pltpu.VMEM((8, width), _F32)],
        compiler_params=_params("parallel", "arbitrary"),
        name="lru_core",
    )(proj, proj, conv_w, vec(conv_b), w_a, vec(b_a), w_x, vec(b_x), vec(lam))


def _softcap(t):
    return GATE_CAP * jnp.tanh(t / GATE_CAP)


def _log_sigmoid(x):
    return jnp.minimum(x, 0.0) - jnp.log(1.0 + jnp.exp(-jnp.abs(x)))


def _mlstm_kernel(q_ref, k_ref, v_ref, og_ref, gt_ref, bif_ref, gn_ref, o_ref,
                  c_ref, n_ref, m_ref, *, nchunk, c, dqk, dv, nh):
    h_base = pl.program_id(1) * nh

    @pl.when(pl.program_id(2) == 0)
    def _():
        c_ref[...] = jnp.zeros_like(c_ref)
        n_ref[...] = jnp.zeros_like(n_ref)
        m_ref[...] = jnp.zeros_like(m_ref)

    row = lax.broadcasted_iota(jnp.int32, (c, c), 0)
    col = lax.broadcasted_iota(jnp.int32, (c, c), 1)
    causal = row >= col
    tril = causal.astype(_BF16)
    eye = (lax.broadcasted_iota(jnp.int32, (2 * ML_HEADS, LANE), 0)
           == lax.broadcasted_iota(jnp.int32, (2 * ML_HEADS, LANE), 1)).astype(_BF16)
    sub16 = lax.broadcasted_iota(jnp.int32, (2 * ML_HEADS, c), 0)
    lane1 = lax.broadcasted_iota(jnp.int32, (1, LANE), 1)
    gn = gn_ref[...]
    bif = bif_ref[...]
    kscale = dqk ** -0.5

    def head_chunk(r0, hh, gl, cum, gl_rows, cum_rows):
        h_id = h_base + hh
        qs = slice(hh * dqk, (hh + 1) * dqk)
        vs = slice(hh * dv, (hh + 1) * dv)
        q = q_ref[pl.ds(r0, c), qs]
        k = k_ref[pl.ds(r0, c), qs] * kscale
        v = v_ref[pl.ds(r0, c), vs]
        og = og_ref[pl.ds(r0, c), vs]
        b_col = jnp.sum(jnp.where(lane1 == h_id + ML_HEADS, cum, 0.0), axis=-1, keepdims=True)
        i_col = jnp.sum(jnp.where(lane1 == h_id, gl, 0.0), axis=-1, keepdims=True)
        b_row = jnp.sum(jnp.where(sub16 == h_id + ML_HEADS, cum_rows, 0.0), axis=0, keepdims=True)
        i_row = jnp.sum(jnp.where(sub16 == h_id, gl_rows, 0.0), axis=0, keepdims=True)

        m_st = m_ref[hh]
        c_st = c_ref[hh]
        n_st = n_ref[hh]

        d_mat = jnp.where(causal, b_col - b_row + i_row, NEG_BIG)
        inter = b_col + m_st
        m_t = jnp.maximum(inter, jnp.max(d_mat, axis=-1, keepdims=True))
        w_intra = jnp.where(causal, jnp.exp(jnp.minimum(d_mat - m_t, 0.0)), 0.0)
        w_inter = jnp.exp(inter - m_t)
        q16 = q.astype(_BF16)
        k16 = k.astype(_BF16)
        v16 = v.astype(_BF16)
        qk = _dot_nt(q16, k16) * w_intra
        num = _dot(qk.astype(_BF16), v16) + w_inter * _dot(q16, c_st.astype(_BF16))
        den = jnp.sum(qk, axis=-1, keepdims=True) + w_inter * jnp.sum(q * n_st, axis=-1, keepdims=True)
        hval = num / jnp.maximum(jnp.abs(den), jnp.exp(-m_t))

        g_tot = b_col[c - 1:c, :]
        upd_col = g_tot - b_col + i_col
        m_new = jnp.maximum(g_tot + m_st, jnp.max(upd_col, axis=0, keepdims=True))
        w_upd = jnp.exp(upd_col - m_new)
        decay = jnp.exp(g_tot + m_st - m_new)
        kw = k * w_upd
        c_ref[hh] = decay * c_st + _dot_tn(kw.astype(_BF16), v16)
        n_ref[hh] = decay * n_st + jnp.sum(kw, axis=0, keepdims=True)
        m_ref[hh] = m_new

        hn = hval * lax.rsqrt(jnp.mean(hval * hval, axis=-1, keepdims=True) + EPS) * gn
        o_ref[pl.ds(r0, c), vs] = (hn * _sigmoid(og)).astype(o_ref.dtype)

    def chunk(ci, carry):
        r0 = pl.multiple_of(ci * c, c)
        gt = jnp.where(lane1 < 2 * ML_HEADS, gt_ref[pl.ds(r0, c), :], 0.0)
        pre = _softcap(gt + bif)
        is_f = jnp.logical_and(lane1 >= ML_HEADS, lane1 < 2 * ML_HEADS)
        gl = jnp.where(is_f, _log_sigmoid(pre), pre)
        cum = _dot_exact_lhs(tril, gl)
        ch, cm, cl = _split3(cum)
        cum_rows = _dot_nt(eye, ch) + _dot_nt(eye, cm) + _dot_nt(eye, cl)
        gh, gm, gl3 = _split3(gl)
        gl_rows = _dot_nt(eye, gh) + _dot_nt(eye, gm) + _dot_nt(eye, gl3)
        for hh in range(nh):
            head_chunk(r0, hh, gl, cum, gl_rows, cum_rows)
        return carry

    lax.fori_loop(0, nchunk, chunk, 0)


def mlstm_core(proj, b_if, g_norm, bsz, seq, dqk, dv):
    m = proj.shape[0]
    heads = ML_HEADS
    hv = heads * dv
    c = _tile(seq, ML_CHUNK)
    tb = _tile(seq, 512)
    nsblk = seq // tb
    gate_blk = (2 * heads * dqk + 2 * hv) // LANE
    bif = jnp.zeros((1, LANE), _F32).at[0, :2 * heads].set(b_if.reshape(-1))
    nh = ML_GROUP
    ngrp = heads // nh
    qw = nh * dqk
    vw = nh * dv
    v_off = (2 * heads * dqk) // vw
    return pl.pallas_call(
        functools.partial(_mlstm_kernel, nchunk=tb // c, c=c, dqk=dqk, dv=dv, nh=nh),
        grid=(bsz, ngrp, nsblk),
        in_specs=[pl.BlockSpec((tb, qw), lambda bb, h, s: (bb * nsblk + s, h)),
                  pl.BlockSpec((tb, qw), lambda bb, h, s: (bb * nsblk + s, ngrp + h)),
                  pl.BlockSpec((tb, vw), lambda bb, h, s: (bb * nsblk + s, v_off + h)),
                  pl.BlockSpec((tb, vw), lambda bb, h, s: (bb * nsblk + s, v_off + ngrp + h)),
                  pl.BlockSpec((tb, LANE), lambda bb, h, s: (bb * nsblk + s, gate_blk)),
                  pl.BlockSpec((1, LANE), lambda bb, h, s: (0, 0)),
                  pl.BlockSpec((1, dv), lambda bb, h, s: (0, 0))],
        out_specs=pl.BlockSpec((tb, vw), lambda bb, h, s: (bb * nsblk + s, h)),
        out_shape=jax.ShapeDtypeStruct((m, hv), _BF16),
        scratch_shapes=[pltpu.VMEM((nh, dqk, dv), _F32),
                        pltpu.VMEM((nh, 1, dqk), _F32),
                        pltpu.VMEM((nh, 1, 1), _F32)],
        compiler_params=_params("parallel", "parallel", "arbitrary"),
        name="mlstm_core",
    )(proj, proj, proj, proj, proj, bif, g_norm.reshape(1, dv))


def _xattn_kernel(q_ref, k_ref, v_ref, o_ref, *, heads):
    d = q_ref.shape[1]
    hd = d // heads
    scale = hd ** -0.5
    for h in range(heads):
        qh = q_ref[:, h * hd:(h + 1) * hd]
        kh = k_ref[:, h * hd:(h + 1) * hd]
        vh = v_ref[:, h * hd:(h + 1) * hd]
        s = _dot_nt(qh, kh) * scale
        s = s - jnp.max(s, axis=-1, keepdims=True)
        e = jnp.exp(s)
        p = e / jnp.sum(e, axis=-1, keepdims=True)
        o_ref[:, h * hd:(h + 1) * hd] = _dot(p.astype(_BF16), vh).astype(o_ref.dtype)


def xattn_core(q, kv, bsz, seq, n_mem):
    m, d = q.shape
    tq = _tile(seq, 512)
    nq = seq // tq
    return pl.pallas_call(
        functools.partial(_xattn_kernel, heads=XA_HEADS),
        grid=(bsz, nq),
        in_specs=[pl.BlockSpec((tq, d), lambda bb, s: (bb * nq + s, 0)),
                  pl.BlockSpec((n_mem, d), lambda bb, s: (bb, 0)),
                  pl.BlockSpec((n_mem, d), lambda bb, s: (bb, 1))],
        out_specs=pl.BlockSpec((tq, d), lambda bb, s: (bb * nq + s, 0)),
        out_shape=jax.ShapeDtypeStruct((m, d), _BF16),
        compiler_params=_params("parallel", "parallel"),
        name="xattn_core",
    )(q, kv, kv)


def kernel(x, mem, mem_norm_g, norm_g, final_norm_g, ffn_w_gu, ffn_w_down, xa_w_q, xa_w_kv, xa_w_o,
           hg_lb_param, hg_w_in, hg_g_norm, hg_w_out,
           lru_w_in, lru_conv_w, lru_conv_b, lru_w_a, lru_b_a, lru_w_x, lru_b_x, lru_lambda, lru_w_out,
           ml_w_in, ml_b_if, ml_g_norm, ml_w_out):
    bsz, seq, d = x.shape
    n_mem = mem.shape[1]
    depth = norm_g.shape[0]
    m = bsz * seq
    ml_dqk = d // (2 * ML_HEADS)
    ml_dv = d // ML_HEADS

    xf = x.reshape(m, d)
    mem_n = rms_cast(mem.reshape(bsz * n_mem, d), mem_norm_g)
    lb_all = hgrn_lower_bounds(hg_lb_param)

    xn = rms_cast(xf, norm_g[0, 0])
    for layer in range(depth):
        kind = layer % N_MIXERS
        idx = layer // N_MIXERS

        h = mm_swiglu(xn, ffn_w_gu, (layer, 0))
        xf, xn = mm_down_norm(h, cast_weight(ffn_w_down, (layer, 0)), xf, norm_g[layer, 1], 0.5)

        if kind == 0:
            proj = mm_ws(xn, hg_w_in, (idx,), _F32, name="hg_in")
            y = hgrn_core(proj, lb_all[layer], hg_g_norm[idx], bsz, seq)
            w_out = hg_w_out
        elif kind == 1:
            proj = mm_ws(xn, lru_w_in, (idx,), _F32, name="lru_in")
            y = lru_core(proj, lru_conv_w[idx], lru_conv_b[idx], lru_w_a[idx], lru_b_a[idx],
                         lru_w_x[idx], lru_b_x[idx], lru_lambda[idx], bsz, seq)
            w_out = lru_w_out
        else:
            proj = mm_ws(xn, ml_w_in, (idx,), _F32, tn=896, name="ml_in")
            y = mlstm_core(proj, ml_b_if[idx], ml_g_norm[idx], bsz, seq, ml_dqk, ml_dv)
            w_out = ml_w_out
        xf, xn = mm_resid_norm(y, w_out, (idx,), xf, norm_g[layer, 2], 1.0, name="mixer_out")

        q = mm_ws(xn, xa_w_q, (layer,), _BF16, name="xa_q")
        kv = mm_ws(mem_n, xa_w_kv, (layer,), _BF16, name="xa_kv")
        o = xattn_core(q, kv, bsz, seq, n_mem)
        xf, xn = mm_resid_norm(o, xa_w_o, (layer,), xf, norm_g[layer, 3], 1.0, name="xa_out")

        h = mm_swiglu(xn, ffn_w_gu, (layer, 1))
        w16 = cast_weight(ffn_w_down, (layer, 1))
        if layer + 1 < depth:
            xf, xn = mm_down_norm(h, w16, xf, norm_g[layer + 1, 0], 0.5)
        else:
            _, out = mm_down_norm(h, w16, xf, final_norm_g, 0.5, emit_x=False, norm_dtype=_F32,
                                  name="ffn_down_final")
    return out.reshape(bsz, seq, d)
```

```python
import functools
import math

import jax
import jax.numpy as jnp
from jax import lax
from jax.experimental import pallas as pl
from jax.experimental.pallas import tpu as pltpu

EPS = 1e-6
N_MIXERS = 3
HG_HEAD = 128
HG_CHUNK = 64
HG_SUB = 16
HG_GROUP = 4
HG_SAFE_LOG2 = 100.0
F_FLOOR = 1e-12
LRU_BLOCKS = 8
CONV_W = 4
LRU_C = 8.0
ML_HEADS = 8
ML_CHUNK = 256
ML_GROUP = 8
GATE_CAP = 15.0
NEG_BIG = -1e30
XA_HEADS = 4
LANE = 128
FF_TILE = 512
SWIGLU_SPLIT = 4
EPILOGUE_SPLIT = 2
DOWN_ROWS = 512
DOWN_SPLIT = 4
VMEM_LIMIT = 56 * 1024 * 1024

_BF16 = jnp.bfloat16
_F32 = jnp.float32


def _params(*sem):
    return pltpu.CompilerParams(dimension_semantics=sem, vmem_limit_bytes=VMEM_LIMIT)


def _tile(dim, pref):
    t = min(dim, pref)
    while dim % t:
        t //= 2
    return t


def _sigmoid(x):
    return 1.0 / (1.0 + jnp.exp(-x))


def _dot(a, b):
    return jnp.dot(a, b, preferred_element_type=_F32)


def _dot_nt(a, b):
    return lax.dot_general(a, b, (((1,), (1,)), ((), ())), preferred_element_type=_F32)


def _dot_tn(a, b):
    return lax.dot_general(a, b, (((0,), (0,)), ((), ())), preferred_element_type=_F32)


def _split3(x):
    hi = x.astype(_BF16)
    r1 = x - hi.astype(_F32)
    mid = r1.astype(_BF16)
    lo = (r1 - mid.astype(_F32)).astype(_BF16)
    return hi, mid, lo


def _dot_exact_lhs(sel, x):
    hi, mid, lo = _split3(x)
    return _dot(sel, hi) + _dot(sel, mid) + _dot(sel, lo)


def _rms(x, g):
    return x * lax.rsqrt(jnp.mean(x * x, axis=-1, keepdims=True) + EPS) * g


def _rms_kernel(x_ref, g_ref, o_ref):
    o_ref[...] = _rms(x_ref[...], g_ref[...]).astype(o_ref.dtype)


def rms_cast(x, g, out_dtype=_BF16):
    m, d = x.shape
    tm = _tile(m, 512)
    return pl.pallas_call(
        _rms_kernel,
        grid=(m // tm,),
        in_specs=[pl.BlockSpec((tm, d), lambda i: (i, 0)),
                  pl.BlockSpec((1, d), lambda i: (0, 0))],
        out_specs=pl.BlockSpec((tm, d), lambda i: (i, 0)),
        out_shape=jax.ShapeDtypeStruct((m, d), out_dtype),
        compiler_params=_params("parallel"),
        name="rms_cast",
    )(x, g.reshape(1, d))


def _wspec(widx, block, index_map, **kw):
    lead = tuple(widx)
    return pl.BlockSpec((None,) * len(lead) + tuple(block), lambda *g: lead + tuple(index_map(*g)), **kw)


def _mm_ws_kernel(a_ref, w_ref, o_ref, wbf_ref):
    @pl.when(pl.program_id(1) == 0)
    def _():
        wbf_ref[...] = w_ref[...].astype(_BF16)

    o_ref[...] = _dot(a_ref[...], wbf_ref[...]).astype(o_ref.dtype)


def mm_ws(a, w, widx, out_dtype, tm=1024, tn=1024, name="mm_ws"):
    m, k = a.shape
    n = w.shape[-1]
    tm = _tile(m, tm)
    nj = -(-n // tn)
    return pl.pallas_call(
        _mm_ws_kernel,
        grid=(nj, m // tm),
        in_specs=[pl.BlockSpec((tm, k), lambda j, i: (i, 0)),
                  _wspec(widx, (k, tn), lambda j, i: (0, j))],
        out_specs=pl.BlockSpec((tm, tn), lambda j, i: (i, j)),
        out_shape=jax.ShapeDtypeStruct((m, nj * tn), out_dtype),
        scratch_shapes=[pltpu.VMEM((k, tn), _BF16)],
        compiler_params=_params("arbitrary", "arbitrary"),
        name=name,
    )(a, w)


def _swiglu_kernel(a_ref, wg_ref, wu_ref, o_ref, wbf_ref, *, tnh, rem):
    lead = (0,) * (len(wg_ref.shape) - 2)
    first_row_block = pl.program_id(1) == 0
    is_last = pl.program_id(0) == pl.num_programs(0) - 1
    sub = a_ref.shape[0] // SWIGLU_SPLIT

    def compute(width):
        for r in range(SWIGLU_SPLIT):
            rows = slice(r * sub, (r + 1) * sub)
            acc = _dot(a_ref[rows, :], wbf_ref[:, :2 * width])
            gate = acc[:, :width]
            up = acc[:, width:]
            o_ref[rows, :width] = (gate * _sigmoid(gate) * up).astype(o_ref.dtype)

    if rem == tnh:
        @pl.when(first_row_block)
        def _():
            wbf_ref[:, :tnh] = wg_ref[lead].astype(_BF16)
            wbf_ref[:, tnh:] = wu_ref[lead].astype(_BF16)

        compute(tnh)
        return

    @pl.when(jnp.logical_and(first_row_block, jnp.logical_not(is_last)))
    def _():
        wbf_ref[:, :tnh] = wg_ref[lead].astype(_BF16)
        wbf_ref[:, tnh:] = wu_ref[lead].astype(_BF16)

    @pl.when(jnp.logical_and(first_row_block, is_last))
    def _():
        wbf_ref[:, :rem] = wg_ref[lead][:, tnh - rem:].astype(_BF16)
        wbf_ref[:, rem:2 * rem] = wu_ref[lead][:, tnh - rem:].astype(_BF16)

    @pl.when(jnp.logical_not(is_last))
    def _():
        compute(tnh)

    @pl.when(is_last)
    def _():
        compute(rem)


def mm_swiglu(a, w_gu, widx, tnh=FF_TILE):
    m, k = a.shape
    d_ff = w_gu.shape[-1] // 2
    tm = _tile(m, 1024)
    nj = -(-d_ff // tnh)

    lead = tuple(widx)
    lead_blk = (1,) * len(lead)

    assert d_ff % LANE == 0 and tnh % LANE == 0

    def start(j, base=0):
        return LANE * (base // LANE + jnp.minimum(j * (tnh // LANE), (d_ff - tnh) // LANE))

    def elem_spec(block, index_map):
        return pl.BlockSpec(tuple(pl.Element(b) for b in block), index_map)

    return pl.pallas_call(
        functools.partial(_swiglu_kernel, tnh=tnh, rem=d_ff - (nj - 1) * tnh),
        grid=(nj, m // tm),
        in_specs=[pl.BlockSpec((tm, k), lambda j, i: (i, 0)),
                  elem_spec(lead_blk + (k, tnh), lambda j, i: lead + (0, start(j))),
                  elem_spec(lead_blk + (k, tnh), lambda j, i: lead + (0, start(j, d_ff)))],
        out_specs=pl.BlockSpec((tm, tnh), lambda j, i: (i, j)),
        out_shape=jax.ShapeDtypeStruct((m, d_ff), _BF16),
        scratch_shapes=[pltpu.VMEM((k, 2 * tnh), _BF16)],
        compiler_params=_params("arbitrary", "arbitrary"),
        name="mm_swiglu",
    )(a, w_gu, w_gu)


def _cast_kernel(w_ref, o_ref):
    o_ref[...] = w_ref[...].astype(o_ref.dtype)


def cast_weight(w, widx, row_blocks=8):
    k, d = w.shape[-2:]
    tk = k // row_blocks
    assert tk * row_blocks == k and tk % 16 == 0
    return pl.pallas_call(
        _cast_kernel,
        grid=(row_blocks,),
        in_specs=[_wspec(widx, (tk, d), lambda i: (i, 0))],
        out_specs=pl.BlockSpec((tk, d), lambda i: (i, 0)),
        out_shape=jax.ShapeDtypeStruct((k, d), _BF16),
        compiler_params=_params("parallel"),
        name="cast_weight",
    )(w)


def _down_norm_kernel(a_ref, w_ref, r_ref, g_ref, *out_refs, scale, emit_x):
    n_ref = out_refs[-1]
    sub = a_ref.shape[0] // DOWN_SPLIT
    for r in range(DOWN_SPLIT):
        rows = slice(r * sub, (r + 1) * sub)
        x = r_ref[rows, :] + scale * _dot(a_ref[rows, :], w_ref[...])
        if emit_x:
            out_refs[0][rows, :] = x
        n_ref[rows, :] = _rms(x, g_ref[...]).astype(n_ref.dtype)


def mm_down_norm(a, w16, resid, g, scale, *, emit_x=True, norm_dtype=_BF16, name="ffn_down"):
    m, k = a.shape
    d = w16.shape[-1]
    tm = _tile(m, DOWN_ROWS)
    row_spec = pl.BlockSpec((tm, d), lambda i: (i, 0))
    out_shape = [jax.ShapeDtypeStruct((m, d), norm_dtype)]
    if emit_x:
        out_shape.insert(0, jax.ShapeDtypeStruct((m, d), _F32))
    res = pl.pallas_call(
        functools.partial(_down_norm_kernel, scale=scale, emit_x=emit_x),
        grid=(m // tm,),
        in_specs=[pl.BlockSpec((tm, k), lambda i: (i, 0)),
                  pl.BlockSpec((k, d), lambda i: (0, 0), pipeline_mode=pl.Buffered(1)),
                  row_spec,
                  pl.BlockSpec((1, d), lambda i: (0, 0))],
        out_specs=[row_spec] * len(out_shape),
        out_shape=out_shape,
        compiler_params=_params("parallel"),
        name=name,
    )(a, w16, resid, g.reshape(1, d))
    return (res[0], res[1]) if emit_x else (None, res[0])


def _resid_norm_kernel(a_ref, w_ref, r_ref, g_ref, x_ref, n_ref, wbf_ref, *, scale):
    @pl.when(pl.program_id(0) == 0)
    def _():
        wbf_ref[...] = w_ref[...].astype(_BF16)

    sub = a_ref.shape[0] // EPILOGUE_SPLIT
    for r in range(EPILOGUE_SPLIT):
        rows = slice(r * sub, (r + 1) * sub)
        x = r_ref[rows, :] + scale * _dot(a_ref[rows, :], wbf_ref[...])
        x_ref[rows, :] = x
        n_ref[rows, :] = _rms(x, g_ref[...]).astype(n_ref.dtype)


def mm_resid_norm(a, w, widx, resid, g, scale, name="mm_resid"):
    m, k = a.shape
    d = w.shape[-1]
    tm = _tile(m, 512)
    return pl.pallas_call(
        functools.partial(_resid_norm_kernel, scale=scale),
        grid=(m // tm,),
        in_specs=[pl.BlockSpec((tm, k), lambda i: (i, 0)),
                  _wspec(widx, (k, d), lambda i: (0, 0), pipeline_mode=pl.Buffered(1)),
                  pl.BlockSpec((tm, d), lambda i: (i, 0)),
                  pl.BlockSpec((1, d), lambda i: (0, 0))],
        out_specs=[pl.BlockSpec((tm, d), lambda i: (i, 0)),
                   pl.BlockSpec((tm, d), lambda i: (i, 0))],
        out_shape=[jax.ShapeDtypeStruct((m, d), _F32),
                   jax.ShapeDtypeStruct((m, d), _BF16)],
        scratch_shapes=[pltpu.VMEM((k, d), _BF16)],
        compiler_params=_params("arbitrary"),
        name=name,
    )(a, w, resid, g.reshape(1, d))


def _lb_kernel(p_ref, o_ref):
    p = p_ref[...]
    depth = p.shape[0]
    e = jnp.exp(p - jnp.max(p, axis=0, keepdims=True))
    sm = e / jnp.sum(e, axis=0, keepdims=True)
    run = jnp.zeros_like(sm[0:1])
    for layer in range(depth):
        run = run + sm[layer:layer + 1]
        o_ref[layer:layer + 1, :] = run - sm[0:1]


def hgrn_lower_bounds(p):
    return pl.pallas_call(
        _lb_kernel,
        out_shape=jax.ShapeDtypeStruct(p.shape, _F32),
        name="hgrn_lb",
    )(p)


def _hgrn_kernel(q_ref, f_ref, v_ref, g_ref, lb_ref, gn_ref, o_ref,
                 st_ref, sc_ref, qs_ref, ks_ref, dl_ref, *, nchunk, nh):
    c = HG_CHUNK
    sb = HG_SUB
    d = HG_HEAD
    ru = 8

    @pl.when(pl.program_id(2) == 0)
    def _():
        st_ref[...] = jnp.zeros_like(st_ref)

    gn = gn_ref[...]
    row = lax.broadcasted_iota(jnp.int32, (c, c), 0)
    col = lax.broadcasted_iota(jnp.int32, (c, c), 1)
    future = col > row
    lane_u = lax.broadcasted_iota(jnp.int32, (ru, c), 1)
    row3 = lax.broadcasted_iota(jnp.int32, (c, 3 * c), 0)
    col3 = lax.broadcasted_iota(jnp.int32, (c, 3 * c), 1)
    tril3 = (row3 >= (col3 & (c - 1))).astype(_BF16)

    def gates(r0, hh):
        cs = slice(hh * d, (hh + 1) * d)
        lb = lb_ref[:, cs]
        qp = q_ref[pl.ds(r0, c), cs]
        fp = f_ref[pl.ds(r0, c), cs]
        q = qp * _sigmoid(qp)
        e_abs = jnp.exp(-jnp.abs(fp))
        r_abs = 1.0 / (1.0 + e_abs)
        er_abs = e_abs * r_abs
        pos = fp >= 0.0
        f = lb + (1.0 - lb) * jnp.where(pos, r_abs, er_abs)
        k = (1.0 - lb) * jnp.where(pos, er_abs, r_abs)
        lf = jnp.log2(jnp.maximum(f, F_FLOOR))
        return q, k, lf

    def cumsum_rows(lf):
        return _dot(tril3, jnp.concatenate(_split3(lf), axis=0))

    def block_ref(b, lo):
        return b[lo - 1:lo, :] if lo > 0 else jnp.zeros((1, d), _F32)

    def prepare(r0, slot0):
        qkl = [gates(r0, hh) for hh in range(nh)]
        b_all = cumsum_rows(jnp.concatenate([lf for _, _, lf in qkl], axis=1))
        risks = []
        for hh, (q, k, _) in enumerate(qkl):
            slot = slot0 + hh
            b = b_all[:, hh * d:(hh + 1) * d]
            b_last = b[c - 1:c, :]
            risk = jnp.zeros((1, d), _F32)
            q_rows = []
            k_cols = []
            nblk = c // sb
            for bi in range(nblk):
                lo = bi * sb
                hi = lo + sb
                ref = block_ref(b, lo)
                risk = jnp.maximum(risk, ref - b[hi - 1:hi, :])
                qt = (q[lo:hi, :] * jnp.exp2(b[lo:hi, :] - ref)).astype(_BF16)
                kt = (k[:hi, :] * jnp.exp2(ref - b[:hi, :])).astype(_BF16)
                if hi < c:
                    kt = jnp.concatenate([kt, jnp.zeros((c - hi, d), _BF16)], axis=0)
                zero = jnp.zeros((sb, d), _BF16)
                q_rows.append(jnp.concatenate([zero] * bi + [qt] + [zero] * (nblk - 1 - bi), axis=1))
                k_cols.append(kt)
            scores = _dot_nt(jnp.concatenate(q_rows, axis=0), jnp.concatenate(k_cols, axis=1))
            sc_ref[slot] = jnp.where(future, 0.0, scores).astype(_BF16)
            qs_ref[slot] = (q * jnp.exp2(b)).astype(_BF16)
            ks_ref[slot] = (k * jnp.exp2(b_last - b)).astype(_BF16)
            dl_ref[slot] = jnp.broadcast_to(jnp.exp2(b_last), (8, d))
            risks.append(jnp.max(risk))
        return risks

    def direct_scores(r0, hh, slot):
        q, k, lf = gates(r0, hh)
        b = cumsum_rows(lf)
        bk = b - jnp.log2(k)
        units = []
        for bi in range(c // sb):
            lo = bi * sb
            if bi > 0:
                ref = block_ref(b, lo)
                qt = q[lo:lo + sb, :] * jnp.exp2(b[lo:lo + sb, :] - ref)
                kt = k[:lo, :] * jnp.exp2(ref - b[:lo, :])
                kt = jnp.concatenate([kt, jnp.zeros((c - lo, d), _F32)], axis=0)
                off = _dot_nt(qt.astype(_BF16), kt.astype(_BF16))
            for uu in range(sb // ru):
                t0 = lo + uu * ru
                b_u = b[t0:t0 + ru, :]
                q_u = q[t0:t0 + ru, :]
                sc = off[uu * ru:(uu + 1) * ru, :] if bi > 0 else jnp.zeros((ru, c), _F32)
                for s in range(lo, t0 + ru):
                    w = q_u * jnp.exp2(b_u - bk[s:s + 1, :])
                    sc = jnp.where(lane_u == s, jnp.sum(w, axis=-1, keepdims=True), sc)
                units.append(sc)
        sc_ref[slot] = jnp.where(future, 0.0, jnp.concatenate(units, axis=0)).astype(_BF16)

    def finish(r0, hh, slot):
        cs = slice(hh * d, (hh + 1) * d)
        v16 = v_ref[pl.ds(r0, c), cs].astype(_BF16)
        gp = g_ref[pl.ds(r0, c), cs]
        st = st_ref[hh]
        o = _dot(sc_ref[slot], v16) + _dot_nt(qs_ref[slot], st.astype(_BF16))
        st_ref[hh] = st * dl_ref[slot, 0:1, :] + _dot_tn(v16, ks_ref[slot])
        on = o * lax.rsqrt(jnp.mean(o * o, axis=-1, keepdims=True) + EPS) * gn
        o_ref[pl.ds(r0, c), cs] = (on * (gp * _sigmoid(gp))).astype(o_ref.dtype)

    def pair_work(pi, base):
        starts = [(2 * pi + half) * c for half in range(2)]
        starts = [s if isinstance(s, int) else pl.multiple_of(s, c) for s in starts]
        return [(starts[half], hh, base + half * nh + hh) for half in range(2) for hh in range(nh)]

    def prepare_pair(pi, base):
        work = pair_work(pi, base)
        risks = prepare(work[0][0], base) + prepare(work[nh][0], base + nh)
        worst = functools.reduce(jnp.maximum, risks)

        @pl.when(worst > HG_SAFE_LOG2)
        def _():
            for (r0, hh, slot), risk in zip(work, risks):
                @pl.when(risk > HG_SAFE_LOG2)
                def _():
                    direct_scores(r0, hh, slot)

    def finish_pair(pi, base):
        for r0, hh, slot in pair_work(pi, base):
            finish(r0, hh, slot)

    assert nchunk % 2 == 0
    npair = nchunk // 2
    nslot = 2 * nh
    prepare_pair(0, 0)

    def trip(pi, carry):
        base = (pi % 2) * nslot
        finish_pair(pi, base)
        prepare_pair(pi + 1, nslot - base)
        return carry

    lax.fori_loop(0, npair - 1, trip, 0)
    finish_pair(npair - 1, ((npair - 1) % 2) * nslot)


def hgrn_core(proj, lb, g_norm, bsz, seq):
    m = proj.shape[0]
    width = proj.shape[1] // 4
    heads = width // HG_HEAD
    nh = HG_GROUP
    gw = nh * HG_HEAD
    ngrp = heads // nh
    tb = _tile(seq, 1024)
    nsblk = seq // tb

    def spec(off):
        return pl.BlockSpec((tb, gw), lambda bb, h, s: (bb * nsblk + s, off * ngrp + h))

    return pl.pallas_call(
        functools.partial(_hgrn_kernel, nchunk=tb // HG_CHUNK, nh=nh),
        grid=(bsz, ngrp, nsblk),
        in_specs=[spec(0), spec(1), spec(2), spec(3),
                  pl.BlockSpec((1, gw), lambda bb, h, s: (0, h)),
                  pl.BlockSpec((1, HG_HEAD), lambda bb, h, s: (0, 0))],
        out_specs=pl.BlockSpec((tb, gw), lambda bb, h, s: (bb * nsblk + s, h)),
        out_shape=jax.ShapeDtypeStruct((m, width), _BF16),
        scratch_shapes=[pltpu.VMEM((nh, HG_HEAD, HG_HEAD), _F32),
                        pltpu.VMEM((4 * nh, HG_CHUNK, HG_CHUNK), _BF16),
                        pltpu.VMEM((4 * nh, HG_CHUNK, HG_HEAD), _BF16),
                        pltpu.VMEM((4 * nh, HG_CHUNK, HG_HEAD), _BF16),
                        pltpu.VMEM((4 * nh, 8, HG_HEAD), _F32)],
        compiler_params=_params("parallel", "parallel", "arbitrary"),
        name="hgrn_core",
    )(proj, proj, proj, proj, lb.reshape(1, width), g_norm.reshape(1, HG_HEAD))


def _gelu_tanh(x):
    c0 = math.sqrt(2.0 / math.pi)
    return 0.5 * x * (1.0 + jnp.tanh(c0 * (x + 0.044715 * (x * x * x))))


def _lru_kernel(gate_ref, u_ref, cw_ref, cb_ref, wa_ref, ba_ref, wx_ref, bx_ref, lam_ref,
                o_ref, ext_ref, sa_ref, sx_ref, h_ref, *, ts, pad):
    width = u_ref.shape[1]
    blk = width // LRU_BLOCKS
    halo = 8

    @pl.when(pl.program_id(1) == 0)
    def _():
        ext_ref[0:halo, :] = jnp.zeros((halo, width), _F32)
        h_ref[...] = jnp.zeros_like(h_ref)

    @pl.when(pl.program_id(1) > 0)
    def _():
        ext_ref[0:halo, :] = ext_ref[ts:ts + halo, :]

    ext_ref[halo:halo + ts, :] = u_ref[...]
    uc = cb_ref[...] + cw_ref[0:1, :] * ext_ref[pl.ds(halo - 3, ts), :]
    for j in range(1, CONV_W):
        uc = uc + cw_ref[j:j + 1, :] * ext_ref[pl.ds(halo - 3 + j, ts), :]

    u16 = uc.astype(_BF16)
    r_parts = []
    i_parts = []
    for n in range(LRU_BLOCKS):
        ub = u16[:, n * blk:(n + 1) * blk]
        r_parts.append(_dot(ub, wa_ref[n].astype(_BF16)))
        i_parts.append(_dot(ub, wx_ref[n].astype(_BF16)))
    r = _sigmoid(jnp.concatenate(r_parts, axis=1) + ba_ref[...])
    ig = _sigmoid(jnp.concatenate(i_parts, axis=1) + bx_ref[...])

    lam = lam_ref[...]
    softplus = jnp.maximum(-lam, 0.0) + jnp.log(1.0 + jnp.exp(-jnp.abs(lam)))
    log_a = -LRU_C * r * softplus
    a = jnp.exp(log_a)
    inp = jnp.sqrt(jnp.maximum(1.0 - a * a, 0.0)) * (ig * uc)

    sa_ref[0:pad, :] = jnp.ones((pad, width), _F32)
    sx_ref[0:pad, :] = jnp.zeros((pad, width), _F32)
    shift = 1
    while shift < ts:
        sa_ref[pad:pad + ts, :] = a
        sx_ref[pad:pad + ts, :] = inp
        a_sh = sa_ref[pl.ds(pad - shift, ts), :]
        x_sh = sx_ref[pl.ds(pad - shift, ts), :]
        inp = a * x_sh + inp
        a = a * a_sh
        shift *= 2

    h = inp + a * h_ref[0:1, :]
    h_ref[0:1, :] = h[ts - 1:ts, :]
    o_ref[...] = (_gelu_tanh(gate_ref[...]) * h).astype(o_ref.dtype)


def lru_core(proj, conv_w, conv_b, w_a, b_a, w_x, b_x, lam, bsz, seq):
    m = proj.shape[0]
    width = proj.shape[1] // 2
    blk = width // LRU_BLOCKS
    ts = _tile(seq, 256)
    pad = ts // 2 if ts >= 16 else 8
    pad = max(pad, 8)
    nsblk = seq // ts
    vec = lambda t: t.reshape(1, width)
    row_spec = pl.BlockSpec((1, width), lambda bb, s: (0, 0))
    w_spec = pl.BlockSpec((LRU_BLOCKS, blk, blk), lambda bb, s: (0, 0, 0))
    return pl.pallas_call(
        functools.partial(_lru_kernel, ts=ts, pad=pad),
        grid=(bsz, nsblk),
        in_specs=[pl.BlockSpec((ts, width), lambda bb, s: (bb * nsblk + s, 0)),
                  pl.BlockSpec((ts, width), lambda bb, s: (bb * nsblk + s, 1)),
                  pl.BlockSpec((CONV_W, width), lambda bb, s: (0, 0)),
                  row_spec, w_spec, row_spec, w_spec, row_spec, row_spec],
        out_specs=pl.BlockSpec((ts, width), lambda bb, s: (bb * nsblk + s, 0)),
        out_shape=jax.ShapeDtypeStruct((m, width), _BF16),
        scratch_shapes=[pltpu.VMEM((ts + 8, width), _F32),
                        pltpu.VMEM((ts + pad, width), _F32),
                        pltpu.VMEM((ts + pad, width), _F32),
                        pltpu.VMEM((8, width), _F32)],
        compiler_params=_params("parallel", "arbitrary"),
        name="lru_core",
    )(proj, proj, conv_w, vec(conv_b), w_a, vec(b_a), w_x, vec(b_x), vec(lam))


def _softcap(t):
    return GATE_CAP * jnp.tanh(t / GATE_CAP)


def _log_sigmoid(x):
    return jnp.minimum(x, 0.0) - jnp.log(1.0 + jnp.exp(-jnp.abs(x)))


def _mlstm_kernel(q_ref, k_ref, v_ref, og_ref, gt_ref, bif_ref, gn_ref, o_ref,
                  c_ref, n_ref, m_ref, *, nchunk, c, dqk, dv, nh):
    h_base = pl.program_id(1) * nh

    @pl.when(pl.program_id(2) == 0)
    def _():
        c_ref[...] = jnp.zeros_like(c_ref)
        n_ref[...] = jnp.zeros_like(n_ref)
        m_ref[...] = jnp.zeros_like(m_ref)

    row = lax.broadcasted_iota(jnp.int32, (c, c), 0)
    col = lax.broadcasted_iota(jnp.int32, (c, c), 1)
    causal = row >= col
    tril = causal.astype(_BF16)
    eye = (lax.broadcasted_iota(jnp.int32, (2 * ML_HEADS, LANE), 0)
           == lax.broadcasted_iota(jnp.int32, (2 * ML_HEADS, LANE), 1)).astype(_BF16)
    sub16 = lax.broadcasted_iota(jnp.int32, (2 * ML_HEADS, c), 0)
    lane1 = lax.broadcasted_iota(jnp.int32, (1, LANE), 1)
    gn = gn_ref[...]
    bif = bif_ref[...]
    kscale = dqk ** -0.5

    def head_chunk(r0, hh, gl, cum, gl_rows, cum_rows):
        h_id = h_base + hh
        qs = slice(hh * dqk, (hh + 1) * dqk)
        vs = slice(hh * dv, (hh + 1) * dv)
        q = q_ref[pl.ds(r0, c), qs]
        k = k_ref[pl.ds(r0, c), qs] * kscale
        v = v_ref[pl.ds(r0, c), vs]
        og = og_ref[pl.ds(r0, c), vs]
        b_col = jnp.sum(jnp.where(lane1 == h_id + ML_HEADS, cum, 0.0), axis=-1, keepdims=True)
        i_col = jnp.sum(jnp.where(lane1 == h_id, gl, 0.0), axis=-1, keepdims=True)
        b_row = jnp.sum(jnp.where(sub16 == h_id + ML_HEADS, cum_rows, 0.0), axis=0, keepdims=True)
        i_row = jnp.sum(jnp.where(sub16 == h_id, gl_rows, 0.0), axis=0, keepdims=True)

        m_st = m_ref[hh]
        c_st = c_ref[hh]
        n_st = n_ref[hh]

        d_mat = jnp.where(causal, b_col - b_row + i_row, NEG_BIG)
        inter = b_col + m_st
        m_t = jnp.maximum(inter, jnp.max(d_mat, axis=-1, keepdims=True))
        w_intra = jnp.where(causal, jnp.exp(jnp.minimum(d_mat - m_t, 0.0)), 0.0)
        w_inter = jnp.exp(inter - m_t)
        q16 = q.astype(_BF16)
        k16 = k.astype(_BF16)
        v16 = v.astype(_BF16)
        qk = _dot_nt(q16, k16) * w_intra
        num = _dot(qk.astype(_BF16), v16) + w_inter * _dot(q16, c_st.astype(_BF16))
        den = jnp.sum(qk, axis=-1, keepdims=True) + w_inter * jnp.sum(q * n_st, axis=-1, keepdims=True)
        hval = num / jnp.maximum(jnp.abs(den), jnp.exp(-m_t))

        g_tot = b_col[c - 1:c, :]
        upd_col = g_tot - b_col + i_col
        m_new = jnp.maximum(g_tot + m_st, jnp.max(upd_col, axis=0, keepdims=True))
        w_upd = jnp.exp(upd_col - m_new)
        decay = jnp.exp(g_tot + m_st - m_new)
        kw = k * w_upd
        c_ref[hh] = decay * c_st + _dot_tn(kw.astype(_BF16), v16)
        n_ref[hh] = decay * n_st + jnp.sum(kw, axis=0, keepdims=True)
        m_ref[hh] = m_new

        hn = hval * lax.rsqrt(jnp.mean(hval * hval, axis=-1, keepdims=True) + EPS) * gn
        o_ref[pl.ds(r0, c), vs] = (hn * _sigmoid(og)).astype(o_ref.dtype)

    def chunk(ci, carry):
        r0 = pl.multiple_of(ci * c, c)
        gt = jnp.where(lane1 < 2 * ML_HEADS, gt_ref[pl.ds(r0, c), :], 0.0)
        pre = _softcap(gt + bif)
        is_f = jnp.logical_and(lane1 >= ML_HEADS, lane1 < 2 * ML_HEADS)
        gl = jnp.where(is_f, _log_sigmoid(pre), pre)
        cum = _dot_exact_lhs(tril, gl)
        ch, cm, cl = _split3(cum)
        cum_rows = _dot_nt(eye, ch) + _dot_nt(eye, cm) + _dot_nt(eye, cl)
        gh, gm, gl3 = _split3(gl)
        gl_rows = _dot_nt(eye, gh) + _dot_nt(eye, gm) + _dot_nt(eye, gl3)
        for hh in range(nh):
            head_chunk(r0, hh, gl, cum, gl_rows, cum_rows)
        return carry

    lax.fori_loop(0, nchunk, chunk, 0)


def mlstm_core(proj, b_if, g_norm, bsz, seq, dqk, dv):
    m = proj.shape[0]
    heads = ML_HEADS
    hv = heads * dv
    c = _tile(seq, ML_CHUNK)
    tb = _tile(seq, 512)
    nsblk = seq // tb
    gate_blk = (2 * heads * dqk + 2 * hv) // LANE
    bif = jnp.zeros((1, LANE), _F32).at[0, :2 * heads].set(b_if.reshape(-1))
    nh = ML_GROUP
    ngrp = heads // nh
    qw = nh * dqk
    vw = nh * dv
    v_off = (2 * heads * dqk) // vw
    return pl.pallas_call(
        functools.partial(_mlstm_kernel, nchunk=tb // c, c=c, dqk=dqk, dv=dv, nh=nh),
        grid=(bsz, ngrp, nsblk),
        in_specs=[pl.BlockSpec((tb, qw), lambda bb, h, s: (bb * nsblk + s, h)),
                  pl.BlockSpec((tb, qw), lambda bb, h, s: (bb * nsblk + s, ngrp + h)),
                  pl.BlockSpec((tb, vw), lambda bb, h, s: (bb * nsblk + s, v_off + h)),
                  pl.BlockSpec((tb, vw), lambda bb, h, s: (bb * nsblk + s, v_off + ngrp + h)),
                  pl.BlockSpec((tb, LANE), lambda bb, h, s: (bb * nsblk + s, gate_blk)),
                  pl.BlockSpec((1, LANE), lambda bb, h, s: (0, 0)),
                  pl.BlockSpec((1, dv), lambda bb, h, s: (0, 0))],
        out_specs=pl.BlockSpec((tb, vw), lambda bb, h, s: (bb * nsblk + s, h)),
        out_shape=jax.ShapeDtypeStruct((m, hv), _BF16),
        scratch_shapes=[pltpu.VMEM((nh, dqk, dv), _F32),
                        pltpu.VMEM((nh, 1, dqk), _F32),
                        pltpu.VMEM((nh, 1, 1), _F32)],
        compiler_params=_params("parallel", "parallel", "arbitrary"),
        name="mlstm_core",
    )(proj, proj, proj, proj, proj, bif, g_norm.reshape(1, dv))


def _xattn_kernel(q_ref, k_ref, v_ref, o_ref, *, heads):
    d = q_ref.shape[1]
    hd = d // heads
    scale = hd ** -0.5
    for h in range(heads):
        qh = q_ref[:, h * hd:(h + 1) * hd]
        kh = k_ref[:, h * hd:(h + 1) * hd]
        vh = v_ref[:, h * hd:(h + 1) * hd]
        s = _dot_nt(qh, kh) * scale
        s = s - jnp.max(s, axis=-1, keepdims=True)
        e = jnp.exp(s)
        p = e / jnp.sum(e, axis=-1, keepdims=True)
        o_ref[:, h * hd:(h + 1) * hd] = _dot(p.astype(_BF16), vh).astype(o_ref.dtype)


def xattn_core(q, kv, bsz, seq, n_mem):
    m, d = q.shape
    tq = _tile(seq, 512)
    nq = seq // tq
    return pl.pallas_call(
        functools.partial(_xattn_kernel, heads=XA_HEADS),
        grid=(bsz, nq),
        in_specs=[pl.BlockSpec((tq, d), lambda bb, s: (bb * nq + s, 0)),
                  pl.BlockSpec((n_mem, d), lambda bb, s: (bb, 0)),
                  pl.BlockSpec((n_mem, d), lambda bb, s: (bb, 1))],
        out_specs=pl.BlockSpec((tq, d), lambda bb, s: (bb * nq + s, 0)),
        out_shape=jax.ShapeDtypeStruct((m, d), _BF16),
        compiler_params=_params("parallel", "parallel"),
        name="xattn_core",
    )(q, kv, kv)


def kernel(x, mem, mem_norm_g, norm_g, final_norm_g, ffn_w_gu, ffn_w_down, xa_w_q, xa_w_kv, xa_w_o,
           hg_lb_param, hg_w_in, hg_g_norm, hg_w_out,
           lru_w_in, lru_conv_w, lru_conv_b, lru_w_a, lru_b_a, lru_w_x, lru_b_x, lru_lambda, lru_w_out,
           ml_w_in, ml_b_if, ml_g_norm, ml_w_out):
    bsz, seq, d = x.shape
    n_mem = mem.shape[1]
    depth = norm_g.shape[0]
    m = bsz * seq
    ml_dqk = d // (2 * ML_HEADS)
    ml_dv = d // ML_HEADS

    xf = x.reshape(m, d)
    mem_n = rms_cast(mem.reshape(bsz * n_mem, d), mem_norm_g)
    lb_all = hgrn_lower_bounds(hg_lb_param)

    xn = rms_cast(xf, norm_g[0, 0])
    for layer in range(depth):
        kind = layer % N_MIXERS
        idx = layer // N_MIXERS

        h = mm_swiglu(xn, ffn_w_gu, (layer, 0))
        xf, xn = mm_down_norm(h, cast_weight(ffn_w_down, (layer, 0)), xf, norm_g[layer, 1], 0.5)

        if kind == 0:
            proj = mm_ws(xn, hg_w_in, (idx,), _F32, name="hg_in")
            y = hgrn_core(proj, lb_all[layer], hg_g_norm[idx], bsz, seq)
            w_out = hg_w_out
        elif kind == 1:
            proj = mm_ws(xn, lru_w_in, (idx,), _F32, name="lru_in")
            y = lru_core(proj, lru_conv_w[idx], lru_conv_b[idx], lru_w_a[idx], lru_b_a[idx],
                         lru_w_x[idx], lru_b_x[idx], lru_lambda[idx], bsz, seq)
            w_out = lru_w_out
        else:
            proj = mm_ws(xn, ml_w_in, (idx,), _F32, tn=1280, name="ml_in")
            y = mlstm_core(proj, ml_b_if[idx], ml_g_norm[idx], bsz, seq, ml_dqk, ml_dv)
            w_out = ml_w_out
        xf, xn = mm_resid_norm(y, w_out, (idx,), xf, norm_g[layer, 2], 1.0, name="mixer_out")

        q = mm_ws(xn, xa_w_q, (layer,), _BF16, name="xa_q")
        kv = mm_ws(mem_n, xa_w_kv, (layer,), _BF16, name="xa_kv")
        o = xattn_core(q, kv, bsz, seq, n_mem)
        xf, xn = mm_resid_norm(o, xa_w_o, (layer,), xf, norm_g[layer, 3], 1.0, name="xa_out")

        h = mm_swiglu(xn, ffn_w_gu, (layer, 1))
        w16 = cast_weight(ffn_w_down, (layer, 1))
        if layer + 1 < depth:
            xf, xn = mm_down_norm(h, w16, xf, norm_g[layer + 1, 0], 0.5)
        else:
            _, out = mm_down_norm(h, w16, xf, final_norm_g, 0.5, emit_x=False, norm_dtype=_F32,
                                  name="ffn_down_final")
    return out.reshape(bsz, seq, d)
```

```python
import functools
import math

import jax
import jax.numpy as jnp
from jax import lax
from jax.experimental import pallas as pl
from jax.experimental.pallas import tpu as pltpu

EPS = 1e-6
N_MIXERS = 3
HG_HEAD = 128
HG_CHUNK = 64
HG_SUB = 16
HG_GROUP = 8
HG_SAFE_LOG2 = 100.0
F_FLOOR = 1e-12
LRU_BLOCKS = 8
CONV_W = 4
LRU_C = 8.0
ML_HEADS = 8
ML_CHUNK = 256
ML_GROUP = 8
GATE_CAP = 15.0
NEG_BIG = -1e30
XA_HEADS = 4
LANE = 128
FF_TILE = 512
SWIGLU_SPLIT = 4
EPILOGUE_SPLIT = 4
DOWN_ROWS = 512
DOWN_SPLIT = 2
XA_SPLIT = 2
VMEM_LIMIT = 56 * 1024 * 1024

_BF16 = jnp.bfloat16
_F32 = jnp.float32


def _params(*sem):
    return pltpu.CompilerParams(dimension_semantics=sem, vmem_limit_bytes=VMEM_LIMIT)


def _tile(dim, pref):
    t = min(dim, pref)
    while dim % t:
        t //= 2
    return t


def _sigmoid(x):
    return 1.0 / (1.0 + jnp.exp(-x))


def _dot(a, b):
    return jnp.dot(a, b, preferred_element_type=_F32)


def _dot_nt(a, b):
    return lax.dot_general(a, b, (((1,), (1,)), ((), ())), preferred_element_type=_F32)


def _dot_tn(a, b):
    return lax.dot_general(a, b, (((0,), (0,)), ((), ())), preferred_element_type=_F32)


def _split3(x):
    hi = x.astype(_BF16)
    r1 = x - hi.astype(_F32)
    mid = r1.astype(_BF16)
    lo = (r1 - mid.astype(_F32)).astype(_BF16)
    return hi, mid, lo


def _dot_exact_lhs(sel, x):
    hi, mid, lo = _split3(x)
    return _dot(sel, hi) + _dot(sel, mid) + _dot(sel, lo)


def _rms(x, g):
    return x * lax.rsqrt(jnp.mean(x * x, axis=-1, keepdims=True) + EPS) * g


def _rms_kernel(x_ref, g_ref, o_ref):
    o_ref[...] = _rms(x_ref[...], g_ref[...]).astype(o_ref.dtype)


def rms_cast(x, g, out_dtype=_BF16):
    m, d = x.shape
    tm = _tile(m, 512)
    return pl.pallas_call(
        _rms_kernel,
        grid=(m // tm,),
        in_specs=[pl.BlockSpec((tm, d), lambda i: (i, 0)),
                  pl.BlockSpec((1, d), lambda i: (0, 0))],
        out_specs=pl.BlockSpec((tm, d), lambda i: (i, 0)),
        out_shape=jax.ShapeDtypeStruct((m, d), out_dtype),
        compiler_params=_params("parallel"),
        name="rms_cast",
    )(x, g.reshape(1, d))


def _wspec(widx, block, index_map, **kw):
    lead = tuple(widx)
    return pl.BlockSpec((None,) * len(lead) + tuple(block), lambda *g: lead + tuple(index_map(*g)), **kw)


def _mm_ws_kernel(a_ref, w_ref, o_ref, wbf_ref):
    @pl.when(pl.program_id(1) == 0)
    def _():
        wbf_ref[...] = w_ref[...].astype(_BF16)

    o_ref[...] = _dot(a_ref[...], wbf_ref[...]).astype(o_ref.dtype)


def mm_ws(a, w, widx, out_dtype, tm=1024, tn=1024, name="mm_ws"):
    m, k = a.shape
    n = w.shape[-1]
    tm = _tile(m, tm)
    nj = -(-n // tn)
    return pl.pallas_call(
        _mm_ws_kernel,
        grid=(nj, m // tm),
        in_specs=[pl.BlockSpec((tm, k), lambda j, i: (i, 0)),
                  _wspec(widx, (k, tn), lambda j, i: (0, j))],
        out_specs=pl.BlockSpec((tm, tn), lambda j, i: (i, j)),
        out_shape=jax.ShapeDtypeStruct((m, nj * tn), out_dtype),
        scratch_shapes=[pltpu.VMEM((k, tn), _BF16)],
        compiler_params=_params("arbitrary", "arbitrary"),
        name=name,
    )(a, w)


def _swiglu_kernel(a_ref, wg_ref, wu_ref, o_ref, wbf_ref, *, tnh, rem):
    lead = (0,) * (len(wg_ref.shape) - 2)
    first_row_block = pl.program_id(1) == 0
    is_last = pl.program_id(0) == pl.num_programs(0) - 1
    sub = a_ref.shape[0] // SWIGLU_SPLIT

    def compute(width):
        for r in range(SWIGLU_SPLIT):
            rows = slice(r * sub, (r + 1) * sub)
            acc = _dot(a_ref[rows, :], wbf_ref[:, :2 * width])
            gate = acc[:, :width]
            up = acc[:, width:]
            o_ref[rows, :width] = (gate * _sigmoid(gate) * up).astype(o_ref.dtype)

    if rem == tnh:
        @pl.when(first_row_block)
        def _():
            wbf_ref[:, :tnh] = wg_ref[lead].astype(_BF16)
            wbf_ref[:, tnh:] = wu_ref[lead].astype(_BF16)

        compute(tnh)
        return

    @pl.when(jnp.logical_and(first_row_block, jnp.logical_not(is_last)))
    def _():
        wbf_ref[:, :tnh] = wg_ref[lead].astype(_BF16)
        wbf_ref[:, tnh:] = wu_ref[lead].astype(_BF16)

    @pl.when(jnp.logical_and(first_row_block, is_last))
    def _():
        wbf_ref[:, :rem] = wg_ref[lead][:, tnh - rem:].astype(_BF16)
        wbf_ref[:, rem:2 * rem] = wu_ref[lead][:, tnh - rem:].astype(_BF16)

    @pl.when(jnp.logical_not(is_last))
    def _():
        compute(tnh)

    @pl.when(is_last)
    def _():
        compute(rem)


def mm_swiglu(a, w_gu, widx, tnh=FF_TILE):
    m, k = a.shape
    d_ff = w_gu.shape[-1] // 2
    tm = _tile(m, 1024)
    nj = -(-d_ff // tnh)

    lead = tuple(widx)
    lead_blk = (1,) * len(lead)

    assert d_ff % LANE == 0 and tnh % LANE == 0

    def start(j, base=0):
        return LANE * (base // LANE + jnp.minimum(j * (tnh // LANE), (d_ff - tnh) // LANE))

    def elem_spec(block, index_map):
        return pl.BlockSpec(tuple(pl.Element(b) for b in block), index_map)

    return pl.pallas_call(
        functools.partial(_swiglu_kernel, tnh=tnh, rem=d_ff - (nj - 1) * tnh),
        grid=(nj, m // tm),
        in_specs=[pl.BlockSpec((tm, k), lambda j, i: (i, 0)),
                  elem_spec(lead_blk + (k, tnh), lambda j, i: lead + (0, start(j))),
                  elem_spec(lead_blk + (k, tnh), lambda j, i: lead + (0, start(j, d_ff)))],
        out_specs=pl.BlockSpec((tm, tnh), lambda j, i: (i, j)),
        out_shape=jax.ShapeDtypeStruct((m, d_ff), _BF16),
        scratch_shapes=[pltpu.VMEM((k, 2 * tnh), _BF16)],
        compiler_params=_params("arbitrary", "arbitrary"),
        name="mm_swiglu",
    )(a, w_gu, w_gu)


def _cast_kernel(w_ref, o_ref):
    o_ref[...] = w_ref[...].astype(o_ref.dtype)


def cast_weight(w, widx, row_blocks=8):
    k, d = w.shape[-2:]
    tk = k // row_blocks
    assert tk * row_blocks == k and tk % 16 == 0
    return pl.pallas_call(
        _cast_kernel,
        grid=(row_blocks,),
        in_specs=[_wspec(widx, (tk, d), lambda i: (i, 0))],
        out_specs=pl.BlockSpec((tk, d), lambda i: (i, 0)),
        out_shape=jax.ShapeDtypeStruct((k, d), _BF16),
        compiler_params=_params("parallel"),
        name="cast_weight",
    )(w)


def _down_norm_kernel(a_ref, w_ref, r_ref, g_ref, *out_refs, scale, emit_x):
    n_ref = out_refs[-1]
    sub = a_ref.shape[0] // DOWN_SPLIT
    for r in range(DOWN_SPLIT):
        rows = slice(r * sub, (r + 1) * sub)
        x = r_ref[rows, :] + scale * _dot(a_ref[rows, :], w_ref[...])
        if emit_x:
            out_refs[0][rows, :] = x
        n_ref[rows, :] = _rms(x, g_ref[...]).astype(n_ref.dtype)


def mm_down_norm(a, w16, resid, g, scale, *, emit_x=True, norm_dtype=_BF16, name="ffn_down"):
    m, k = a.shape
    d = w16.shape[-1]
    tm = _tile(m, DOWN_ROWS)
    row_spec = pl.BlockSpec((tm, d), lambda i: (i, 0))
    out_shape = [jax.ShapeDtypeStruct((m, d), norm_dtype)]
    if emit_x:
        out_shape.insert(0, jax.ShapeDtypeStruct((m, d), _F32))
    res = pl.pallas_call(
        functools.partial(_down_norm_kernel, scale=scale, emit_x=emit_x),
        grid=(m // tm,),
        in_specs=[pl.BlockSpec((tm, k), lambda i: (i, 0)),
                  pl.BlockSpec((k, d), lambda i: (0, 0), pipeline_mode=pl.Buffered(1)),
                  row_spec,
                  pl.BlockSpec((1, d), lambda i: (0, 0))],
        out_specs=[row_spec] * len(out_shape),
        out_shape=out_shape,
        compiler_params=_params("parallel"),
        name=name,
    )(a, w16, resid, g.reshape(1, d))
    return (res[0], res[1]) if emit_x else (None, res[0])


def _resid_norm_kernel(a_ref, w_ref, r_ref, g_ref, x_ref, n_ref, wbf_ref, *, scale):
    @pl.when(pl.program_id(0) == 0)
    def _():
        wbf_ref[...] = w_ref[...].astype(_BF16)

    sub = a_ref.shape[0] // EPILOGUE_SPLIT
    for r in range(EPILOGUE_SPLIT):
        rows = slice(r * sub, (r + 1) * sub)
        x = r_ref[rows, :] + scale * _dot(a_ref[rows, :], wbf_ref[...])
        x_ref[rows, :] = x
        n_ref[rows, :] = _rms(x, g_ref[...]).astype(n_ref.dtype)


def mm_resid_norm(a, w, widx, resid, g, scale, name="mm_resid"):
    m, k = a.shape
    d = w.shape[-1]
    tm = _tile(m, 512)
    return pl.pallas_call(
        functools.partial(_resid_norm_kernel, scale=scale),
        grid=(m // tm,),
        in_specs=[pl.BlockSpec((tm, k), lambda i: (i, 0)),
                  _wspec(widx, (k, d), lambda i: (0, 0), pipeline_mode=pl.Buffered(1)),
                  pl.BlockSpec((tm, d), lambda i: (i, 0)),
                  pl.BlockSpec((1, d), lambda i: (0, 0))],
        out_specs=[pl.BlockSpec((tm, d), lambda i: (i, 0)),
                   pl.BlockSpec((tm, d), lambda i: (i, 0))],
        out_shape=[jax.ShapeDtypeStruct((m, d), _F32),
                   jax.ShapeDtypeStruct((m, d), _BF16)],
        scratch_shapes=[pltpu.VMEM((k, d), _BF16)],
        compiler_params=_params("arbitrary"),
        name=name,
    )(a, w, resid, g.reshape(1, d))


def _lb_kernel(p_ref, o_ref):
    p = p_ref[...]
    depth = p.shape[0]
    e = jnp.exp(p - jnp.max(p, axis=0, keepdims=True))
    sm = e / jnp.sum(e, axis=0, keepdims=True)
    run = jnp.zeros_like(sm[0:1])
    for layer in range(depth):
        run = run + sm[layer:layer + 1]
        o_ref[layer:layer + 1, :] = run - sm[0:1]


def hgrn_lower_bounds(p):
    return pl.pallas_call(
        _lb_kernel,
        out_shape=jax.ShapeDtypeStruct(p.shape, _F32),
        name="hgrn_lb",
    )(p)


def _hgrn_kernel(q_ref, f_ref, v_ref, g_ref, lb_ref, gn_ref, o_ref,
                 st_ref, sc_ref, qs_ref, ks_ref, dl_ref, *, nchunk, nh):
    c = HG_CHUNK
    sb = HG_SUB
    d = HG_HEAD
    ru = 8

    @pl.when(pl.program_id(2) == 0)
    def _():
        st_ref[...] = jnp.zeros_like(st_ref)

    gn = gn_ref[...]
    row = lax.broadcasted_iota(jnp.int32, (c, c), 0)
    col = lax.broadcasted_iota(jnp.int32, (c, c), 1)
    future = col > row
    lane_u = lax.broadcasted_iota(jnp.int32, (ru, c), 1)
    row3 = lax.broadcasted_iota(jnp.int32, (c, 3 * c), 0)
    col3 = lax.broadcasted_iota(jnp.int32, (c, 3 * c), 1)
    tril3 = (row3 >= (col3 & (c - 1))).astype(_BF16)

    def gates(r0, hh):
        cs = slice(hh * d, (hh + 1) * d)
        lb = lb_ref[:, cs]
        qp = q_ref[pl.ds(r0, c), cs]
        fp = f_ref[pl.ds(r0, c), cs]
        q = qp * _sigmoid(qp)
        e_abs = jnp.exp(-jnp.abs(fp))
        r_abs = 1.0 / (1.0 + e_abs)
        er_abs = e_abs * r_abs
        pos = fp >= 0.0
        f = lb + (1.0 - lb) * jnp.where(pos, r_abs, er_abs)
        k = (1.0 - lb) * jnp.where(pos, er_abs, r_abs)
        lf = jnp.log2(jnp.maximum(f, F_FLOOR))
        return q, k, lf

    def cumsum_rows(lf):
        return _dot(tril3, jnp.concatenate(_split3(lf), axis=0))

    def block_ref(b, lo):
        return b[lo - 1:lo, :] if lo > 0 else jnp.zeros((1, d), _F32)

    def prepare(r0, slot0):
        qkl = [gates(r0, hh) for hh in range(nh)]
        b_all = cumsum_rows(jnp.concatenate([lf for _, _, lf in qkl], axis=1))
        risks = []
        for hh, (q, k, _) in enumerate(qkl):
            slot = slot0 + hh
            b = b_all[:, hh * d:(hh + 1) * d]
            b_last = b[c - 1:c, :]
            risk = jnp.zeros((1, d), _F32)
            q_rows = []
            k_cols = []
            nblk = c // sb
            for bi in range(nblk):
                lo = bi * sb
                hi = lo + sb
                ref = block_ref(b, lo)
                risk = jnp.maximum(risk, ref - b[hi - 1:hi, :])
                qt = (q[lo:hi, :] * jnp.exp2(b[lo:hi, :] - ref)).astype(_BF16)
                kt = (k[:hi, :] * jnp.exp2(ref - b[:hi, :])).astype(_BF16)
                if hi < c:
                    kt = jnp.concatenate([kt, jnp.zeros((c - hi, d), _BF16)], axis=0)
                zero = jnp.zeros((sb, d), _BF16)
                q_rows.append(jnp.concatenate([zero] * bi + [qt] + [zero] * (nblk - 1 - bi), axis=1))
                k_cols.append(kt)
            scores = _dot_nt(jnp.concatenate(q_rows, axis=0), jnp.concatenate(k_cols, axis=1))
            sc_ref[slot] = jnp.where(future, 0.0, scores).astype(_BF16)
            qs_ref[slot] = (q * jnp.exp2(b)).astype(_BF16)
            ks_ref[slot] = (k * jnp.exp2(b_last - b)).astype(_BF16)
            dl_ref[slot] = jnp.broadcast_to(jnp.exp2(b_last), (8, d))
            risks.append(jnp.max(risk))
        return risks

    def direct_scores(r0, hh, slot):
        q, k, lf = gates(r0, hh)
        b = cumsum_rows(lf)
        bk = b - jnp.log2(k)
        units = []
        for bi in range(c // sb):
            lo = bi * sb
            if bi > 0:
                ref = block_ref(b, lo)
                qt = q[lo:lo + sb, :] * jnp.exp2(b[lo:lo + sb, :] - ref)
                kt = k[:lo, :] * jnp.exp2(ref - b[:lo, :])
                kt = jnp.concatenate([kt, jnp.zeros((c - lo, d), _F32)], axis=0)
                off = _dot_nt(qt.astype(_BF16), kt.astype(_BF16))
            for uu in range(sb // ru):
                t0 = lo + uu * ru
                b_u = b[t0:t0 + ru, :]
                q_u = q[t0:t0 + ru, :]
                sc = off[uu * ru:(uu + 1) * ru, :] if bi > 0 else jnp.zeros((ru, c), _F32)
                for s in range(lo, t0 + ru):
                    w = q_u * jnp.exp2(b_u - bk[s:s + 1, :])
                    sc = jnp.where(lane_u == s, jnp.sum(w, axis=-1, keepdims=True), sc)
                units.append(sc)
        sc_ref[slot] = jnp.where(future, 0.0, jnp.concatenate(units, axis=0)).astype(_BF16)

    def finish(r0, hh, slot):
        cs = slice(hh * d, (hh + 1) * d)
        v16 = v_ref[pl.ds(r0, c), cs].astype(_BF16)
        gp = g_ref[pl.ds(r0, c), cs]
        st = st_ref[hh]
        o = _dot(sc_ref[slot], v16) + _dot_nt(qs_ref[slot], st.astype(_BF16))
        st_ref[hh] = st * dl_ref[slot, 0:1, :] + _dot_tn(v16, ks_ref[slot])
        on = o * lax.rsqrt(jnp.mean(o * o, axis=-1, keepdims=True) + EPS) * gn
        o_ref[pl.ds(r0, c), cs] = (on * (gp * _sigmoid(gp))).astype(o_ref.dtype)

    def pair_work(pi, base):
        starts = [(2 * pi + half) * c for half in range(2)]
        starts = [s if isinstance(s, int) else pl.multiple_of(s, c) for s in starts]
        return [(starts[half], hh, base + half * nh + hh) for half in range(2) for hh in range(nh)]

    def prepare_pair(pi, base):
        work = pair_work(pi, base)
        risks = prepare(work[0][0], base) + prepare(work[nh][0], base + nh)
        worst = functools.reduce(jnp.maximum, risks)

        @pl.when(worst > HG_SAFE_LOG2)
        def _():
            for (r0, hh, slot), risk in zip(work, risks):
                @pl.when(risk > HG_SAFE_LOG2)
                def _():
                    direct_scores(r0, hh, slot)

    def finish_pair(pi, base):
        for r0, hh, slot in pair_work(pi, base):
            finish(r0, hh, slot)

    assert nchunk % 2 == 0
    npair = nchunk // 2
    nslot = 2 * nh
    prepare_pair(0, 0)

    def trip(pi, carry):
        base = (pi % 2) * nslot
        finish_pair(pi, base)
        prepare_pair(pi + 1, nslot - base)
        return carry

    lax.fori_loop(0, npair - 1, trip, 0)
    finish_pair(npair - 1, ((npair - 1) % 2) * nslot)


def hgrn_core(proj, lb, g_norm, bsz, seq):
    m = proj.shape[0]
    width = proj.shape[1] // 4
    heads = width // HG_HEAD
    nh = HG_GROUP
    gw = nh * HG_HEAD
    ngrp = heads // nh
    tb = _tile(seq, 1024)
    nsblk = seq // tb

    def spec(off):
        return pl.BlockSpec((tb, gw), lambda bb, h, s: (bb * nsblk + s, off * ngrp + h))

    return pl.pallas_call(
        functools.partial(_hgrn_kernel, nchunk=tb // HG_CHUNK, nh=nh),
        grid=(bsz, ngrp, nsblk),
        in_specs=[spec(0), spec(1), spec(2), spec(3),
                  pl.BlockSpec((1, gw), lambda bb, h, s: (0, h)),
                  pl.BlockSpec((1, HG_HEAD), lambda bb, h, s: (0, 0))],
        out_specs=pl.BlockSpec((tb, gw), lambda bb, h, s: (bb * nsblk + s, h)),
        out_shape=jax.ShapeDtypeStruct((m, width), _BF16),
        scratch_shapes=[pltpu.VMEM((nh, HG_HEAD, HG_HEAD), _F32),
                        pltpu.VMEM((4 * nh, HG_CHUNK, HG_CHUNK), _BF16),
                        pltpu.VMEM((4 * nh, HG_CHUNK, HG_HEAD), _BF16),
                        pltpu.VMEM((4 * nh, HG_CHUNK, HG_HEAD), _BF16),
                        pltpu.VMEM((4 * nh, 8, HG_HEAD), _F32)],
        compiler_params=_params("parallel", "parallel", "arbitrary"),
        name="hgrn_core",
    )(proj, proj, proj, proj, lb.reshape(1, width), g_norm.reshape(1, HG_HEAD))


def _gelu_tanh(x):
    c0 = math.sqrt(2.0 / math.pi)
    return 0.5 * x * (1.0 + jnp.tanh(c0 * (x + 0.044715 * (x * x * x))))


def _lru_kernel(gate_ref, u_ref, cw_ref, cb_ref, wa_ref, ba_ref, wx_ref, bx_ref, lam_ref,
                o_ref, ext_ref, sa_ref, sx_ref, h_ref, *, ts, pad):
    width = u_ref.shape[1]
    blk = width // LRU_BLOCKS
    halo = 8

    @pl.when(pl.program_id(1) == 0)
    def _():
        ext_ref[0:halo, :] = jnp.zeros((halo, width), _F32)
        h_ref[...] = jnp.zeros_like(h_ref)

    @pl.when(pl.program_id(1) > 0)
    def _():
        ext_ref[0:halo, :] = ext_ref[ts:ts + halo, :]

    ext_ref[halo:halo + ts, :] = u_ref[...]
    uc = cb_ref[...] + cw_ref[0:1, :] * ext_ref[pl.ds(halo - 3, ts), :]
    for j in range(1, CONV_W):
        uc = uc + cw_ref[j:j + 1, :] * ext_ref[pl.ds(halo - 3 + j, ts), :]

    u16 = uc.astype(_BF16)
    r_parts = []
    i_parts = []
    for n in range(LRU_BLOCKS):
        ub = u16[:, n * blk:(n + 1) * blk]
        r_parts.append(_dot(ub, wa_ref[n].astype(_BF16)))
        i_parts.append(_dot(ub, wx_ref[n].astype(_BF16)))
    r = _sigmoid(jnp.concatenate(r_parts, axis=1) + ba_ref[...])
    ig = _sigmoid(jnp.concatenate(i_parts, axis=1) + bx_ref[...])

    lam = lam_ref[...]
    softplus = jnp.maximum(-lam, 0.0) + jnp.log(1.0 + jnp.exp(-jnp.abs(lam)))
    log_a = -LRU_C * r * softplus
    a = jnp.exp(log_a)
    inp = jnp.sqrt(jnp.maximum(1.0 - a * a, 0.0)) * (ig * uc)

    sa_ref[0:pad, :] = jnp.ones((pad, width), _F32)
    sx_ref[0:pad, :] = jnp.zeros((pad, width), _F32)
    shift = 1
    while shift < ts:
        sa_ref[pad:pad + ts, :] = a
        sx_ref[pad:pad + ts, :] = inp
        a_sh = sa_ref[pl.ds(pad - shift, ts), :]
        x_sh = sx_ref[pl.ds(pad - shift, ts), :]
        inp = a * x_sh + inp
        a = a * a_sh
        shift *= 2

    h = inp + a * h_ref[0:1, :]
    h_ref[0:1, :] = h[ts - 1:ts, :]
    o_ref[...] = (_gelu_tanh(gate_ref[...]) * h).astype(o_ref.dtype)


def lru_core(proj, conv_w, conv_b, w_a, b_a, w_x, b_x, lam, bsz, seq):
    m = proj.shape[0]
    width = proj.shape[1] // 2
    blk = width // LRU_BLOCKS
    ts = _tile(seq, 256)
    pad = ts // 2 if ts >= 16 else 8
    pad = max(pad, 8)
    nsblk = seq // ts
    vec = lambda t: t.reshape(1, width)
    row_spec = pl.BlockSpec((1, width), lambda bb, s: (0, 0))
    w_spec = pl.BlockSpec((LRU_BLOCKS, blk, blk), lambda bb, s: (0, 0, 0))
    return pl.pallas_call(
        functools.partial(_lru_kernel, ts=ts, pad=pad),
        grid=(bsz, nsblk),
        in_specs=[pl.BlockSpec((ts, width), lambda bb, s: (bb * nsblk + s, 0)),
                  pl.BlockSpec((ts, width), lambda bb, s: (bb * nsblk + s, 1)),
                  pl.BlockSpec((CONV_W, width), lambda bb, s: (0, 0)),
                  row_spec, w_spec, row_spec, w_spec, row_spec, row_spec],
        out_specs=pl.BlockSpec((ts, width), lambda bb, s: (bb * nsblk + s, 0)),
        out_shape=jax.ShapeDtypeStruct((m, width), _BF16),
        scratch_shapes=[pltpu.VMEM((ts + 8, width), _F32),
                        pltpu.VMEM((ts + pad, width), _F32),
                        pltpu.VMEM((ts + pad, width), _F32),
                        pltpu.VMEM((8, width), _F32)],
        compiler_params=_params("parallel", "arbitrary"),
        name="lru_core",
    )(proj, proj, conv_w, vec(conv_b), w_a, vec(b_a), w_x, vec(b_x), vec(lam))


def _softcap(t):
    return GATE_CAP * jnp.tanh(t / GATE_CAP)


def _log_sigmoid(x):
    return jnp.minimum(x, 0.0) - jnp.log(1.0 + jnp.exp(-jnp.abs(x)))


def _mlstm_kernel(q_ref, k_ref, v_ref, og_ref, gt_ref, bif_ref, gn_ref, o_ref,
                  c_ref, n_ref, m_ref, *, nchunk, c, dqk, dv, nh):
    h_base = pl.program_id(1) * nh

    @pl.when(pl.program_id(2) == 0)
    def _():
        c_ref[...] = jnp.zeros_like(c_ref)
        n_ref[...] = jnp.zeros_like(n_ref)
        m_ref[...] = jnp.zeros_like(m_ref)

    row = lax.broadcasted_iota(jnp.int32, (c, c), 0)
    col = lax.broadcasted_iota(jnp.int32, (c, c), 1)
    causal = row >= col
    tril = causal.astype(_BF16)
    eye = (lax.broadcasted_iota(jnp.int32, (2 * ML_HEADS, LANE), 0)
           == lax.broadcasted_iota(jnp.int32, (2 * ML_HEADS, LANE), 1)).astype(_BF16)
    sub16 = lax.broadcasted_iota(jnp.int32, (2 * ML_HEADS, c), 0)
    lane1 = lax.broadcasted_iota(jnp.int32, (1, LANE), 1)
    gn = gn_ref[...]
    bif = bif_ref[...]
    kscale = dqk ** -0.5

    def head_chunk(r0, hh, gl, cum, gl_rows, cum_rows):
        h_id = h_base + hh
        qs = slice(hh * dqk, (hh + 1) * dqk)
        vs = slice(hh * dv, (hh + 1) * dv)
        q = q_ref[pl.ds(r0, c), qs]
        k = k_ref[pl.ds(r0, c), qs] * kscale
        v = v_ref[pl.ds(r0, c), vs]
        og = og_ref[pl.ds(r0, c), vs]
        b_col = jnp.sum(jnp.where(lane1 == h_id + ML_HEADS, cum, 0.0), axis=-1, keepdims=True)
        i_col = jnp.sum(jnp.where(lane1 == h_id, gl, 0.0), axis=-1, keepdims=True)
        b_row = jnp.sum(jnp.where(sub16 == h_id + ML_HEADS, cum_rows, 0.0), axis=0, keepdims=True)
        i_row = jnp.sum(jnp.where(sub16 == h_id, gl_rows, 0.0), axis=0, keepdims=True)

        m_st = m_ref[hh]
        c_st = c_ref[hh]
        n_st = n_ref[hh]

        d_mat = jnp.where(causal, b_col - b_row + i_row, NEG_BIG)
        inter = b_col + m_st
        m_t = jnp.maximum(inter, jnp.max(d_mat, axis=-1, keepdims=True))
        w_intra = jnp.where(causal, jnp.exp(jnp.minimum(d_mat - m_t, 0.0)), 0.0)
        w_inter = jnp.exp(inter - m_t)
        q16 = q.astype(_BF16)
        k16 = k.astype(_BF16)
        v16 = v.astype(_BF16)
        qk = _dot_nt(q16, k16) * w_intra
        num = _dot(qk.astype(_BF16), v16) + w_inter * _dot(q16, c_st.astype(_BF16))
        den = jnp.sum(qk, axis=-1, keepdims=True) + w_inter * jnp.sum(q * n_st, axis=-1, keepdims=True)
        hval = num / jnp.maximum(jnp.abs(den), jnp.exp(-m_t))

        g_tot = b_col[c - 1:c, :]
        upd_col = g_tot - b_col + i_col
        m_new = jnp.maximum(g_tot + m_st, jnp.max(upd_col, axis=0, keepdims=True))
        w_upd = jnp.exp(upd_col - m_new)
        decay = jnp.exp(g_tot + m_st - m_new)
        kw = k * w_upd
        c_ref[hh] = decay * c_st + _dot_tn(kw.astype(_BF16), v16)
        n_ref[hh] = decay * n_st + jnp.sum(kw, axis=0, keepdims=True)
        m_ref[hh] = m_new

        hn = hval * lax.rsqrt(jnp.mean(hval * hval, axis=-1, keepdims=True) + EPS) * gn
        o_ref[pl.ds(r0, c), vs] = (hn * _sigmoid(og)).astype(o_ref.dtype)

    def chunk(ci, carry):
        r0 = pl.multiple_of(ci * c, c)
        gt = jnp.where(lane1 < 2 * ML_HEADS, gt_ref[pl.ds(r0, c), :], 0.0)
        pre = _softcap(gt + bif)
        is_f = jnp.logical_and(lane1 >= ML_HEADS, lane1 < 2 * ML_HEADS)
        gl = jnp.where(is_f, _log_sigmoid(pre), pre)
        cum = _dot_exact_lhs(tril, gl)
        ch, cm, cl = _split3(cum)
        cum_rows = _dot_nt(eye, ch) + _dot_nt(eye, cm) + _dot_nt(eye, cl)
        gh, gm, gl3 = _split3(gl)
        gl_rows = _dot_nt(eye, gh) + _dot_nt(eye, gm) + _dot_nt(eye, gl3)
        for hh in range(nh):
            head_chunk(r0, hh, gl, cum, gl_rows, cum_rows)
        return carry

    lax.fori_loop(0, nchunk, chunk, 0)


def mlstm_core(proj, b_if, g_norm, bsz, seq, dqk, dv):
    m = proj.shape[0]
    heads = ML_HEADS
    hv = heads * dv
    c = _tile(seq, ML_CHUNK)
    tb = _tile(seq, 512)
    nsblk = seq // tb
    gate_blk = (2 * heads * dqk + 2 * hv) // LANE
    bif = jnp.zeros((1, LANE), _F32).at[0, :2 * heads].set(b_if.reshape(-1))
    nh = ML_GROUP
    ngrp = heads // nh
    qw = nh * dqk
    vw = nh * dv
    v_off = (2 * heads * dqk) // vw
    return pl.pallas_call(
        functools.partial(_mlstm_kernel, nchunk=tb // c, c=c, dqk=dqk, dv=dv, nh=nh),
        grid=(bsz, ngrp, nsblk),
        in_specs=[pl.BlockSpec((tb, qw), lambda bb, h, s: (bb * nsblk + s, h)),
                  pl.BlockSpec((tb, qw), lambda bb, h, s: (bb * nsblk + s, ngrp + h)),
                  pl.BlockSpec((tb, vw), lambda bb, h, s: (bb * nsblk + s, v_off + h)),
                  pl.BlockSpec((tb, vw), lambda bb, h, s: (bb * nsblk + s, v_off + ngrp + h)),
                  pl.BlockSpec((tb, LANE), lambda bb, h, s: (bb * nsblk + s, gate_blk)),
                  pl.BlockSpec((1, LANE), lambda bb, h, s: (0, 0)),
                  pl.BlockSpec((1, dv), lambda bb, h, s: (0, 0))],
        out_specs=pl.BlockSpec((tb, vw), lambda bb, h, s: (bb * nsblk + s, h)),
        out_shape=jax.ShapeDtypeStruct((m, hv), _BF16),
        scratch_shapes=[pltpu.VMEM((nh, dqk, dv), _F32),
                        pltpu.VMEM((nh, 1, dqk), _F32),
                        pltpu.VMEM((nh, 1, 1), _F32)],
        compiler_params=_params("parallel", "parallel", "arbitrary"),
        name="mlstm_core",
    )(proj, proj, proj, proj, proj, bif, g_norm.reshape(1, dv))


def _xattn_kernel(xn_ref, wq_ref, k_ref, v_ref, wo_ref, r_ref, g_ref, x_ref, n_ref, *, heads):
    d = xn_ref.shape[1]
    hd = d // heads
    scale = hd ** -0.5
    sub = xn_ref.shape[0] // XA_SPLIT
    for r in range(XA_SPLIT):
        rows = slice(r * sub, (r + 1) * sub)
        q = _dot(xn_ref[rows, :], wq_ref[...]).astype(_BF16)
        outs = []
        for h in range(heads):
            hs = slice(h * hd, (h + 1) * hd)
            s = _dot_nt(q[:, hs], k_ref[:, hs]) * scale
            s = s - jnp.max(s, axis=-1, keepdims=True)
            e = jnp.exp(s)
            p = e / jnp.sum(e, axis=-1, keepdims=True)
            outs.append(_dot(p.astype(_BF16), v_ref[:, hs]).astype(_BF16))
        x = r_ref[rows, :] + _dot(jnp.concatenate(outs, axis=1), wo_ref[...])
        x_ref[rows, :] = x
        n_ref[rows, :] = _rms(x, g_ref[...]).astype(n_ref.dtype)


def xattn_layer(xn, wq16, kv, wo16, resid, g, bsz, seq, n_mem):
    m, d = xn.shape
    tq = _tile(seq, 512)
    nq = seq // tq
    row_spec = pl.BlockSpec((tq, d), lambda i: (i, 0))
    w_spec = pl.BlockSpec((d, d), lambda i: (0, 0), pipeline_mode=pl.Buffered(1))
    return pl.pallas_call(
        functools.partial(_xattn_kernel, heads=XA_HEADS),
        grid=(bsz * nq,),
        in_specs=[row_spec, w_spec,
                  pl.BlockSpec((n_mem, d), lambda i: (i // nq, 0)),
                  pl.BlockSpec((n_mem, d), lambda i: (i // nq, 1)),
                  w_spec, row_spec,
                  pl.BlockSpec((1, d), lambda i: (0, 0))],
        out_specs=[row_spec, row_spec],
        out_shape=[jax.ShapeDtypeStruct((m, d), _F32),
                   jax.ShapeDtypeStruct((m, d), _BF16)],
        compiler_params=_params("parallel"),
        name="xattn_layer",
    )(xn, wq16, kv, kv, wo16, resid, g.reshape(1, d))


def kernel(x, mem, mem_norm_g, norm_g, final_norm_g, ffn_w_gu, ffn_w_down, xa_w_q, xa_w_kv, xa_w_o,
           hg_lb_param, hg_w_in, hg_g_norm, hg_w_out,
           lru_w_in, lru_conv_w, lru_conv_b, lru_w_a, lru_b_a, lru_w_x, lru_b_x, lru_lambda, lru_w_out,
           ml_w_in, ml_b_if, ml_g_norm, ml_w_out):
    bsz, seq, d = x.shape
    n_mem = mem.shape[1]
    depth = norm_g.shape[0]
    m = bsz * seq
    ml_dqk = d // (2 * ML_HEADS)
    ml_dv = d // ML_HEADS

    xf = x.reshape(m, d)
    mem_n = rms_cast(mem.reshape(bsz * n_mem, d), mem_norm_g)
    lb_all = hgrn_lower_bounds(hg_lb_param)

    xn = rms_cast(xf, norm_g[0, 0])
    for layer in range(depth):
        kind = layer % N_MIXERS
        idx = layer // N_MIXERS

        h = mm_swiglu(xn, ffn_w_gu, (layer, 0))
        xf, xn = mm_down_norm(h, cast_weight(ffn_w_down, (layer, 0)), xf, norm_g[layer, 1], 0.5)

        if kind == 0:
            proj = mm_ws(xn, hg_w_in, (idx,), _F32, name="hg_in")
            y = hgrn_core(proj, lb_all[layer], hg_g_norm[idx], bsz, seq)
            w_out = hg_w_out
        elif kind == 1:
            proj = mm_ws(xn, lru_w_in, (idx,), _F32, name="lru_in")
            y = lru_core(proj, lru_conv_w[idx], lru_conv_b[idx], lru_w_a[idx], lru_b_a[idx],
                         lru_w_x[idx], lru_b_x[idx], lru_lambda[idx], bsz, seq)
            w_out = lru_w_out
        else:
            proj = mm_ws(xn, ml_w_in, (idx,), _F32, tn=1280, name="ml_in")
            y = mlstm_core(proj, ml_b_if[idx], ml_g_norm[idx], bsz, seq, ml_dqk, ml_dv)
            w_out = ml_w_out
        xf, xn = mm_resid_norm(y, w_out, (idx,), xf, norm_g[layer, 2], 1.0, name="mixer_out")

        kv = mm_ws(mem_n, xa_w_kv, (layer,), _BF16, name="xa_kv")
        xf, xn = xattn_layer(xn, cast_weight(xa_w_q, (layer,)), kv, cast_weight(xa_w_o, (layer,)),
                             xf, norm_g[layer, 3], bsz, seq, n_mem)

        h = mm_swiglu(xn, ffn_w_gu, (layer, 1))
        w16 = cast_weight(ffn_w_down, (layer, 1))
        if layer + 1 < depth:
            xf, xn = mm_down_norm(h, w16, xf, norm_g[layer + 1, 0], 0.5)
        else:
            _, out = mm_down_norm(h, w16, xf, final_norm_g, 0.5, emit_x=False, norm_dtype=_F32,
                                  name="ffn_down_final")
    return out.reshape(bsz, seq, d)
```

```python
import functools
import math

import jax
import jax.numpy as jnp
from jax import lax
from jax.experimental import pallas as pl
from jax.experimental.pallas import tpu as pltpu

EPS = 1e-6
N_MIXERS = 3
HG_HEAD = 128
HG_CHUNK = 64
HG_SUB = 16
HG_GROUP = 8
HG_SAFE_LOG2 = 100.0
F_FLOOR = 1e-12
LRU_BLOCKS = 8
CONV_W = 4
LRU_C = 8.0
ML_HEADS = 8
ML_CHUNK = 256
ML_GROUP = 8
GATE_CAP = 15.0
NEG_BIG = -1e30
XA_HEADS = 4
LANE = 128
FF_TILE = 512
SWIGLU_SPLIT = 4
EPILOGUE_SPLIT = 4
DOWN_ROWS = 512
DOWN_SPLIT = 2
XA_SPLIT = 2
CAST_ROWS = 128
VMEM_LIMIT = 56 * 1024 * 1024

_BF16 = jnp.bfloat16
_F32 = jnp.float32


def _params(*sem):
    return pltpu.CompilerParams(dimension_semantics=sem, vmem_limit_bytes=VMEM_LIMIT)


def _tile(dim, pref):
    t = min(dim, pref)
    while dim % t:
        t //= 2
    return t


def _sigmoid(x):
    return 1.0 / (1.0 + jnp.exp(-x))


def _dot(a, b):
    return jnp.dot(a, b, preferred_element_type=_F32)


def _dot_nt(a, b):
    return lax.dot_general(a, b, (((1,), (1,)), ((), ())), preferred_element_type=_F32)


def _dot_tn(a, b):
    return lax.dot_general(a, b, (((0,), (0,)), ((), ())), preferred_element_type=_F32)


def _split3(x):
    hi = x.astype(_BF16)
    r1 = x - hi.astype(_F32)
    mid = r1.astype(_BF16)
    lo = (r1 - mid.astype(_F32)).astype(_BF16)
    return hi, mid, lo


def _dot_exact_lhs(sel, x):
    hi, mid, lo = _split3(x)
    return _dot(sel, hi) + _dot(sel, mid) + _dot(sel, lo)


def _rms(x, g):
    return x * lax.rsqrt(jnp.mean(x * x, axis=-1, keepdims=True) + EPS) * g


def _rms_kernel(x_ref, g_ref, o_ref):
    o_ref[...] = _rms(x_ref[...], g_ref[...]).astype(o_ref.dtype)


def rms_cast(x, g, out_dtype=_BF16):
    m, d = x.shape
    tm = _tile(m, 512)
    return pl.pallas_call(
        _rms_kernel,
        grid=(m // tm,),
        in_specs=[pl.BlockSpec((tm, d), lambda i: (i, 0)),
                  pl.BlockSpec((1, d), lambda i: (0, 0))],
        out_specs=pl.BlockSpec((tm, d), lambda i: (i, 0)),
        out_shape=jax.ShapeDtypeStruct((m, d), out_dtype),
        compiler_params=_params("parallel"),
        name="rms_cast",
    )(x, g.reshape(1, d))


def _wspec(widx, block, index_map, **kw):
    lead = tuple(widx)
    return pl.BlockSpec((None,) * len(lead) + tuple(block), lambda *g: lead + tuple(index_map(*g)), **kw)


def _mm_ws_kernel(a_ref, w_ref, o_ref, wbf_ref):
    @pl.when(pl.program_id(1) == 0)
    def _():
        wbf_ref[...] = w_ref[...].astype(_BF16)

    o_ref[...] = _dot(a_ref[...], wbf_ref[...]).astype(o_ref.dtype)


def mm_ws(a, w, widx, out_dtype, tm=1024, tn=1024, name="mm_ws"):
    m, k = a.shape
    n = w.shape[-1]
    tm = _tile(m, tm)
    nj = -(-n // tn)
    return pl.pallas_call(
        _mm_ws_kernel,
        grid=(nj, m // tm),
        in_specs=[pl.BlockSpec((tm, k), lambda j, i: (i, 0)),
                  _wspec(widx, (k, tn), lambda j, i: (0, j))],
        out_specs=pl.BlockSpec((tm, tn), lambda j, i: (i, j)),
        out_shape=jax.ShapeDtypeStruct((m, nj * tn), out_dtype),
        scratch_shapes=[pltpu.VMEM((k, tn), _BF16)],
        compiler_params=_params("arbitrary", "arbitrary"),
        name=name,
    )(a, w)


def _swiglu_kernel(a_ref, wg_ref, wu_ref, *rest, tnh, rem, ncast):
    cast_in = rest[:ncast]
    o_ref = rest[ncast]
    cast_out = rest[ncast + 1:2 * ncast + 1]
    wbf_ref = rest[2 * ncast + 1]
    lead = (0,) * (len(wg_ref.shape) - 2)
    first_row_block = pl.program_id(1) == 0
    is_last = pl.program_id(0) == pl.num_programs(0) - 1
    sub = a_ref.shape[0] // SWIGLU_SPLIT

    def compute(width):
        for src, dst in zip(cast_in, cast_out):
            dst[...] = src[...].astype(dst.dtype)
        for r in range(SWIGLU_SPLIT):
            rows = slice(r * sub, (r + 1) * sub)
            acc = _dot(a_ref[rows, :], wbf_ref[:, :2 * width])
            gate = acc[:, :width]
            up = acc[:, width:]
            o_ref[rows, :width] = (gate * _sigmoid(gate) * up).astype(o_ref.dtype)

    if rem == tnh:
        @pl.when(first_row_block)
        def _():
            wbf_ref[:, :tnh] = wg_ref[lead].astype(_BF16)
            wbf_ref[:, tnh:] = wu_ref[lead].astype(_BF16)

        compute(tnh)
        return

    @pl.when(jnp.logical_and(first_row_block, jnp.logical_not(is_last)))
    def _():
        wbf_ref[:, :tnh] = wg_ref[lead].astype(_BF16)
        wbf_ref[:, tnh:] = wu_ref[lead].astype(_BF16)

    @pl.when(jnp.logical_and(first_row_block, is_last))
    def _():
        wbf_ref[:, :rem] = wg_ref[lead][:, tnh - rem:].astype(_BF16)
        wbf_ref[:, rem:2 * rem] = wu_ref[lead][:, tnh - rem:].astype(_BF16)

    @pl.when(jnp.logical_not(is_last))
    def _():
        compute(tnh)

    @pl.when(is_last)
    def _():
        compute(rem)


def mm_swiglu(a, w_gu, widx, casts=(), tnh=FF_TILE):
    m, k = a.shape
    d_ff = w_gu.shape[-1] // 2
    tm = _tile(m, 1024)
    nj = -(-d_ff // tnh)
    ni = m // tm

    cast_in_specs, cast_out_specs, cast_out_shapes = [], [], []
    first_step = 0
    for w, w_lead in casts:
        rows, cols = w.shape[-2:]
        nchunk = rows // CAST_ROWS
        assert nchunk * CAST_ROWS == rows

        def chunk(j, i, first=first_step, last=nchunk - 1):
            return (jnp.clip(j * ni + i - first, 0, last), 0)

        cast_in_specs.append(_wspec(w_lead, (CAST_ROWS, cols), chunk))
        cast_out_specs.append(pl.BlockSpec((CAST_ROWS, cols), chunk))
        cast_out_shapes.append(jax.ShapeDtypeStruct((rows, cols), _BF16))
        first_step += nchunk
    assert first_step <= nj * ni
    ncast = len(casts)

    lead = tuple(widx)
    lead_blk = (1,) * len(lead)

    assert d_ff % LANE == 0 and tnh % LANE == 0

    def start(j, base=0):
        return LANE * (base // LANE + jnp.minimum(j * (tnh // LANE), (d_ff - tnh) // LANE))

    def elem_spec(block, index_map):
        return pl.BlockSpec(tuple(pl.Element(b) for b in block), index_map)

    res = pl.pallas_call(
        functools.partial(_swiglu_kernel, tnh=tnh, rem=d_ff - (nj - 1) * tnh, ncast=ncast),
        grid=(nj, ni),
        in_specs=[pl.BlockSpec((tm, k), lambda j, i: (i, 0)),
                  elem_spec(lead_blk + (k, tnh), lambda j, i: lead + (0, start(j))),
                  elem_spec(lead_blk + (k, tnh), lambda j, i: lead + (0, start(j, d_ff)))] + cast_in_specs,
        out_specs=[pl.BlockSpec((tm, tnh), lambda j, i: (i, j))] + cast_out_specs,
        out_shape=[jax.ShapeDtypeStruct((m, d_ff), _BF16)] + cast_out_shapes,
        scratch_shapes=[pltpu.VMEM((k, 2 * tnh), _BF16)],
        compiler_params=_params("arbitrary", "arbitrary"),
        name="mm_swiglu",
    )(a, w_gu, w_gu, *[w for w, _ in casts])
    return res[0], list(res[1:])


def _down_norm_kernel(a_ref, w_ref, r_ref, g_ref, *out_refs, scale, emit_x):
    n_ref = out_refs[-1]
    sub = a_ref.shape[0] // DOWN_SPLIT
    for r in range(DOWN_SPLIT):
        rows = slice(r * sub, (r + 1) * sub)
        x = r_ref[rows, :] + scale * _dot(a_ref[rows, :], w_ref[...])
        if emit_x:
            out_refs[0][rows, :] = x
        n_ref[rows, :] = _rms(x, g_ref[...]).astype(n_ref.dtype)


def mm_down_norm(a, w16, resid, g, scale, *, emit_x=True, norm_dtype=_BF16, name="ffn_down"):
    m, k = a.shape
    d = w16.shape[-1]
    tm = _tile(m, DOWN_ROWS)
    row_spec = pl.BlockSpec((tm, d), lambda i: (i, 0))
    out_shape = [jax.ShapeDtypeStruct((m, d), norm_dtype)]
    if emit_x:
        out_shape.insert(0, jax.ShapeDtypeStruct((m, d), _F32))
    res = pl.pallas_call(
        functools.partial(_down_norm_kernel, scale=scale, emit_x=emit_x),
        grid=(m // tm,),
        in_specs=[pl.BlockSpec((tm, k), lambda i: (i, 0)),
                  pl.BlockSpec((k, d), lambda i: (0, 0), pipeline_mode=pl.Buffered(1)),
                  row_spec,
                  pl.BlockSpec((1, d), lambda i: (0, 0))],
        out_specs=[row_spec] * len(out_shape),
        out_shape=out_shape,
        compiler_params=_params("parallel"),
        name=name,
    )(a, w16, resid, g.reshape(1, d))
    return (res[0], res[1]) if emit_x else (None, res[0])


def _resid_norm_kernel(a_ref, w_ref, r_ref, g_ref, x_ref, n_ref, wbf_ref, *, scale):
    @pl.when(pl.program_id(0) == 0)
    def _():
        wbf_ref[...] = w_ref[...].astype(_BF16)

    sub = a_ref.shape[0] // EPILOGUE_SPLIT
    for r in range(EPILOGUE_SPLIT):
        rows = slice(r * sub, (r + 1) * sub)
        x = r_ref[rows, :] + scale * _dot(a_ref[rows, :], wbf_ref[...])
        x_ref[rows, :] = x
        n_ref[rows, :] = _rms(x, g_ref[...]).astype(n_ref.dtype)


def mm_resid_norm(a, w, widx, resid, g, scale, name="mm_resid"):
    m, k = a.shape
    d = w.shape[-1]
    tm = _tile(m, 512)
    return pl.pallas_call(
        functools.partial(_resid_norm_kernel, scale=scale),
        grid=(m // tm,),
        in_specs=[pl.BlockSpec((tm, k), lambda i: (i, 0)),
                  _wspec(widx, (k, d), lambda i: (0, 0), pipeline_mode=pl.Buffered(1)),
                  pl.BlockSpec((tm, d), lambda i: (i, 0)),
                  pl.BlockSpec((1, d), lambda i: (0, 0))],
        out_specs=[pl.BlockSpec((tm, d), lambda i: (i, 0)),
                   pl.BlockSpec((tm, d), lambda i: (i, 0))],
        out_shape=[jax.ShapeDtypeStruct((m, d), _F32),
                   jax.ShapeDtypeStruct((m, d), _BF16)],
        scratch_shapes=[pltpu.VMEM((k, d), _BF16)],
        compiler_params=_params("arbitrary"),
        name=name,
    )(a, w, resid, g.reshape(1, d))


def _lb_kernel(p_ref, o_ref):
    p = p_ref[...]
    depth = p.shape[0]
    e = jnp.exp(p - jnp.max(p, axis=0, keepdims=True))
    sm = e / jnp.sum(e, axis=0, keepdims=True)
    run = jnp.zeros_like(sm[0:1])
    for layer in range(depth):
        run = run + sm[layer:layer + 1]
        o_ref[layer:layer + 1, :] = run - sm[0:1]


def hgrn_lower_bounds(p):
    return pl.pallas_call(
        _lb_kernel,
        out_shape=jax.ShapeDtypeStruct(p.shape, _F32),
        name="hgrn_lb",
    )(p)


def _hgrn_kernel(q_ref, f_ref, v_ref, g_ref, lb_ref, gn_ref, o_ref,
                 st_ref, sc_ref, qs_ref, ks_ref, dl_ref, *, nchunk, nh):
    c = HG_CHUNK
    sb = HG_SUB
    d = HG_HEAD
    ru = 8

    @pl.when(pl.program_id(2) == 0)
    def _():
        st_ref[...] = jnp.zeros_like(st_ref)

    gn = gn_ref[...]
    row = lax.broadcasted_iota(jnp.int32, (c, c), 0)
    col = lax.broadcasted_iota(jnp.int32, (c, c), 1)
    future = col > row
    lane_u = lax.broadcasted_iota(jnp.int32, (ru, c), 1)
    row3 = lax.broadcasted_iota(jnp.int32, (c, 3 * c), 0)
    col3 = lax.broadcasted_iota(jnp.int32, (c, 3 * c), 1)
    tril3 = (row3 >= (col3 & (c - 1))).astype(_BF16)

    def gates(r0, hh):
        cs = slice(hh * d, (hh + 1) * d)
        lb = lb_ref[:, cs]
        qp = q_ref[pl.ds(r0, c), cs]
        fp = f_ref[pl.ds(r0, c), cs]
        q = qp * _sigmoid(qp)
        e_abs = jnp.exp(-jnp.abs(fp))
        r_abs = 1.0 / (1.0 + e_abs)
        er_abs = e_abs * r_abs
        pos = fp >= 0.0
        f = lb + (1.0 - lb) * jnp.where(pos, r_abs, er_abs)
        k = (1.0 - lb) * jnp.where(pos, er_abs, r_abs)
        lf = jnp.log2(jnp.maximum(f, F_FLOOR))
        return q, k, lf

    def cumsum_rows(lf):
        return _dot(tril3, jnp.concatenate(_split3(lf), axis=0))

    def block_ref(b, lo):
        return b[lo - 1:lo, :] if lo > 0 else jnp.zeros((1, d), _F32)

    def prepare(r0, slot0):
        qkl = [gates(r0, hh) for hh in range(nh)]
        b_all = cumsum_rows(jnp.concatenate([lf for _, _, lf in qkl], axis=1))
        risks = []
        for hh, (q, k, _) in enumerate(qkl):
            slot = slot0 + hh
            b = b_all[:, hh * d:(hh + 1) * d]
            b_last = b[c - 1:c, :]
            risk = jnp.zeros((1, d), _F32)
            q_rows = []
            k_cols = []
            nblk = c // sb
            for bi in range(nblk):
                lo = bi * sb
                hi = lo + sb
                ref = block_ref(b, lo)
                risk = jnp.maximum(risk, ref - b[hi - 1:hi, :])
                qt = (q[lo:hi, :] * jnp.exp2(b[lo:hi, :] - ref)).astype(_BF16)
                kt = (k[:hi, :] * jnp.exp2(ref - b[:hi, :])).astype(_BF16)
                if hi < c:
                    kt = jnp.concatenate([kt, jnp.zeros((c - hi, d), _BF16)], axis=0)
                zero = jnp.zeros((sb, d), _BF16)
                q_rows.append(jnp.concatenate([zero] * bi + [qt] + [zero] * (nblk - 1 - bi), axis=1))
                k_cols.append(kt)
            scores = _dot_nt(jnp.concatenate(q_rows, axis=0), jnp.concatenate(k_cols, axis=1))
            sc_ref[slot] = jnp.where(future, 0.0, scores).astype(_BF16)
            qs_ref[slot] = (q * jnp.exp2(b)).astype(_BF16)
            ks_ref[slot] = (k * jnp.exp2(b_last - b)).astype(_BF16)
            dl_ref[slot] = jnp.broadcast_to(jnp.exp2(b_last), (8, d))
            risks.append(jnp.max(risk))
        return risks

    def direct_scores(r0, hh, slot):
        q, k, lf = gates(r0, hh)
        b = cumsum_rows(lf)
        bk = b - jnp.log2(k)
        units = []
        for bi in range(c // sb):
            lo = bi * sb
            if bi > 0:
                ref = block_ref(b, lo)
                qt = q[lo:lo + sb, :] * jnp.exp2(b[lo:lo + sb, :] - ref)
                kt = k[:lo, :] * jnp.exp2(ref - b[:lo, :])
                kt = jnp.concatenate([kt, jnp.zeros((c - lo, d), _F32)], axis=0)
                off = _dot_nt(qt.astype(_BF16), kt.astype(_BF16))
            for uu in range(sb // ru):
                t0 = lo + uu * ru
                b_u = b[t0:t0 + ru, :]
                q_u = q[t0:t0 + ru, :]
                sc = off[uu * ru:(uu + 1) * ru, :] if bi > 0 else jnp.zeros((ru, c), _F32)
                for s in range(lo, t0 + ru):
                    w = q_u * jnp.exp2(b_u - bk[s:s + 1, :])
                    sc = jnp.where(lane_u == s, jnp.sum(w, axis=-1, keepdims=True), sc)
                units.append(sc)
        sc_ref[slot] = jnp.where(future, 0.0, jnp.concatenate(units, axis=0)).astype(_BF16)

    def finish(r0, hh, slot):
        cs = slice(hh * d, (hh + 1) * d)
        v16 = v_ref[pl.ds(r0, c), cs].astype(_BF16)
        gp = g_ref[pl.ds(r0, c), cs]
        st = st_ref[hh]
        o = _dot(sc_ref[slot], v16) + _dot_nt(qs_ref[slot], st.astype(_BF16))
        st_ref[hh] = st * dl_ref[slot, 0:1, :] + _dot_tn(v16, ks_ref[slot])
        on = o * lax.rsqrt(jnp.mean(o * o, axis=-1, keepdims=True) + EPS) * gn
        o_ref[pl.ds(r0, c), cs] = (on * (gp * _sigmoid(gp))).astype(o_ref.dtype)

    def pair_work(pi, base):
        starts = [(2 * pi + half) * c for half in range(2)]
        starts = [s if isinstance(s, int) else pl.multiple_of(s, c) for s in starts]
        return [(starts[half], hh, base + half * nh + hh) for half in range(2) for hh in range(nh)]

    def prepare_pair(pi, base):
        work = pair_work(pi, base)
        risks = prepare(work[0][0], base) + prepare(work[nh][0], base + nh)
        worst = functools.reduce(jnp.maximum, risks)

        @pl.when(worst > HG_SAFE_LOG2)
        def _():
            for (r0, hh, slot), risk in zip(work, risks):
                @pl.when(risk > HG_SAFE_LOG2)
                def _():
                    direct_scores(r0, hh, slot)

    def finish_pair(pi, base):
        for r0, hh, slot in pair_work(pi, base):
            finish(r0, hh, slot)

    assert nchunk % 2 == 0
    npair = nchunk // 2
    nslot = 2 * nh
    prepare_pair(0, 0)

    def trip(pi, carry):
        base = (pi % 2) * nslot
        finish_pair(pi, base)
        prepare_pair(pi + 1, nslot - base)
        return carry

    lax.fori_loop(0, npair - 1, trip, 0)
    finish_pair(npair - 1, ((npair - 1) % 2) * nslot)


def hgrn_core(proj, lb, g_norm, bsz, seq):
    m = proj.shape[0]
    width = proj.shape[1] // 4
    heads = width // HG_HEAD
    nh = HG_GROUP
    gw = nh * HG_HEAD
    ngrp = heads // nh
    tb = _tile(seq, 1024)
    nsblk = seq // tb

    def spec(off):
        return pl.BlockSpec((tb, gw), lambda bb, h, s: (bb * nsblk + s, off * ngrp + h))

    return pl.pallas_call(
        functools.partial(_hgrn_kernel, nchunk=tb // HG_CHUNK, nh=nh),
        grid=(bsz, ngrp, nsblk),
        in_specs=[spec(0), spec(1), spec(2), spec(3),
                  pl.BlockSpec((1, gw), lambda bb, h, s: (0, h)),
                  pl.BlockSpec((1, HG_HEAD), lambda bb, h, s: (0, 0))],
        out_specs=pl.BlockSpec((tb, gw), lambda bb, h, s: (bb * nsblk + s, h)),
        out_shape=jax.ShapeDtypeStruct((m, width), _BF16),
        scratch_shapes=[pltpu.VMEM((nh, HG_HEAD, HG_HEAD), _F32),
                        pltpu.VMEM((4 * nh, HG_CHUNK, HG_CHUNK), _BF16),
                        pltpu.VMEM((4 * nh, HG_CHUNK, HG_HEAD), _BF16),
                        pltpu.VMEM((4 * nh, HG_CHUNK, HG_HEAD), _BF16),
                        pltpu.VMEM((4 * nh, 8, HG_HEAD), _F32)],
        compiler_params=_params("parallel", "parallel", "arbitrary"),
        name="hgrn_core",
    )(proj, proj, proj, proj, lb.reshape(1, width), g_norm.reshape(1, HG_HEAD))


def _gelu_tanh(x):
    c0 = math.sqrt(2.0 / math.pi)
    return 0.5 * x * (1.0 + jnp.tanh(c0 * (x + 0.044715 * (x * x * x))))


def _lru_kernel(gate_ref, u_ref, cw_ref, cb_ref, wa_ref, ba_ref, wx_ref, bx_ref, lam_ref,
                o_ref, ext_ref, sa_ref, sx_ref, h_ref, *, ts, pad):
    width = u_ref.shape[1]
    blk = width // LRU_BLOCKS
    halo = 8

    @pl.when(pl.program_id(1) == 0)
    def _():
        ext_ref[0:halo, :] = jnp.zeros((halo, width), _F32)
        h_ref[...] = jnp.zeros_like(h_ref)

    @pl.when(pl.program_id(1) > 0)
    def _():
        ext_ref[0:halo, :] = ext_ref[ts:ts + halo, :]

    ext_ref[halo:halo + ts, :] = u_ref[...]
    uc = cb_ref[...] + cw_ref[0:1, :] * ext_ref[pl.ds(halo - 3, ts), :]
    for j in range(1, CONV_W):
        uc = uc + cw_ref[j:j + 1, :] * ext_ref[pl.ds(halo - 3 + j, ts), :]

    u16 = uc.astype(_BF16)
    r_parts = []
    i_parts = []
    for n in range(LRU_BLOCKS):
        ub = u16[:, n * blk:(n + 1) * blk]
        r_parts.append(_dot(ub, wa_ref[n].astype(_BF16)))
        i_parts.append(_dot(ub, wx_ref[n].astype(_BF16)))
    r = _sigmoid(jnp.concatenate(r_parts, axis=1) + ba_ref[...])
    ig = _sigmoid(jnp.concatenate(i_parts, axis=1) + bx_ref[...])

    lam = lam_ref[...]
    softplus = jnp.maximum(-lam, 0.0) + jnp.log(1.0 + jnp.exp(-jnp.abs(lam)))
    log_a = -LRU_C * r * softplus
    a = jnp.exp(log_a)
    inp = jnp.sqrt(jnp.maximum(1.0 - a * a, 0.0)) * (ig * uc)

    sa_ref[0:pad, :] = jnp.ones((pad, width), _F32)
    sx_ref[0:pad, :] = jnp.zeros((pad, width), _F32)
    shift = 1
    while shift < ts:
        sa_ref[pad:pad + ts, :] = a
        sx_ref[pad:pad + ts, :] = inp
        a_sh = sa_ref[pl.ds(pad - shift, ts), :]
        x_sh = sx_ref[pl.ds(pad - shift, ts), :]
        inp = a * x_sh + inp
        a = a * a_sh
        shift *= 2

    h = inp + a * h_ref[0:1, :]
    h_ref[0:1, :] = h[ts - 1:ts, :]
    o_ref[...] = (_gelu_tanh(gate_ref[...]) * h).astype(o_ref.dtype)


def lru_core(proj, conv_w, conv_b, w_a, b_a, w_x, b_x, lam, bsz, seq):
    m = proj.shape[0]
    width = proj.shape[1] // 2
    blk = width // LRU_BLOCKS
    ts = _tile(seq, 256)
    pad = ts // 2 if ts >= 16 else 8
    pad = max(pad, 8)
    nsblk = seq // ts
    vec = lambda t: t.reshape(1, width)
    row_spec = pl.BlockSpec((1, width), lambda bb, s: (0, 0))
    w_spec = pl.BlockSpec((LRU_BLOCKS, blk, blk), lambda bb, s: (0, 0, 0))
    return pl.pallas_call(
        functools.partial(_lru_kernel, ts=ts, pad=pad),
        grid=(bsz, nsblk),
        in_specs=[pl.BlockSpec((ts, width), lambda bb, s: (bb * nsblk + s, 0)),
                  pl.BlockSpec((ts, width), lambda bb, s: (bb * nsblk + s, 1)),
                  pl.BlockSpec((CONV_W, width), lambda bb, s: (0, 0)),
                  row_spec, w_spec, row_spec, w_spec, row_spec, row_spec],
        out_specs=pl.BlockSpec((ts, width), lambda bb, s: (bb * nsblk + s, 0)),
        out_shape=jax.ShapeDtypeStruct((m, width), _BF16),
        scratch_shapes=[pltpu.VMEM((ts + 8, width), _F32),
                        pltpu.VMEM((ts + pad, width), _F32),
                        pltpu.VMEM((ts + pad, width), _F32),
                        pltpu.VMEM((8, width), _F32)],
        compiler_params=_params("parallel", "arbitrary"),
        name="lru_core",
    )(proj, proj, conv_w, vec(conv_b), w_a, vec(b_a), w_x, vec(b_x), vec(lam))


def _softcap(t):
    return GATE_CAP * jnp.tanh(t / GATE_CAP)


def _log_sigmoid(x):
    return jnp.minimum(x, 0.0) - jnp.log(1.0 + jnp.exp(-jnp.abs(x)))


def _mlstm_kernel(q_ref, k_ref, v_ref, og_ref, gt_ref, bif_ref, gn_ref, o_ref,
                  c_ref, n_ref, m_ref, *, nchunk, c, dqk, dv, nh):
    h_base = pl.program_id(1) * nh

    @pl.when(pl.program_id(2) == 0)
    def _():
        c_ref[...] = jnp.zeros_like(c_ref)
        n_ref[...] = jnp.zeros_like(n_ref)
        m_ref[...] = jnp.zeros_like(m_ref)

    row = lax.broadcasted_iota(jnp.int32, (c, c), 0)
    col = lax.broadcasted_iota(jnp.int32, (c, c), 1)
    causal = row >= col
    tril = causal.astype(_BF16)
    eye = (lax.broadcasted_iota(jnp.int32, (2 * ML_HEADS, LANE), 0)
           == lax.broadcasted_iota(jnp.int32, (2 * ML_HEADS, LANE), 1)).astype(_BF16)
    sub16 = lax.broadcasted_iota(jnp.int32, (2 * ML_HEADS, c), 0)
    lane1 = lax.broadcasted_iota(jnp.int32, (1, LANE), 1)
    gn = gn_ref[...]
    bif = bif_ref[...]
    kscale = dqk ** -0.5

    def head_chunk(r0, hh, gl, cum, gl_rows, cum_rows):
        h_id = h_base + hh
        qs = slice(hh * dqk, (hh + 1) * dqk)
        vs = slice(hh * dv, (hh + 1) * dv)
        q = q_ref[pl.ds(r0, c), qs]
        k = k_ref[pl.ds(r0, c), qs] * kscale
        v = v_ref[pl.ds(r0, c), vs]
        og = og_ref[pl.ds(r0, c), vs]
        b_col = jnp.sum(jnp.where(lane1 == h_id + ML_HEADS, cum, 0.0), axis=-1, keepdims=True)
        i_col = jnp.sum(jnp.where(lane1 == h_id, gl, 0.0), axis=-1, keepdims=True)
        b_row = jnp.sum(jnp.where(sub16 == h_id + ML_HEADS, cum_rows, 0.0), axis=0, keepdims=True)
        i_row = jnp.sum(jnp.where(sub16 == h_id, gl_rows, 0.0), axis=0, keepdims=True)

        m_st = m_ref[hh]
        c_st = c_ref[hh]
        n_st = n_ref[hh]

        d_mat = jnp.where(causal, b_col - b_row + i_row, NEG_BIG)
        inter = b_col + m_st
        m_t = jnp.maximum(inter, jnp.max(d_mat, axis=-1, keepdims=True))
        w_intra = jnp.where(causal, jnp.exp(jnp.minimum(d_mat - m_t, 0.0)), 0.0)
        w_inter = jnp.exp(inter - m_t)
        q16 = q.astype(_BF16)
        k16 = k.astype(_BF16)
        v16 = v.astype(_BF16)
        qk = _dot_nt(q16, k16) * w_intra
        num = _dot(qk.astype(_BF16), v16) + w_inter * _dot(q16, c_st.astype(_BF16))
        den = jnp.sum(qk, axis=-1, keepdims=True) + w_inter * jnp.sum(q * n_st, axis=-1, keepdims=True)
        hval = num / jnp.maximum(jnp.abs(den), jnp.exp(-m_t))

        g_tot = b_col[c - 1:c, :]
        upd_col = g_tot - b_col + i_col
        m_new = jnp.maximum(g_tot + m_st, jnp.max(upd_col, axis=0, keepdims=True))
        w_upd = jnp.exp(upd_col - m_new)
        decay = jnp.exp(g_tot + m_st - m_new)
        kw = k * w_upd
        c_ref[hh] = decay * c_st + _dot_tn(kw.astype(_BF16), v16)
        n_ref[hh] = decay * n_st + jnp.sum(kw, axis=0, keepdims=True)
        m_ref[hh] = m_new

        hn = hval * lax.rsqrt(jnp.mean(hval * hval, axis=-1, keepdims=True) + EPS) * gn
        o_ref[pl.ds(r0, c), vs] = (hn * _sigmoid(og)).astype(o_ref.dtype)

    def chunk(ci, carry):
        r0 = pl.multiple_of(ci * c, c)
        gt = jnp.where(lane1 < 2 * ML_HEADS, gt_ref[pl.ds(r0, c), :], 0.0)
        pre = _softcap(gt + bif)
        is_f = jnp.logical_and(lane1 >= ML_HEADS, lane1 < 2 * ML_HEADS)
        gl = jnp.where(is_f, _log_sigmoid(pre), pre)
        cum = _dot_exact_lhs(tril, gl)
        ch, cm, cl = _split3(cum)
        cum_rows = _dot_nt(eye, ch) + _dot_nt(eye, cm) + _dot_nt(eye, cl)
        gh, gm, gl3 = _split3(gl)
        gl_rows = _dot_nt(eye, gh) + _dot_nt(eye, gm) + _dot_nt(eye, gl3)
        for hh in range(nh):
            head_chunk(r0, hh, gl, cum, gl_rows, cum_rows)
        return carry

    lax.fori_loop(0, nchunk, chunk, 0)


def mlstm_core(proj, b_if, g_norm, bsz, seq, dqk, dv):
    m = proj.shape[0]
    heads = ML_HEADS
    hv = heads * dv
    c = _tile(seq, ML_CHUNK)
    tb = _tile(seq, 512)
    nsblk = seq // tb
    gate_blk = (2 * heads * dqk + 2 * hv) // LANE
    bif = jnp.zeros((1, LANE), _F32).at[0, :2 * heads].set(b_if.reshape(-1))
    nh = ML_GROUP
    ngrp = heads // nh
    qw = nh * dqk
    vw = nh * dv
    v_off = (2 * heads * dqk) // vw
    return pl.pallas_call(
        functools.partial(_mlstm_kernel, nchunk=tb // c, c=c, dqk=dqk, dv=dv, nh=nh),
        grid=(bsz, ngrp, nsblk),
        in_specs=[pl.BlockSpec((tb, qw), lambda bb, h, s: (bb * nsblk + s, h)),
                  pl.BlockSpec((tb, qw), lambda bb, h, s: (bb * nsblk + s, ngrp + h)),
                  pl.BlockSpec((tb, vw), lambda bb, h, s: (bb * nsblk + s, v_off + h)),
                  pl.BlockSpec((tb, vw), lambda bb, h, s: (bb * nsblk + s, v_off + ngrp + h)),
                  pl.BlockSpec((tb, LANE), lambda bb, h, s: (bb * nsblk + s, gate_blk)),
                  pl.BlockSpec((1, LANE), lambda bb, h, s: (0, 0)),
                  pl.BlockSpec((1, dv), lambda bb, h, s: (0, 0))],
        out_specs=pl.BlockSpec((tb, vw), lambda bb, h, s: (bb * nsblk + s, h)),
        out_shape=jax.ShapeDtypeStruct((m, hv), _BF16),
        scratch_shapes=[pltpu.VMEM((nh, dqk, dv), _F32),
                        pltpu.VMEM((nh, 1, dqk), _F32),
                        pltpu.VMEM((nh, 1, 1), _F32)],
        compiler_params=_params("parallel", "parallel", "arbitrary"),
        name="mlstm_core",
    )(proj, proj, proj, proj, proj, bif, g_norm.reshape(1, dv))


def _xattn_kernel(xn_ref, wq_ref, k_ref, v_ref, wo_ref, r_ref, g_ref, x_ref, n_ref, *, heads):
    d = xn_ref.shape[1]
    hd = d // heads
    scale = hd ** -0.5
    sub = xn_ref.shape[0] // XA_SPLIT
    for r in range(XA_SPLIT):
        rows = slice(r * sub, (r + 1) * sub)
        q = _dot(xn_ref[rows, :], wq_ref[...]).astype(_BF16)
        outs = []
        for h in range(heads):
            hs = slice(h * hd, (h + 1) * hd)
            s = _dot_nt(q[:, hs], k_ref[:, hs]) * scale
            s = s - jnp.max(s, axis=-1, keepdims=True)
            e = jnp.exp(s)
            p = e / jnp.sum(e, axis=-1, keepdims=True)
            outs.append(_dot(p.astype(_BF16), v_ref[:, hs]).astype(_BF16))
        x = r_ref[rows, :] + _dot(jnp.concatenate(outs, axis=1), wo_ref[...])
        x_ref[rows, :] = x
        n_ref[rows, :] = _rms(x, g_ref[...]).astype(n_ref.dtype)


def xattn_layer(xn, wq16, kv, wo16, resid, g, bsz, seq, n_mem):
    m, d = xn.shape
    tq = _tile(seq, 512)
    nq = seq // tq
    row_spec = pl.BlockSpec((tq, d), lambda i: (i, 0))
    w_spec = pl.BlockSpec((d, d), lambda i: (0, 0), pipeline_mode=pl.Buffered(1))
    return pl.pallas_call(
        functools.partial(_xattn_kernel, heads=XA_HEADS),
        grid=(bsz * nq,),
        in_specs=[row_spec, w_spec,
                  pl.BlockSpec((n_mem, d), lambda i: (i // nq, 0)),
                  pl.BlockSpec((n_mem, d), lambda i: (i // nq, 1)),
                  w_spec, row_spec,
                  pl.BlockSpec((1, d), lambda i: (0, 0))],
        out_specs=[row_spec, row_spec],
        out_shape=[jax.ShapeDtypeStruct((m, d), _F32),
                   jax.ShapeDtypeStruct((m, d), _BF16)],
        compiler_params=_params("parallel"),
        name="xattn_layer",
    )(xn, wq16, kv, kv, wo16, resid, g.reshape(1, d))


def kernel(x, mem, mem_norm_g, norm_g, final_norm_g, ffn_w_gu, ffn_w_down, xa_w_q, xa_w_kv, xa_w_o,
           hg_lb_param, hg_w_in, hg_g_norm, hg_w_out,
           lru_w_in, lru_conv_w, lru_conv_b, lru_w_a, lru_b_a, lru_w_x, lru_b_x, lru_lambda, lru_w_out,
           ml_w_in, ml_b_if, ml_g_norm, ml_w_out):
    bsz, seq, d = x.shape
    n_mem = mem.shape[1]
    depth = norm_g.shape[0]
    m = bsz * seq
    ml_dqk = d // (2 * ML_HEADS)
    ml_dv = d // ML_HEADS

    xf = x.reshape(m, d)
    mem_n = rms_cast(mem.reshape(bsz * n_mem, d), mem_norm_g)
    lb_all = hgrn_lower_bounds(hg_lb_param)

    xn = rms_cast(xf, norm_g[0, 0])
    for layer in range(depth):
        kind = layer % N_MIXERS
        idx = layer // N_MIXERS

        h, (w_down16, w_q16, w_o16) = mm_swiglu(
            xn, ffn_w_gu, (layer, 0),
            casts=[(ffn_w_down, (layer, 0)), (xa_w_q, (layer,)), (xa_w_o, (layer,))])
        xf, xn = mm_down_norm(h, w_down16, xf, norm_g[layer, 1], 0.5)

        if kind == 0:
            proj = mm_ws(xn, hg_w_in, (idx,), _F32, name="hg_in")
            y = hgrn_core(proj, lb_all[layer], hg_g_norm[idx], bsz, seq)
            w_out = hg_w_out
        elif kind == 1:
            proj = mm_ws(xn, lru_w_in, (idx,), _F32, name="lru_in")
            y = lru_core(proj, lru_conv_w[idx], lru_conv_b[idx], lru_w_a[idx], lru_b_a[idx],
                         lru_w_x[idx], lru_b_x[idx], lru_lambda[idx], bsz, seq)
            w_out = lru_w_out
        else:
            proj = mm_ws(xn, ml_w_in, (idx,), _F32, tn=1280, name="ml_in")
            y = mlstm_core(proj, ml_b_if[idx], ml_g_norm[idx], bsz, seq, ml_dqk, ml_dv)
            w_out = ml_w_out
        xf, xn = mm_resid_norm(y, w_out, (idx,), xf, norm_g[layer, 2], 1.0, name="mixer_out")

        kv = mm_ws(mem_n, xa_w_kv, (layer,), _BF16, name="xa_kv")
        xf, xn = xattn_layer(xn, w_q16, kv, w_o16, xf, norm_g[layer, 3], bsz, seq, n_mem)

        h, (w_down16,) = mm_swiglu(xn, ffn_w_gu, (layer, 1), casts=[(ffn_w_down, (layer, 1))])
        if layer + 1 < depth:
            xf, xn = mm_down_norm(h, w_down16, xf, norm_g[layer + 1, 0], 0.5)
        else:
            _, out = mm_down_norm(h, w_down16, xf, final_norm_g, 0.5, emit_x=False, norm_dtype=_F32,
                                  name="ffn_down_final")
    return out.reshape(bsz, seq, d)
```

```python
import functools
import math

import jax
import jax.numpy as jnp
from jax import lax
from jax.experimental import pallas as pl
from jax.experimental.pallas import tpu as pltpu

EPS = 1e-6
N_MIXERS = 3
HG_HEAD = 128
HG_CHUNK = 64
HG_SUB = 16
HG_GROUP = 8
HG_SAFE_LOG2 = 100.0
F_FLOOR = 1e-12
LRU_BLOCKS = 8
CONV_W = 4
LRU_C = 8.0
ML_HEADS = 8
ML_CHUNK = 256
ML_GROUP = 8
GATE_CAP = 15.0
NEG_BIG = -1e30
XA_HEADS = 4
LANE = 128
FF_TILE = 512
SWIGLU_SPLIT = 1
EPILOGUE_SPLIT = 1
DOWN_ROWS = 512
DOWN_SPLIT = 1
XA_SPLIT = 2
CAST_ROWS = 128
VMEM_LIMIT = 56 * 1024 * 1024

_BF16 = jnp.bfloat16
_F32 = jnp.float32


def _params(*sem):
    return pltpu.CompilerParams(dimension_semantics=sem, vmem_limit_bytes=VMEM_LIMIT)


def _tile(dim, pref):
    t = min(dim, pref)
    while dim % t:
        t //= 2
    return t


def _sigmoid(x):
    return 1.0 / (1.0 + jnp.exp(-x))


def _dot(a, b):
    return jnp.dot(a, b, preferred_element_type=_F32)


def _dot_nt(a, b):
    return lax.dot_general(a, b, (((1,), (1,)), ((), ())), preferred_element_type=_F32)


def _dot_tn(a, b):
    return lax.dot_general(a, b, (((0,), (0,)), ((), ())), preferred_element_type=_F32)


def _split3(x):
    hi = x.astype(_BF16)
    r1 = x - hi.astype(_F32)
    mid = r1.astype(_BF16)
    lo = (r1 - mid.astype(_F32)).astype(_BF16)
    return hi, mid, lo


def _dot_exact_lhs(sel, x):
    hi, mid, lo = _split3(x)
    return _dot(sel, hi) + _dot(sel, mid) + _dot(sel, lo)


def _rms(x, g):
    return x * lax.rsqrt(jnp.mean(x * x, axis=-1, keepdims=True) + EPS) * g


def _rms_kernel(x_ref, g_ref, o_ref):
    o_ref[...] = _rms(x_ref[...], g_ref[...]).astype(o_ref.dtype)


def rms_cast(x, g, out_dtype=_BF16):
    m, d = x.shape
    tm = _tile(m, 512)
    return pl.pallas_call(
        _rms_kernel,
        grid=(m // tm,),
        in_specs=[pl.BlockSpec((tm, d), lambda i: (i, 0)),
                  pl.BlockSpec((1, d), lambda i: (0, 0))],
        out_specs=pl.BlockSpec((tm, d), lambda i: (i, 0)),
        out_shape=jax.ShapeDtypeStruct((m, d), out_dtype),
        compiler_params=_params("parallel"),
        name="rms_cast",
    )(x, g.reshape(1, d))


def _wspec(widx, block, index_map, **kw):
    lead = tuple(widx)
    return pl.BlockSpec((None,) * len(lead) + tuple(block), lambda *g: lead + tuple(index_map(*g)), **kw)


def _mm_ws_kernel(a_ref, w_ref, o_ref, wbf_ref):
    @pl.when(pl.program_id(1) == 0)
    def _():
        wbf_ref[...] = w_ref[...].astype(_BF16)

    o_ref[...] = _dot(a_ref[...], wbf_ref[...]).astype(o_ref.dtype)


def mm_ws(a, w, widx, out_dtype, tm=1024, tn=1024, name="mm_ws"):
    m, k = a.shape
    n = w.shape[-1]
    tm = _tile(m, tm)
    nj = -(-n // tn)
    return pl.pallas_call(
        _mm_ws_kernel,
        grid=(nj, m // tm),
        in_specs=[pl.BlockSpec((tm, k), lambda j, i: (i, 0)),
                  _wspec(widx, (k, tn), lambda j, i: (0, j))],
        out_specs=pl.BlockSpec((tm, tn), lambda j, i: (i, j)),
        out_shape=jax.ShapeDtypeStruct((m, nj * tn), out_dtype),
        scratch_shapes=[pltpu.VMEM((k, tn), _BF16)],
        compiler_params=_params("arbitrary", "arbitrary"),
        name=name,
    )(a, w)


def _swiglu_kernel(a_ref, wg_ref, wu_ref, *rest, tnh, rem, ncast):
    cast_in = rest[:ncast]
    o_ref = rest[ncast]
    cast_out = rest[ncast + 1:2 * ncast + 1]
    wbf_ref = rest[2 * ncast + 1]
    lead = (0,) * (len(wg_ref.shape) - 2)
    first_row_block = pl.program_id(1) == 0
    is_last = pl.program_id(0) == pl.num_programs(0) - 1
    sub = a_ref.shape[0] // SWIGLU_SPLIT

    def compute(width):
        for src, dst in zip(cast_in, cast_out):
            dst[...] = src[...].astype(dst.dtype)
        for r in range(SWIGLU_SPLIT):
            rows = slice(r * sub, (r + 1) * sub)
            acc = _dot(a_ref[rows, :], wbf_ref[:, :2 * width])
            gate = acc[:, :width]
            up = acc[:, width:]
            o_ref[rows, :width] = (gate * _sigmoid(gate) * up).astype(o_ref.dtype)

    if rem == tnh:
        @pl.when(first_row_block)
        def _():
            wbf_ref[:, :tnh] = wg_ref[lead].astype(_BF16)
            wbf_ref[:, tnh:] = wu_ref[lead].astype(_BF16)

        compute(tnh)
        return

    @pl.when(jnp.logical_and(first_row_block, jnp.logical_not(is_last)))
    def _():
        wbf_ref[:, :tnh] = wg_ref[lead].astype(_BF16)
        wbf_ref[:, tnh:] = wu_ref[lead].astype(_BF16)

    @pl.when(jnp.logical_and(first_row_block, is_last))
    def _():
        wbf_ref[:, :rem] = wg_ref[lead][:, tnh - rem:].astype(_BF16)
        wbf_ref[:, rem:2 * rem] = wu_ref[lead][:, tnh - rem:].astype(_BF16)

    @pl.when(jnp.logical_not(is_last))
    def _():
        compute(tnh)

    @pl.when(is_last)
    def _():
        compute(rem)


def mm_swiglu(a, w_gu, widx, casts=(), tnh=FF_TILE):
    m, k = a.shape
    d_ff = w_gu.shape[-1] // 2
    tm = _tile(m, 1024)
    nj = -(-d_ff // tnh)
    ni = m // tm

    cast_in_specs, cast_out_specs, cast_out_shapes = [], [], []
    first_step = 0
    for w, w_lead in casts:
        rows, cols = w.shape[-2:]
        nchunk = rows // CAST_ROWS
        assert nchunk * CAST_ROWS == rows

        def chunk(j, i, first=first_step, last=nchunk - 1):
            return (jnp.clip(j * ni + i - first, 0, last), 0)

        cast_in_specs.append(_wspec(w_lead, (CAST_ROWS, cols), chunk))
        cast_out_specs.append(pl.BlockSpec((CAST_ROWS, cols), chunk))
        cast_out_shapes.append(jax.ShapeDtypeStruct((rows, cols), _BF16))
        first_step += nchunk
    assert first_step <= nj * ni
    ncast = len(casts)

    lead = tuple(widx)
    lead_blk = (1,) * len(lead)

    assert d_ff % LANE == 0 and tnh % LANE == 0

    def start(j, base=0):
        return LANE * (base // LANE + jnp.minimum(j * (tnh // LANE), (d_ff - tnh) // LANE))

    def elem_spec(block, index_map):
        return pl.BlockSpec(tuple(pl.Element(b) for b in block), index_map)

    res = pl.pallas_call(
        functools.partial(_swiglu_kernel, tnh=tnh, rem=d_ff - (nj - 1) * tnh, ncast=ncast),
        grid=(nj, ni),
        in_specs=[pl.BlockSpec((tm, k), lambda j, i: (i, 0)),
                  elem_spec(lead_blk + (k, tnh), lambda j, i: lead + (0, start(j))),
                  elem_spec(lead_blk + (k, tnh), lambda j, i: lead + (0, start(j, d_ff)))] + cast_in_specs,
        out_specs=[pl.BlockSpec((tm, tnh), lambda j, i: (i, j))] + cast_out_specs,
        out_shape=[jax.ShapeDtypeStruct((m, d_ff), _BF16)] + cast_out_shapes,
        scratch_shapes=[pltpu.VMEM((k, 2 * tnh), _BF16)],
        compiler_params=_params("arbitrary", "arbitrary"),
        name="mm_swiglu",
    )(a, w_gu, w_gu, *[w for w, _ in casts])
    return res[0], list(res[1:])


def _down_norm_kernel(a_ref, w_ref, r_ref, g_ref, *out_refs, scale, emit_x):
    n_ref = out_refs[-1]
    sub = a_ref.shape[0] // DOWN_SPLIT
    for r in range(DOWN_SPLIT):
        rows = slice(r * sub, (r + 1) * sub)
        x = r_ref[rows, :] + scale * _dot(a_ref[rows, :], w_ref[...])
        if emit_x:
            out_refs[0][rows, :] = x
        n_ref[rows, :] = _rms(x, g_ref[...]).astype(n_ref.dtype)


def mm_down_norm(a, w16, resid, g, scale, *, emit_x=True, norm_dtype=_BF16, name="ffn_down"):
    m, k = a.shape
    d = w16.shape[-1]
    tm = _tile(m, DOWN_ROWS)
    row_spec = pl.BlockSpec((tm, d), lambda i: (i, 0))
    out_shape = [jax.ShapeDtypeStruct((m, d), norm_dtype)]
    if emit_x:
        out_shape.insert(0, jax.ShapeDtypeStruct((m, d), _F32))
    res = pl.pallas_call(
        functools.partial(_down_norm_kernel, scale=scale, emit_x=emit_x),
        grid=(m // tm,),
        in_specs=[pl.BlockSpec((tm, k), lambda i: (i, 0)),
                  pl.BlockSpec((k, d), lambda i: (0, 0), pipeline_mode=pl.Buffered(1)),
                  row_spec,
                  pl.BlockSpec((1, d), lambda i: (0, 0))],
        out_specs=[row_spec] * len(out_shape),
        out_shape=out_shape,
        compiler_params=_params("parallel"),
        name=name,
    )(a, w16, resid, g.reshape(1, d))
    return (res[0], res[1]) if emit_x else (None, res[0])


def _resid_norm_kernel(a_ref, w_ref, r_ref, g_ref, x_ref, n_ref, wbf_ref, *, scale):
    @pl.when(pl.program_id(0) == 0)
    def _():
        wbf_ref[...] = w_ref[...].astype(_BF16)

    sub = a_ref.shape[0] // EPILOGUE_SPLIT
    for r in range(EPILOGUE_SPLIT):
        rows = slice(r * sub, (r + 1) * sub)
        x = r_ref[rows, :] + scale * _dot(a_ref[rows, :], wbf_ref[...])
        x_ref[rows, :] = x
        n_ref[rows, :] = _rms(x, g_ref[...]).astype(n_ref.dtype)


def mm_resid_norm(a, w, widx, resid, g, scale, name="mm_resid"):
    m, k = a.shape
    d = w.shape[-1]
    tm = _tile(m, 512)
    return pl.pallas_call(
        functools.partial(_resid_norm_kernel, scale=scale),
        grid=(m // tm,),
        in_specs=[pl.BlockSpec((tm, k), lambda i: (i, 0)),
                  _wspec(widx, (k, d), lambda i: (0, 0), pipeline_mode=pl.Buffered(1)),
                  pl.BlockSpec((tm, d), lambda i: (i, 0)),
                  pl.BlockSpec((1, d), lambda i: (0, 0))],
        out_specs=[pl.BlockSpec((tm, d), lambda i: (i, 0)),
                   pl.BlockSpec((tm, d), lambda i: (i, 0))],
        out_shape=[jax.ShapeDtypeStruct((m, d), _F32),
                   jax.ShapeDtypeStruct((m, d), _BF16)],
        scratch_shapes=[pltpu.VMEM((k, d), _BF16)],
        compiler_params=_params("arbitrary"),
        name=name,
    )(a, w, resid, g.reshape(1, d))


def _lb_kernel(p_ref, o_ref):
    p = p_ref[...]
    depth = p.shape[0]
    e = jnp.exp(p - jnp.max(p, axis=0, keepdims=True))
    sm = e / jnp.sum(e, axis=0, keepdims=True)
    run = jnp.zeros_like(sm[0:1])
    for layer in range(depth):
        run = run + sm[layer:layer + 1]
        o_ref[layer:layer + 1, :] = run - sm[0:1]


def hgrn_lower_bounds(p):
    return pl.pallas_call(
        _lb_kernel,
        out_shape=jax.ShapeDtypeStruct(p.shape, _F32),
        name="hgrn_lb",
    )(p)


def _hgrn_kernel(q_ref, f_ref, v_ref, g_ref, lb_ref, gn_ref, o_ref,
                 st_ref, sc_ref, qs_ref, ks_ref, dl_ref, *, nchunk, nh):
    c = HG_CHUNK
    sb = HG_SUB
    d = HG_HEAD
    ru = 8

    @pl.when(pl.program_id(2) == 0)
    def _():
        st_ref[...] = jnp.zeros_like(st_ref)

    gn = gn_ref[...]
    row = lax.broadcasted_iota(jnp.int32, (c, c), 0)
    col = lax.broadcasted_iota(jnp.int32, (c, c), 1)
    future = col > row
    lane_u = lax.broadcasted_iota(jnp.int32, (ru, c), 1)
    row3 = lax.broadcasted_iota(jnp.int32, (c, 3 * c), 0)
    col3 = lax.broadcasted_iota(jnp.int32, (c, 3 * c), 1)
    tril3 = (row3 >= (col3 & (c - 1))).astype(_BF16)

    def gates(r0, hh):
        cs = slice(hh * d, (hh + 1) * d)
        lb = lb_ref[:, cs]
        qp = q_ref[pl.ds(r0, c), cs]
        fp = f_ref[pl.ds(r0, c), cs]
        q = qp * _sigmoid(qp)
        e_abs = jnp.exp(-jnp.abs(fp))
        r_abs = 1.0 / (1.0 + e_abs)
        er_abs = e_abs * r_abs
        pos = fp >= 0.0
        f = lb + (1.0 - lb) * jnp.where(pos, r_abs, er_abs)
        k = (1.0 - lb) * jnp.where(pos, er_abs, r_abs)
        lf = jnp.log2(jnp.maximum(f, F_FLOOR))
        return q, k, lf

    def cumsum_rows(lf):
        return _dot(tril3, jnp.concatenate(_split3(lf), axis=0))

    def block_ref(b, lo):
        return b[lo - 1:lo, :] if lo > 0 else jnp.zeros((1, d), _F32)

    def prepare(r0, slot0):
        qkl = [gates(r0, hh) for hh in range(nh)]
        b_all = cumsum_rows(jnp.concatenate([lf for _, _, lf in qkl], axis=1))
        risks = []
        for hh, (q, k, _) in enumerate(qkl):
            slot = slot0 + hh
            b = b_all[:, hh * d:(hh + 1) * d]
            b_last = b[c - 1:c, :]
            risk = jnp.zeros((1, d), _F32)
            q_rows = []
            k_cols = []
            nblk = c // sb
            for bi in range(nblk):
                lo = bi * sb
                hi = lo + sb
                ref = block_ref(b, lo)
                risk = jnp.maximum(risk, ref - b[hi - 1:hi, :])
                qt = (q[lo:hi, :] * jnp.exp2(b[lo:hi, :] - ref)).astype(_BF16)
                kt = (k[:hi, :] * jnp.exp2(ref - b[:hi, :])).astype(_BF16)
                if hi < c:
                    kt = jnp.concatenate([kt, jnp.zeros((c - hi, d), _BF16)], axis=0)
                zero = jnp.zeros((sb, d), _BF16)
                q_rows.append(jnp.concatenate([zero] * bi + [qt] + [zero] * (nblk - 1 - bi), axis=1))
                k_cols.append(kt)
            scores = _dot_nt(jnp.concatenate(q_rows, axis=0), jnp.concatenate(k_cols, axis=1))
            sc_ref[slot] = jnp.where(future, 0.0, scores).astype(_BF16)
            qs_ref[slot] = (q * jnp.exp2(b)).astype(_BF16)
            ks_ref[slot] = (k * jnp.exp2(b_last - b)).astype(_BF16)
            dl_ref[slot] = jnp.broadcast_to(jnp.exp2(b_last), (8, d))
            risks.append(jnp.max(risk))
        return risks

    def direct_scores(r0, hh, slot):
        q, k, lf = gates(r0, hh)
        b = cumsum_rows(lf)
        bk = b - jnp.log2(k)
        units = []
        for bi in range(c // sb):
            lo = bi * sb
            if bi > 0:
                ref = block_ref(b, lo)
                qt = q[lo:lo + sb, :] * jnp.exp2(b[lo:lo + sb, :] - ref)
                kt = k[:lo, :] * jnp.exp2(ref - b[:lo, :])
                kt = jnp.concatenate([kt, jnp.zeros((c - lo, d), _F32)], axis=0)
                off = _dot_nt(qt.astype(_BF16), kt.astype(_BF16))
            for uu in range(sb // ru):
                t0 = lo + uu * ru
                b_u = b[t0:t0 + ru, :]
                q_u = q[t0:t0 + ru, :]
                sc = off[uu * ru:(uu + 1) * ru, :] if bi > 0 else jnp.zeros((ru, c), _F32)
                for s in range(lo, t0 + ru):
                    w = q_u * jnp.exp2(b_u - bk[s:s + 1, :])
                    sc = jnp.where(lane_u == s, jnp.sum(w, axis=-1, keepdims=True), sc)
                units.append(sc)
        sc_ref[slot] = jnp.where(future, 0.0, jnp.concatenate(units, axis=0)).astype(_BF16)

    def finish(r0, hh, slot):
        cs = slice(hh * d, (hh + 1) * d)
        v16 = v_ref[pl.ds(r0, c), cs].astype(_BF16)
        gp = g_ref[pl.ds(r0, c), cs]
        st = st_ref[hh]
        o = _dot(sc_ref[slot], v16) + _dot_nt(qs_ref[slot], st.astype(_BF16))
        st_ref[hh] = st * dl_ref[slot, 0:1, :] + _dot_tn(v16, ks_ref[slot])
        on = o * lax.rsqrt(jnp.mean(o * o, axis=-1, keepdims=True) + EPS) * gn
        o_ref[pl.ds(r0, c), cs] = (on * (gp * _sigmoid(gp))).astype(o_ref.dtype)

    def pair_work(pi, base):
        starts = [(2 * pi + half) * c for half in range(2)]
        starts = [s if isinstance(s, int) else pl.multiple_of(s, c) for s in starts]
        return [(starts[half], hh, base + half * nh + hh) for half in range(2) for hh in range(nh)]

    def prepare_pair(pi, base):
        work = pair_work(pi, base)
        risks = prepare(work[0][0], base) + prepare(work[nh][0], base + nh)
        worst = functools.reduce(jnp.maximum, risks)

        @pl.when(worst > HG_SAFE_LOG2)
        def _():
            for (r0, hh, slot), risk in zip(work, risks):
                @pl.when(risk > HG_SAFE_LOG2)
                def _():
                    direct_scores(r0, hh, slot)

    def finish_pair(pi, base):
        for r0, hh, slot in pair_work(pi, base):
            finish(r0, hh, slot)

    assert nchunk % 2 == 0
    npair = nchunk // 2
    nslot = 2 * nh
    prepare_pair(0, 0)

    def trip(pi, carry):
        base = (pi % 2) * nslot
        finish_pair(pi, base)
        prepare_pair(pi + 1, nslot - base)
        return carry

    lax.fori_loop(0, npair - 1, trip, 0)
    finish_pair(npair - 1, ((npair - 1) % 2) * nslot)


def hgrn_core(proj, lb, g_norm, bsz, seq):
    m = proj.shape[0]
    width = proj.shape[1] // 4
    heads = width // HG_HEAD
    nh = HG_GROUP
    gw = nh * HG_HEAD
    ngrp = heads // nh
    tb = _tile(seq, 1024)
    nsblk = seq // tb

    def spec(off):
        return pl.BlockSpec((tb, gw), lambda bb, h, s: (bb * nsblk + s, off * ngrp + h))

    return pl.pallas_call(
        functools.partial(_hgrn_kernel, nchunk=tb // HG_CHUNK, nh=nh),
        grid=(bsz, ngrp, nsblk),
        in_specs=[spec(0), spec(1), spec(2), spec(3),
                  pl.BlockSpec((1, gw), lambda bb, h, s: (0, h)),
                  pl.BlockSpec((1, HG_HEAD), lambda bb, h, s: (0, 0))],
        out_specs=pl.BlockSpec((tb, gw), lambda bb, h, s: (bb * nsblk + s, h)),
        out_shape=jax.ShapeDtypeStruct((m, width), _BF16),
        scratch_shapes=[pltpu.VMEM((nh, HG_HEAD, HG_HEAD), _F32),
                        pltpu.VMEM((4 * nh, HG_CHUNK, HG_CHUNK), _BF16),
                        pltpu.VMEM((4 * nh, HG_CHUNK, HG_HEAD), _BF16),
                        pltpu.VMEM((4 * nh, HG_CHUNK, HG_HEAD), _BF16),
                        pltpu.VMEM((4 * nh, 8, HG_HEAD), _F32)],
        compiler_params=_params("parallel", "parallel", "arbitrary"),
        name="hgrn_core",
    )(proj, proj, proj, proj, lb.reshape(1, width), g_norm.reshape(1, HG_HEAD))


def _gelu_tanh(x):
    c0 = math.sqrt(2.0 / math.pi)
    return 0.5 * x * (1.0 + jnp.tanh(c0 * (x + 0.044715 * (x * x * x))))


def _lru_kernel(gate_ref, u_ref, cw_ref, cb_ref, wa_ref, ba_ref, wx_ref, bx_ref, lam_ref,
                o_ref, ext_ref, sa_ref, sx_ref, h_ref, *, ts, pad):
    width = u_ref.shape[1]
    blk = width // LRU_BLOCKS
    halo = 8

    @pl.when(pl.program_id(1) == 0)
    def _():
        ext_ref[0:halo, :] = jnp.zeros((halo, width), _F32)
        h_ref[...] = jnp.zeros_like(h_ref)

    @pl.when(pl.program_id(1) > 0)
    def _():
        ext_ref[0:halo, :] = ext_ref[ts:ts + halo, :]

    ext_ref[halo:halo + ts, :] = u_ref[...]
    uc = cb_ref[...] + cw_ref[0:1, :] * ext_ref[pl.ds(halo - 3, ts), :]
    for j in range(1, CONV_W):
        uc = uc + cw_ref[j:j + 1, :] * ext_ref[pl.ds(halo - 3 + j, ts), :]

    u16 = uc.astype(_BF16)
    r_parts = []
    i_parts = []
    for n in range(LRU_BLOCKS):
        ub = u16[:, n * blk:(n + 1) * blk]
        r_parts.append(_dot(ub, wa_ref[n].astype(_BF16)))
        i_parts.append(_dot(ub, wx_ref[n].astype(_BF16)))
    r = _sigmoid(jnp.concatenate(r_parts, axis=1) + ba_ref[...])
    ig = _sigmoid(jnp.concatenate(i_parts, axis=1) + bx_ref[...])

    lam = lam_ref[...]
    softplus = jnp.maximum(-lam, 0.0) + jnp.log(1.0 + jnp.exp(-jnp.abs(lam)))
    log_a = -LRU_C * r * softplus
    a = jnp.exp(log_a)
    inp = jnp.sqrt(jnp.maximum(1.0 - a * a, 0.0)) * (ig * uc)

    sa_ref[0:pad, :] = jnp.ones((pad, width), _F32)
    sx_ref[0:pad, :] = jnp.zeros((pad, width), _F32)
    shift = 1
    while shift < ts:
        sa_ref[pad:pad + ts, :] = a
        sx_ref[pad:pad + ts, :] = inp
        a_sh = sa_ref[pl.ds(pad - shift, ts), :]
        x_sh = sx_ref[pl.ds(pad - shift, ts), :]
        inp = a * x_sh + inp
        a = a * a_sh
        shift *= 2

    h = inp + a * h_ref[0:1, :]
    h_ref[0:1, :] = h[ts - 1:ts, :]
    o_ref[...] = (_gelu_tanh(gate_ref[...]) * h).astype(o_ref.dtype)


def lru_core(proj, conv_w, conv_b, w_a, b_a, w_x, b_x, lam, bsz, seq):
    m = proj.shape[0]
    width = proj.shape[1] // 2
    blk = width // LRU_BLOCKS
    ts = _tile(seq, 256)
    pad = ts // 2 if ts >= 16 else 8
    pad = max(pad, 8)
    nsblk = seq // ts
    vec = lambda t: t.reshape(1, width)
    row_spec = pl.BlockSpec((1, width), lambda bb, s: (0, 0))
    w_spec = pl.BlockSpec((LRU_BLOCKS, blk, blk), lambda bb, s: (0, 0, 0))
    return pl.pallas_call(
        functools.partial(_lru_kernel, ts=ts, pad=pad),
        grid=(bsz, nsblk),
        in_specs=[pl.BlockSpec((ts, width), lambda bb, s: (bb * nsblk + s, 0)),
                  pl.BlockSpec((ts, width), lambda bb, s: (bb * nsblk + s, 1)),
                  pl.BlockSpec((CONV_W, width), lambda bb, s: (0, 0)),
                  row_spec, w_spec, row_spec, w_spec, row_spec, row_spec],
        out_specs=pl.BlockSpec((ts, width), lambda bb, s: (bb * nsblk + s, 0)),
        out_shape=jax.ShapeDtypeStruct((m, width), _BF16),
        scratch_shapes=[pltpu.VMEM((ts + 8, width), _F32),
                        pltpu.VMEM((ts + pad, width), _F32),
                        pltpu.VMEM((ts + pad, width), _F32),
                        pltpu.VMEM((8, width), _F32)],
        compiler_params=_params("parallel", "arbitrary"),
        name="lru_core",
    )(proj, proj, conv_w, vec(conv_b), w_a, vec(b_a), w_x, vec(b_x), vec(lam))


def _softcap(t):
    return GATE_CAP * jnp.tanh(t / GATE_CAP)


def _log_sigmoid(x):
    return jnp.minimum(x, 0.0) - jnp.log(1.0 + jnp.exp(-jnp.abs(x)))


def _mlstm_kernel(q_ref, k_ref, v_ref, og_ref, gt_ref, bif_ref, gn_ref, o_ref,
                  c_ref, n_ref, m_ref, *, nchunk, c, dqk, dv, nh):
    h_base = pl.program_id(1) * nh

    @pl.when(pl.program_id(2) == 0)
    def _():
        c_ref[...] = jnp.zeros_like(c_ref)
        n_ref[...] = jnp.zeros_like(n_ref)
        m_ref[...] = jnp.zeros_like(m_ref)

    row = lax.broadcasted_iota(jnp.int32, (c, c), 0)
    col = lax.broadcasted_iota(jnp.int32, (c, c), 1)
    causal = row >= col
    tril = causal.astype(_BF16)
    eye = (lax.broadcasted_iota(jnp.int32, (2 * ML_HEADS, LANE), 0)
           == lax.broadcasted_iota(jnp.int32, (2 * ML_HEADS, LANE), 1)).astype(_BF16)
    sub16 = lax.broadcasted_iota(jnp.int32, (2 * ML_HEADS, c), 0)
    lane1 = lax.broadcasted_iota(jnp.int32, (1, LANE), 1)
    gn = gn_ref[...]
    bif = bif_ref[...]
    kscale = dqk ** -0.5

    def head_chunk(r0, hh, gl, cum, gl_rows, cum_rows):
        h_id = h_base + hh
        qs = slice(hh * dqk, (hh + 1) * dqk)
        vs = slice(hh * dv, (hh + 1) * dv)
        q = q_ref[pl.ds(r0, c), qs]
        k = k_ref[pl.ds(r0, c), qs] * kscale
        v = v_ref[pl.ds(r0, c), vs]
        og = og_ref[pl.ds(r0, c), vs]
        b_col = jnp.sum(jnp.where(lane1 == h_id + ML_HEADS, cum, 0.0), axis=-1, keepdims=True)
        i_col = jnp.sum(jnp.where(lane1 == h_id, gl, 0.0), axis=-1, keepdims=True)
        b_row = jnp.sum(jnp.where(sub16 == h_id + ML_HEADS, cum_rows, 0.0), axis=0, keepdims=True)
        i_row = jnp.sum(jnp.where(sub16 == h_id, gl_rows, 0.0), axis=0, keepdims=True)

        m_st = m_ref[hh]
        c_st = c_ref[hh]
        n_st = n_ref[hh]

        d_mat = jnp.where(causal, b_col - b_row + i_row, NEG_BIG)
        inter = b_col + m_st
        m_t = jnp.maximum(inter, jnp.max(d_mat, axis=-1, keepdims=True))
        w_intra = jnp.where(causal, jnp.exp(jnp.minimum(d_mat - m_t, 0.0)), 0.0)
        w_inter = jnp.exp(inter - m_t)
        q16 = q.astype(_BF16)
        k16 = k.astype(_BF16)
        v16 = v.astype(_BF16)
        qk = _dot_nt(q16, k16) * w_intra
        num = _dot(qk.astype(_BF16), v16) + w_inter * _dot(q16, c_st.astype(_BF16))
        den = jnp.sum(qk, axis=-1, keepdims=True) + w_inter * jnp.sum(q * n_st, axis=-1, keepdims=True)
        hval = num / jnp.maximum(jnp.abs(den), jnp.exp(-m_t))

        g_tot = b_col[c - 1:c, :]
        upd_col = g_tot - b_col + i_col
        m_new = jnp.maximum(g_tot + m_st, jnp.max(upd_col, axis=0, keepdims=True))
        w_upd = jnp.exp(upd_col - m_new)
        decay = jnp.exp(g_tot + m_st - m_new)
        kw = k * w_upd
        c_ref[hh] = decay * c_st + _dot_tn(kw.astype(_BF16), v16)
        n_ref[hh] = decay * n_st + jnp.sum(kw, axis=0, keepdims=True)
        m_ref[hh] = m_new

        hn = hval * lax.rsqrt(jnp.mean(hval * hval, axis=-1, keepdims=True) + EPS) * gn
        o_ref[pl.ds(r0, c), vs] = (hn * _sigmoid(og)).astype(o_ref.dtype)

    def chunk(ci, carry):
        r0 = pl.multiple_of(ci * c, c)
        gt = jnp.where(lane1 < 2 * ML_HEADS, gt_ref[pl.ds(r0, c), :], 0.0)
        pre = _softcap(gt + bif)
        is_f = jnp.logical_and(lane1 >= ML_HEADS, lane1 < 2 * ML_HEADS)
        gl = jnp.where(is_f, _log_sigmoid(pre), pre)
        cum = _dot_exact_lhs(tril, gl)
        ch, cm, cl = _split3(cum)
        cum_rows = _dot_nt(eye, ch) + _dot_nt(eye, cm) + _dot_nt(eye, cl)
        gh, gm, gl3 = _split3(gl)
        gl_rows = _dot_nt(eye, gh) + _dot_nt(eye, gm) + _dot_nt(eye, gl3)
        for hh in range(nh):
            head_chunk(r0, hh, gl, cum, gl_rows, cum_rows)
        return carry

    lax.fori_loop(0, nchunk, chunk, 0)


def mlstm_core(proj, b_if, g_norm, bsz, seq, dqk, dv):
    m = proj.shape[0]
    heads = ML_HEADS
    hv = heads * dv
    c = _tile(seq, ML_CHUNK)
    tb = _tile(seq, 512)
    nsblk = seq // tb
    gate_blk = (2 * heads * dqk + 2 * hv) // LANE
    bif = jnp.zeros((1, LANE), _F32).at[0, :2 * heads].set(b_if.reshape(-1))
    nh = ML_GROUP
    ngrp = heads // nh
    qw = nh * dqk
    vw = nh * dv
    v_off = (2 * heads * dqk) // vw
    return pl.pallas_call(
        functools.partial(_mlstm_kernel, nchunk=tb // c, c=c, dqk=dqk, dv=dv, nh=nh),
        grid=(bsz, ngrp, nsblk),
        in_specs=[pl.BlockSpec((tb, qw), lambda bb, h, s: (bb * nsblk + s, h)),
                  pl.BlockSpec((tb, qw), lambda bb, h, s: (bb * nsblk + s, ngrp + h)),
                  pl.BlockSpec((tb, vw), lambda bb, h, s: (bb * nsblk + s, v_off + h)),
                  pl.BlockSpec((tb, vw), lambda bb, h, s: (bb * nsblk + s, v_off + ngrp + h)),
                  pl.BlockSpec((tb, LANE), lambda bb, h, s: (bb * nsblk + s, gate_blk)),
                  pl.BlockSpec((1, LANE), lambda bb, h, s: (0, 0)),
                  pl.BlockSpec((1, dv), lambda bb, h, s: (0, 0))],
        out_specs=pl.BlockSpec((tb, vw), lambda bb, h, s: (bb * nsblk + s, h)),
        out_shape=jax.ShapeDtypeStruct((m, hv), _BF16),
        scratch_shapes=[pltpu.VMEM((nh, dqk, dv), _F32),
                        pltpu.VMEM((nh, 1, dqk), _F32),
                        pltpu.VMEM((nh, 1, 1), _F32)],
        compiler_params=_params("parallel", "parallel", "arbitrary"),
        name="mlstm_core",
    )(proj, proj, proj, proj, proj, bif, g_norm.reshape(1, dv))


def _xattn_kernel(xn_ref, wq_ref, k_ref, v_ref, wo_ref, r_ref, g_ref, x_ref, n_ref, *, heads):
    d = xn_ref.shape[1]
    hd = d // heads
    scale = hd ** -0.5
    sub = xn_ref.shape[0] // XA_SPLIT
    for r in range(XA_SPLIT):
        rows = slice(r * sub, (r + 1) * sub)
        q = _dot(xn_ref[rows, :], wq_ref[...]).astype(_BF16)
        outs = []
        for h in range(heads):
            hs = slice(h * hd, (h + 1) * hd)
            s = _dot_nt(q[:, hs], k_ref[:, hs]) * scale
            s = s - jnp.max(s, axis=-1, keepdims=True)
            e = jnp.exp(s)
            p = e / jnp.sum(e, axis=-1, keepdims=True)
            outs.append(_dot(p.astype(_BF16), v_ref[:, hs]).astype(_BF16))
        x = r_ref[rows, :] + _dot(jnp.concatenate(outs, axis=1), wo_ref[...])
        x_ref[rows, :] = x
        n_ref[rows, :] = _rms(x, g_ref[...]).astype(n_ref.dtype)


def xattn_layer(xn, wq16, kv, wo16, resid, g, bsz, seq, n_mem):
    m, d = xn.shape
    tq = _tile(seq, 512)
    nq = seq // tq
    row_spec = pl.BlockSpec((tq, d), lambda i: (i, 0))
    w_spec = pl.BlockSpec((d, d), lambda i: (0, 0), pipeline_mode=pl.Buffered(1))
    return pl.pallas_call(
        functools.partial(_xattn_kernel, heads=XA_HEADS),
        grid=(bsz * nq,),
        in_specs=[row_spec, w_spec,
                  pl.BlockSpec((n_mem, d), lambda i: (i // nq, 0)),
                  pl.BlockSpec((n_mem, d), lambda i: (i // nq, 1)),
                  w_spec, row_spec,
                  pl.BlockSpec((1, d), lambda i: (0, 0))],
        out_specs=[row_spec, row_spec],
        out_shape=[jax.ShapeDtypeStruct((m, d), _F32),
                   jax.ShapeDtypeStruct((m, d), _BF16)],
        compiler_params=_params("parallel"),
        name="xattn_layer",
    )(xn, wq16, kv, kv, wo16, resid, g.reshape(1, d))


def kernel(x, mem, mem_norm_g, norm_g, final_norm_g, ffn_w_gu, ffn_w_down, xa_w_q, xa_w_kv, xa_w_o,
           hg_lb_param, hg_w_in, hg_g_norm, hg_w_out,
           lru_w_in, lru_conv_w, lru_conv_b, lru_w_a, lru_b_a, lru_w_x, lru_b_x, lru_lambda, lru_w_out,
           ml_w_in, ml_b_if, ml_g_norm, ml_w_out):
    bsz, seq, d = x.shape
    n_mem = mem.shape[1]
    depth = norm_g.shape[0]
    m = bsz * seq
    ml_dqk = d // (2 * ML_HEADS)
    ml_dv = d // ML_HEADS

    xf = x.reshape(m, d)
    mem_n = rms_cast(mem.reshape(bsz * n_mem, d), mem_norm_g)
    lb_all = hgrn_lower_bounds(hg_lb_param)

    xn = rms_cast(xf, norm_g[0, 0])
    for layer in range(depth):
        kind = layer % N_MIXERS
        idx = layer // N_MIXERS

        h, (w_down16, w_q16, w_o16) = mm_swiglu(
            xn, ffn_w_gu, (layer, 0),
            casts=[(ffn_w_down, (layer, 0)), (xa_w_q, (layer,)), (xa_w_o, (layer,))])
        xf, xn = mm_down_norm(h, w_down16, xf, norm_g[layer, 1], 0.5)

        if kind == 0:
            proj = mm_ws(xn, hg_w_in, (idx,), _F32, name="hg_in")
            y = hgrn_core(proj, lb_all[layer], hg_g_norm[idx], bsz, seq)
            w_out = hg_w_out
        elif kind == 1:
            proj = mm_ws(xn, lru_w_in, (idx,), _F32, name="lru_in")
            y = lru_core(proj, lru_conv_w[idx], lru_conv_b[idx], lru_w_a[idx], lru_b_a[idx],
                         lru_w_x[idx], lru_b_x[idx], lru_lambda[idx], bsz, seq)
            w_out = lru_w_out
        else:
            proj = mm_ws(xn, ml_w_in, (idx,), _F32, tn=1280, name="ml_in")
            y = mlstm_core(proj, ml_b_if[idx], ml_g_norm[idx], bsz, seq, ml_dqk, ml_dv)
            w_out = ml_w_out
        xf, xn = mm_resid_norm(y, w_out, (idx,), xf, norm_g[layer, 2], 1.0, name="mixer_out")

        kv = mm_ws(mem_n, xa_w_kv, (layer,), _BF16, name="xa_kv")
        xf, xn = xattn_layer(xn, w_q16, kv, w_o16, xf, norm_g[layer, 3], bsz, seq, n_mem)

        h, (w_down16,) = mm_swiglu(xn, ffn_w_gu, (layer, 1), casts=[(ffn_w_down, (layer, 1))])
        if layer + 1 < depth:
            xf, xn = mm_down_norm(h, w_down16, xf, norm_g[layer + 1, 0], 0.5)
        else:
            _, out = mm_down_norm(h, w_down16, xf, final_norm_g, 0.5, emit_x=False, norm_dtype=_F32,
                                  name="ffn_down_final")
    return out.reshape(bsz, seq, d)
```
